```python
import math
import jax, jax.numpy as jnp
from jax import lax
import numpy as np

D_MODEL = 1024
BATCH = 8
SEQ = 4096
DEPTH = 4

N_MIXERS = 2
N_SSM_LAYERS = (DEPTH + 1) // 2
N_ATT_LAYERS = DEPTH // 2
SSM_WIDTH = D_MODEL // 2
SSM_CH = 16
SSM_GROUPS = SSM_WIDTH // SSM_CH
SSM_STATE = 64
DT_MIN, DT_MAX = 1e-3, 1e-1
HEAD_DIM = 64
N_HEADS = D_MODEL // HEAD_DIM
N_KV_HEADS = N_HEADS // 4
KV_GROUP = N_HEADS // N_KV_HEADS
IDX_HEADS = max(4, D_MODEL // 128)
IDX_DIM = 64
TOPK_MAX = 256
Q_BLOCK = 128
ROPE_THETA = 10000.0
ATT_SPLITS = (N_HEADS * HEAD_DIM, N_KV_HEADS * HEAD_DIM, N_KV_HEADS * HEAD_DIM,
              IDX_HEADS * IDX_DIM, IDX_DIM, IDX_HEADS)
ATT_IN_COLS = sum(ATT_SPLITS)
D_FF = 4 * D_MODEL
NORM_EPS = 1e-6

kernel_name = "hybrid_s5_dsa_sandwich_trunk"


def rms_norm(x, g):
    xf = x.astype(jnp.float32)
    y = xf * lax.rsqrt(jnp.mean(xf * xf, axis=-1, keepdims=True) + NORM_EPS)
    return (y * g.astype(jnp.float32)).astype(x.dtype)


def rope_tables(seq_len, dim, dtype):
    half = dim // 2
    inv_freq = ROPE_THETA ** (-jnp.arange(half, dtype=jnp.float32) * 2.0 / dim)
    ang = jnp.arange(seq_len, dtype=jnp.float32)[:, None] * inv_freq[None, :]
    cos = jnp.concatenate([jnp.cos(ang), jnp.cos(ang)], axis=-1)
    sin = jnp.concatenate([jnp.sin(ang), jnp.sin(ang)], axis=-1)
    return cos.astype(dtype), sin.astype(dtype)


def apply_rope(x, cos, sin):
    half = x.shape[-1] // 2
    rot = jnp.concatenate([-x[..., half:], x[..., :half]], axis=-1)
    return x * cos[None, :, None, :] + rot * sin[None, :, None, :]


def s5_mixer(h, w_in, lam_re, lam_im, log_dt, b_re, b_im, c_re, c_im, d_skip, w_glu, w_out):
    bsz, seq, _ = h.shape
    dt_ = h.dtype
    u = (h @ w_in).reshape(bsz, seq, SSM_GROUPS, SSM_CH)
    lr = lam_re.astype(jnp.float32); li = lam_im.astype(jnp.float32)
    dt = jnp.exp(log_dt.astype(jnp.float32))[:, None]
    mag = jnp.exp(lr * dt)
    abar_re = mag * jnp.cos(li * dt)
    abar_im = mag * jnp.sin(li * dt)
    den = lr * lr + li * li
    nr = abar_re - 1.0
    ni = abar_im
    fr = (nr * lr + ni * li) / den
    fi = (ni * lr - nr * li) / den
    br = b_re.astype(jnp.float32); bi = b_im.astype(jnp.float32)
    bbar_re = (fr[..., None] * br - fi[..., None] * bi).astype(dt_)
    bbar_im = (fr[..., None] * bi + fi[..., None] * br).astype(dt_)
    bu_re = jnp.einsum('bsgc,gpc->bsgp', u, bbar_re)
    bu_im = jnp.einsum('bsgc,gpc->bsgp', u, bbar_im)
    a_re = jnp.broadcast_to(abar_re.astype(dt_), bu_re.shape)
    a_im = jnp.broadcast_to(abar_im.astype(dt_), bu_im.shape)

    def combine(left, right):
        a1r, a1i, b1r, b1i = left
        a2r, a2i, b2r, b2i = right
        return (a2r * a1r - a2i * a1i,
                a2r * a1i + a2i * a1r,
                a2r * b1r - a2i * b1i + b2r,
                a2r * b1i + a2i * b1r + b2i)

    _, _, xs_re, xs_im = lax.associative_scan(combine, (a_re, a_im, bu_re, bu_im), axis=1)
    y = (jnp.einsum('bsgp,gcp->bsgc', xs_re, c_re)
         - jnp.einsum('bsgp,gcp->bsgc', xs_im, c_im)
         + d_skip[None, None] * u)
    z = jax.nn.gelu(y.reshape(bsz, seq, SSM_WIDTH))
    z = z * jax.nn.sigmoid(z @ w_glu)
    return z @ w_out


def dsa_mixer(h, w_in, w_out):
    bsz, seq, _ = h.shape
    dt_ = h.dtype
    proj = h @ w_in
    offs = np.cumsum(ATT_SPLITS)[:-1].tolist()
    q, k, v, qi, ki, wi = jnp.split(proj, offs, axis=-1)
    cos, sin = rope_tables(seq, HEAD_DIM, dt_)
    q = apply_rope(q.reshape(bsz, seq, N_HEADS, HEAD_DIM), cos, sin)
    k = apply_rope(k.reshape(bsz, seq, N_KV_HEADS, HEAD_DIM), cos, sin)
    v = v.reshape(bsz, seq, N_KV_HEADS, HEAD_DIM)
    qi = apply_rope(qi.reshape(bsz, seq, IDX_HEADS, IDX_DIM), cos, sin)
    ki = apply_rope(ki[:, :, None, :], cos, sin)[:, :, 0, :]
    wi = wi * (IDX_HEADS ** -0.5 * IDX_DIM ** -0.5)

    k_sel = min(TOPK_MAX, seq // 4)
    n_blk = seq // Q_BLOCK
    to_blocks = lambda a: jnp.swapaxes(a.reshape((bsz, n_blk, Q_BLOCK) + a.shape[2:]), 0, 1)
    q_b, qi_b, wi_b = to_blocks(q), to_blocks(qi), to_blocks(wi)
    starts = jnp.arange(n_blk, dtype=jnp.int32) * Q_BLOCK
    key_pos = jnp.arange(seq, dtype=jnp.int32)
    b_idx = jnp.arange(bsz)[:, None, None]
    scale = HEAD_DIM ** -0.5

    def block(args):
        q_blk, qi_blk, wi_blk, t0 = args
        t = t0 + jnp.arange(Q_BLOCK, dtype=jnp.int32)
        rel = jax.nn.relu(jnp.einsum('bthd,bsd->bths', qi_blk, ki).astype(jnp.float32))
        score = jnp.einsum('bths,bth->bts', rel, wi_blk.astype(jnp.float32))
        causal = key_pos[None, :] <= t[:, None]
        score = jnp.where(causal[None], score, -jnp.inf)
        _, idx = lax.top_k(score, k_sel)
        kg = k[b_idx, idx]
        vg = v[b_idx, idx]
        qg = q_blk.reshape(bsz, Q_BLOCK, N_KV_HEADS, KV_GROUP, HEAD_DIM)
        att = jnp.einsum('btngd,btknd->btngk', qg, kg).astype(jnp.float32) * scale
        valid = idx <= t[None, :, None]
        att = jnp.where(valid[:, :, None, None, :], att, -jnp.inf)
        p = jax.nn.softmax(att, axis=-1).astype(dt_)
        o = jnp.einsum('btngk,btknd->btngd', p, vg)
        return o.reshape(bsz, Q_BLOCK, N_HEADS * HEAD_DIM)

    out = lax.map(block, (q_b, qi_b, wi_b, starts))
    out = jnp.swapaxes(out, 0, 1).reshape(bsz, seq, N_HEADS * HEAD_DIM)
    return out @ w_out


def sq_relu_mlp(h, w1, w2):
    a = jax.nn.relu(h @ w1)
    return (a * a) @ w2


def setup_inputs(seed: int = 0) -> dict:
    key = jax.random.key(seed)
    ks = jax.random.split(key, 20)
    nA, nB = N_SSM_LAYERS, N_ATT_LAYERS
    G, P, C, E = SSM_GROUPS, SSM_STATE, SSM_CH, SSM_WIDTH
    nrm = lambda k, shape, fan: jax.random.normal(k, shape, jnp.float32) * (fan ** -0.5)
    x = jax.random.normal(ks[0], (BATCH, SEQ, D_MODEL), jnp.float32)
    norm_g = 1.0 + 0.01 * jax.random.normal(ks[1], (DEPTH, 4, D_MODEL), jnp.float32)
    mlp_w1 = nrm(ks[2], (DEPTH, D_MODEL, D_FF), D_MODEL)
    mlp_w2 = nrm(ks[3], (DEPTH, D_FF, D_MODEL), D_FF)
    ssm_w_in = nrm(ks[4], (nA, D_MODEL, E), D_MODEL)
    n_idx = jnp.arange(P, dtype=jnp.float32)
    ssm_lam_re = -0.5 + 0.01 * jax.random.normal(ks[5], (nA, G, P), jnp.float32)
    ssm_lam_im = math.pi * n_idx[None, None, :] + 0.01 * jax.random.normal(ks[6], (nA, G, P), jnp.float32)
    ssm_log_dt = jax.random.uniform(ks[7], (nA, G), jnp.float32, math.log(DT_MIN), math.log(DT_MAX))
    ssm_b_re = nrm(ks[8], (nA, G, P, C), 2 * C)
    ssm_b_im = nrm(ks[9], (nA, G, P, C), 2 * C)
    ssm_c_re = nrm(ks[10], (nA, G, C, P), 2 * P)
    ssm_c_im = nrm(ks[11], (nA, G, C, P), 2 * P)
    ssm_d = jax.random.normal(ks[12], (nA, G, C), jnp.float32)
    ssm_w_glu = nrm(ks[13], (nA, E, E), E)
    ssm_w_out = nrm(ks[14], (nA, E, D_MODEL), E)
    att_w_in = nrm(ks[15], (nB, D_MODEL, ATT_IN_COLS), D_MODEL)
    att_w_out = nrm(ks[16], (nB, N_HEADS * HEAD_DIM, D_MODEL), N_HEADS * HEAD_DIM)
    return {"x": x, "norm_g": norm_g, "mlp_w1": mlp_w1, "mlp_w2": mlp_w2,
            "ssm_w_in": ssm_w_in, "ssm_lam_re": ssm_lam_re, "ssm_lam_im": ssm_lam_im,
            "ssm_log_dt": ssm_log_dt, "ssm_b_re": ssm_b_re, "ssm_b_im": ssm_b_im,
            "ssm_c_re": ssm_c_re, "ssm_c_im": ssm_c_im, "ssm_d": ssm_d,
            "ssm_w_glu": ssm_w_glu, "ssm_w_out": ssm_w_out,
            "att_w_in": att_w_in, "att_w_out": att_w_out}


def reference(x, norm_g, mlp_w1, mlp_w2, ssm_w_in, ssm_lam_re, ssm_lam_im, ssm_log_dt,
              ssm_b_re, ssm_b_im, ssm_c_re, ssm_c_im, ssm_d, ssm_w_glu, ssm_w_out,
              att_w_in, att_w_out):
    h = x
    for i in range(DEPTH):
        j = i // N_MIXERS
        pre = rms_norm(h, norm_g[i, 0])
        if i % N_MIXERS == 0:
            mix = s5_mixer(pre, ssm_w_in[j], ssm_lam_re[j], ssm_lam_im[j], ssm_log_dt[j],
                           ssm_b_re[j], ssm_b_im[j], ssm_c_re[j], ssm_c_im[j], ssm_d[j],
                           ssm_w_glu[j], ssm_w_out[j])
        else:
            mix = dsa_mixer(pre, att_w_in[j], att_w_out[j])
        h = h + rms_norm(mix, norm_g[i, 1])
        ff = sq_relu_mlp(rms_norm(h, norm_g[i, 2]), mlp_w1[i], mlp_w2[i])
        h = h + rms_norm(ff, norm_g[i, 3])
    return h
```

```python
import functools
import math

import jax
import jax.numpy as jnp
from jax import lax
from jax.experimental import pallas as pl
from jax.experimental.pallas import tpu as pltpu

NORM_EPS = 1e-6
SSM_CH = 16
SSM_STATE = 64
DT_MIN, DT_MAX = 1e-3, 1e-1
HEAD_DIM = 64
KV_GROUP = 4
IDX_DIM = 64
TOPK_MAX = 256
ROPE_THETA = 10000.0

LANES = 128
SUBLANES = 8
VMEM_LIMIT_BYTES = 56 * 1024 * 1024

TOKEN_TILE = 512
FF_TILE = 1024
SCAN_CHUNK = 64
SCAN_LANES = 512
Q_TILE = 256
K_TILE = 512
BISECT_ITERS = 20


def _params(*sem):
    return pltpu.CompilerParams(dimension_semantics=sem, vmem_limit_bytes=VMEM_LIMIT_BYTES)


def _rms(x, g):
    return x * lax.rsqrt(jnp.mean(x * x, axis=-1, keepdims=True) + NORM_EPS) * g


def _dot(a, b):
    return jnp.dot(a, b, preferred_element_type=jnp.float32)


def _dot_nt(a, b):
    return lax.dot_general(a, b, (((1,), (1,)), ((), ())), preferred_element_type=jnp.float32)


def _mlp_kernel(h_ref, g_in_ref, g_out_ref, w1_ref, w2_ref, o_ref, xn_ref, acc_ref):
    j = pl.program_id(1)

    @pl.when(j == 0)
    def _():
        xn_ref[...] = _rms(h_ref[...], g_in_ref[...]).astype(jnp.bfloat16)
        acc_ref[...] = jnp.zeros_like(acc_ref)

    a = jnp.maximum(_dot(xn_ref[...], w1_ref[...]), 0.0)
    acc_ref[...] += _dot((a * a).astype(jnp.bfloat16), w2_ref[...])

    @pl.when(j == pl.num_programs(1) - 1)
    def _():
        o_ref[...] = h_ref[...] + _rms(acc_ref[...], g_out_ref[...])


def _mlp(h, g_in, g_out, w1, w2):
    t, d = h.shape
    ff = w1.shape[1]
    tm = min(TOKEN_TILE, t)
    tf = min(FF_TILE, ff)
    return pl.pallas_call(
        _mlp_kernel,
        out_shape=jax.ShapeDtypeStruct((t, d), jnp.float32),
        grid=(t // tm, ff // tf),
        in_specs=[
            pl.BlockSpec((tm, d), lambda i, j: (i, 0)),
            pl.BlockSpec((1, d), lambda i, j: (0, 0)),
            pl.BlockSpec((1, d), lambda i, j: (0, 0)),
            pl.BlockSpec((d, tf), lambda i, j: (0, j)),
            pl.BlockSpec((tf, d), lambda i, j: (j, 0)),
        ],
        out_specs=pl.BlockSpec((tm, d), lambda i, j: (i, 0)),
        scratch_shapes=[pltpu.VMEM((tm, d), jnp.bfloat16), pltpu.VMEM((tm, d), jnp.float32)],
        compiler_params=_params("parallel", "arbitrary"),
        name="mlp",
    )(h, g_in, g_out, w1, w2)


def _ssm_in_kernel(h_ref, g_ref, w_ref, u_ref):
    xn = _rms(h_ref[...], g_ref[...]).astype(jnp.bfloat16)
    u_ref[...] = _dot(xn, w_ref[...])


def _ssm_in(h, g, w, bsz, seq):
    d = h.shape[1]
    e = w.shape[1]
    tm = min(TOKEN_TILE, seq)
    nt = seq // tm
    return pl.pallas_call(
        _ssm_in_kernel,
        out_shape=jax.ShapeDtypeStruct((seq, bsz * e), jnp.float32),
        grid=(bsz, nt),
        in_specs=[
            pl.BlockSpec((tm, d), lambda b, i: (b * nt + i, 0)),
            pl.BlockSpec((1, d), lambda b, i: (0, 0)),
            pl.BlockSpec((d, e), lambda b, i: (0, 0)),
        ],
        out_specs=pl.BlockSpec((tm, e), lambda b, i: (i, b)),
        compiler_params=_params("parallel", "parallel"),
        name="ssm_in",
    )(h, g, w)


def _ssm_scan_kernel(u_ref, ar_ref, ai_ref, bmat_ref, cmat_ref, d_ref, z_ref, x_ref, st_ref,
                     *, bsz, chunk, half, lanes):
    @pl.when(pl.program_id(0) == 0)
    def _():
        st_ref[...] = jnp.zeros_like(st_ref)

    u = u_ref[...]
    x_ref[...] = _dot(u.astype(jnp.bfloat16), bmat_ref[...])

    for c in range(half // lanes):
        re = slice(c * lanes, (c + 1) * lanes)
        im = slice(half + c * lanes, half + (c + 1) * lanes)
        ar = ar_ref[:, re]
        ai = ai_ref[:, re]

        def step(t, carry, re=re, im=im, ar=ar, ai=ai):
            xr, xi = carry
            rows = pl.ds(pl.multiple_of(t * bsz, bsz), bsz)
            nr = ar * xr - ai * xi + x_ref[rows, re]
            ni = ar * xi + ai * xr + x_ref[rows, im]
            x_ref[rows, re] = nr
            x_ref[rows, im] = ni
            return nr, ni

        xr, xi = lax.fori_loop(0, chunk, step, (st_ref[:, re], st_ref[:, im]))
        st_ref[:, re] = xr
        st_ref[:, im] = xi

    y = _dot(x_ref[...].astype(jnp.bfloat16), cmat_ref[...]) + d_ref[...] * u
    z_ref[...] = jax.nn.gelu(y)


def _ssm_scan(u2, ar, ai, bmat, cmat, dvec, bsz):
    rows, e = u2.shape
    seq = rows // bsz
    n2 = bmat.shape[1]
    half = n2 // 2
    chunk = min(SCAN_CHUNK, seq)
    lanes = min(SCAN_LANES, half)
    kern = functools.partial(_ssm_scan_kernel, bsz=bsz, chunk=chunk, half=half, lanes=lanes)
    return pl.pallas_call(
        kern,
        out_shape=jax.ShapeDtypeStruct((rows, e), jnp.float32),
        grid=(seq // chunk,),
        in_specs=[
            pl.BlockSpec((chunk * bsz, e), lambda i: (i, 0)),
            pl.BlockSpec((bsz, half), lambda i: (0, 0)),
            pl.BlockSpec((bsz, half), lambda i: (0, 0)),
            pl.BlockSpec((e, n2), lambda i: (0, 0)),
            pl.BlockSpec((n2, e), lambda i: (0, 0)),
            pl.BlockSpec((1, e), lambda i: (0, 0)),
        ],
        out_specs=pl.BlockSpec((chunk * bsz, e), lambda i: (i, 0)),
        scratch_shapes=[pltpu.VMEM((chunk * bsz, n2), jnp.float32),
                        pltpu.VMEM((bsz, n2), jnp.float32)],
        compiler_params=_params("arbitrary"),
        name="ssm_scan",
    )(u2, ar, ai, bmat, cmat, dvec)


def _ssm_out_kernel(z_ref, h_ref, g_ref, wg_ref, wo_ref, o_ref):
    z = z_ref[...]
    zz = z * jax.nn.sigmoid(_dot(z.astype(jnp.bfloat16), wg_ref[...]))
    mix = _dot(zz.astype(jnp.bfloat16), wo_ref[...])
    o_ref[...] = h_ref[...] + _rms(mix, g_ref[...])


def _ssm_out(z, h, g, wg, wo, bsz, seq):
    d = h.shape[1]
    e = wg.shape[0]
    tm = min(TOKEN_TILE, seq)
    nt = seq // tm
    return pl.pallas_call(
        _ssm_out_kernel,
        out_shape=jax.ShapeDtypeStruct(h.shape, jnp.float32),
        grid=(bsz, nt),
        in_specs=[
            pl.BlockSpec((tm, e), lambda b, i: (i, b)),
            pl.BlockSpec((tm, d), lambda b, i: (b * nt + i, 0)),
            pl.BlockSpec((1, d), lambda b, i: (0, 0)),
            pl.BlockSpec((e, e), lambda b, i: (0, 0)),
            pl.BlockSpec((e, d), lambda b, i: (0, 0)),
        ],
        out_specs=pl.BlockSpec((tm, d), lambda b, i: (b * nt + i, 0)),
        compiler_params=_params("parallel", "parallel"),
        name="ssm_out",
    )(z, h, g, wg, wo)


def _ssm_discretise(lam_re, lam_im, log_dt, b_re, b_im, c_re, c_im, d_skip, bsz):
    g, p = lam_re.shape
    c = b_re.shape[2]
    dt = jnp.exp(log_dt)[:, None]
    mag = jnp.exp(lam_re * dt)
    abar_re = mag * jnp.cos(lam_im * dt)
    abar_im = mag * jnp.sin(lam_im * dt)
    den = lam_re * lam_re + lam_im * lam_im
    nr = abar_re - 1.0
    ni = abar_im
    fr = (nr * lam_re + ni * lam_im) / den
    fi = (ni * lam_re - nr * lam_im) / den
    bbar_re = fr[..., None] * b_re - fi[..., None] * b_im
    bbar_im = fr[..., None] * b_im + fi[..., None] * b_re
    eye = jnp.eye(g, dtype=jnp.float32)
    bd = lambda m: jnp.einsum('gpc,gh->gchp', m, eye).reshape(g * c, g * p)
    bmat = jnp.concatenate([bd(bbar_re), bd(bbar_im)], axis=1).astype(jnp.bfloat16)
    cd = lambda m: jnp.einsum('gcp,gh->gphc', m, eye).reshape(g * p, g * c)
    cmat = jnp.concatenate([cd(c_re), cd(-c_im)], axis=0).astype(jnp.bfloat16)
    ar = jnp.broadcast_to(abar_re.reshape(1, g * p), (bsz, g * p))
    ai = jnp.broadcast_to(abar_im.reshape(1, g * p), (bsz, g * p))
    return ar, ai, bmat, cmat, d_skip.reshape(1, g * c)


def _rope128(x, cos, sin_signed, lane):
    swapped = jnp.where((lane % HEAD_DIM) < HEAD_DIM // 2,
                        pltpu.roll(x, LANES - HEAD_DIM // 2, axis=1),
                        pltpu.roll(x, HEAD_DIM // 2, axis=1))
    return x * cos + swapped * sin_signed


def _att_in_kernel(h_ref, g_ref, w_ref, cos_ref, sin_ref, q_ref, k_ref, v_ref, qi_ref, ki_ref, wi_ref,
                   *, n_q, n_k, n_v, n_qi, n_ki, q_scale, wi_scale):
    xn = _rms(h_ref[...], g_ref[...]).astype(jnp.bfloat16)
    proj = _dot(xn, w_ref[...])
    cos = cos_ref[...]
    sin = sin_ref[...]
    lane = lax.broadcasted_iota(jnp.int32, cos.shape, 1)
    off = 0

    def roped(ref, width, off, scale=None):
        for c in range(width // LANES):
            x = _rope128(proj[:, off + c * LANES: off + (c + 1) * LANES], cos, sin, lane)
            if scale is not None:
                x = x * scale
            ref[:, c * LANES:(c + 1) * LANES] = x.astype(ref.dtype)
        return off + width

    off = roped(q_ref, n_q, off, q_scale)
    off = roped(k_ref, n_k, off)
    v_ref[...] = proj[:, off:off + n_v].astype(v_ref.dtype)
    off += n_v
    off = roped(qi_ref, n_qi, off)
    off = roped(ki_ref, n_ki, off)
    wi_ref[...] = proj[:, off:off + LANES] * wi_scale


def _att_in(h, g, w, cos, sin, bsz, seq, widths, q_scale, wi_scale):
    d = h.shape[1]
    n_q, n_k, n_v, n_qi, n_ki = widths
    ncols = w.shape[1]
    tm = min(TOKEN_TILE, seq)
    nt = seq // tm
    t = bsz * seq
    row = lambda b, i: (b * nt + i, 0)
    kern = functools.partial(_att_in_kernel, n_q=n_q, n_k=n_k, n_v=n_v, n_qi=n_qi, n_ki=n_ki,
                             q_scale=q_scale, wi_scale=wi_scale)
    bf = jnp.bfloat16
    return pl.pallas_call(
        kern,
        out_shape=[jax.ShapeDtypeStruct((t, n_q), bf), jax.ShapeDtypeStruct((t, n_k), bf),
                   jax.ShapeDtypeStruct((t, n_v), bf), jax.ShapeDtypeStruct((t, n_qi), bf),
                   jax.ShapeDtypeStruct((t, n_ki), bf), jax.ShapeDtypeStruct((t, LANES), jnp.float32)],
        grid=(bsz, nt),
        in_specs=[
            pl.BlockSpec((tm, d), row),
            pl.BlockSpec((1, d), lambda b, i: (0, 0)),
            pl.BlockSpec((d, ncols), lambda b, i: (0, 0)),
            pl.BlockSpec((tm, LANES), lambda b, i: (i, 0)),
            pl.BlockSpec((tm, LANES), lambda b, i: (i, 0)),
        ],
        out_specs=[pl.BlockSpec((tm, n_q), row), pl.BlockSpec((tm, n_k), row), pl.BlockSpec((tm, n_v), row),
                   pl.BlockSpec((tm, n_qi), row), pl.BlockSpec((tm, n_ki), row), pl.BlockSpec((tm, LANES), row)],
        compiler_params=_params("parallel", "parallel"),
        name="att_in",
    )(h, g, w, cos, sin)


def _dsa_kernel(q_ref, qi_ref, wi_ref, k_ref, v_ref, ki_ref, o_ref,
                sc_ref, m_ref, l_ref, acc_ref, *, tq, tk, k_sel, n_heads, n_idx):
    i = pl.program_id(1)
    n_kt = (i * tq + tq + tk - 1) // tk
    neg_inf = jnp.float32(-jnp.inf)
    row = i * tq + lax.broadcasted_iota(jnp.int32, (tq, tk), 0)
    col0 = lax.broadcasted_iota(jnp.int32, (tq, tk), 1)
    wi = wi_ref[...]

    def score_tile(kt, carry):
        rmax, rmin = carry
        ks = pl.ds(pl.multiple_of(kt * tk, tk), tk)
        acc = jnp.zeros((tq, tk), jnp.float32)
        for hh in range(n_idx):
            qc = qi_ref[:, (hh // 2) * LANES:(hh // 2 + 1) * LANES]
            kc = ki_ref[ks, (hh % 2) * LANES:(hh % 2 + 1) * LANES]
            acc = acc + jnp.maximum(_dot_nt(qc, kc), 0.0) * wi[:, hh:hh + 1]
        causal = (col0 + kt * tk) <= row
        sc_ref[kt] = jnp.where(causal, acc, neg_inf)
        rmax = jnp.maximum(rmax, jnp.max(jnp.where(causal, acc, neg_inf), axis=1, keepdims=True))
        rmin = jnp.minimum(rmin, jnp.min(jnp.where(causal, acc, -neg_inf), axis=1, keepdims=True))
        return rmax, rmin

    rmax, rmin = lax.fori_loop(0, n_kt, score_tile,
                               (jnp.full((tq, 1), neg_inf), jnp.full((tq, 1), -neg_inf)))

    kf = jnp.float32(k_sel)

    def count_ge(x):
        def body(kt, c):
            return c + jnp.sum(jnp.where(sc_ref[kt] >= x, 1.0, 0.0), axis=1, keepdims=True)
        return lax.fori_loop(0, n_kt, body, jnp.zeros((tq, 1), jnp.float32))

    def bisect(_, carry):
        lo, hi = carry
        mid = 0.5 * lo + 0.5 * hi
        ge = count_ge(mid) >= kf
        return jnp.where(ge, mid, lo), jnp.where(ge, hi, mid)

    lo, hi = lax.fori_loop(0, BISECT_ITERS, bisect, (rmin, rmax))

    few = (i * tq + lax.broadcasted_iota(jnp.int32, (tq, 1), 0)) < k_sel

    def max_le(x):
        def body(kt, c):
            s = sc_ref[kt]
            return jnp.maximum(c, jnp.max(jnp.where(s <= x, s, neg_inf), axis=1, keepdims=True))
        return lax.fori_loop(0, n_kt, body, jnp.full((tq, 1), neg_inf))

    def probe(x):
        def body(kt, c):
            cnt, nxt = c
            s = sc_ref[kt]
            cnt = cnt + jnp.sum(jnp.where(s >= x, 1.0, 0.0), axis=1, keepdims=True)
            nxt = jnp.maximum(nxt, jnp.max(jnp.where(s < x, s, neg_inf), axis=1, keepdims=True))
            return cnt, nxt
        return lax.fori_loop(0, n_kt, body, (jnp.zeros((tq, 1), jnp.float32), jnp.full((tq, 1), neg_inf)))

    cand0 = jnp.where(few, rmin, max_le(hi))
    cnt0, nxt0 = probe(cand0)

    def unresolved(cnt):
        return jnp.logical_and(jnp.logical_not(few), cnt < kf)

    def finish_cond(c):
        _, cnt, _ = c
        return jnp.max(jnp.where(unresolved(cnt), 1.0, 0.0)) > 0.0

    def finish_body(c):
        cand, cnt, nxt = c
        cand = jnp.where(unresolved(cnt), nxt, cand)
        cnt, nxt = probe(cand)
        return cand, cnt, nxt

    thr, cnt_ge, _ = lax.while_loop(finish_cond, finish_body, (cand0, cnt0, nxt0))

    tied = jnp.logical_and(jnp.logical_not(few), cnt_ge > kf)

    @pl.when(jnp.max(jnp.where(tied, 1.0, 0.0)) > 0.0)
    def _():
        def gt_body(kt, c):
            return c + jnp.sum(jnp.where(sc_ref[kt] > thr, 1.0, 0.0), axis=1, keepdims=True)
        cnt_gt = lax.fori_loop(0, n_kt, gt_body, jnp.zeros((tq, 1), jnp.float32))
        need = kf - cnt_gt
        tri = (lax.broadcasted_iota(jnp.int32, (tk, tk), 0)
               <= lax.broadcasted_iota(jnp.int32, (tk, tk), 1)).astype(jnp.bfloat16)

        def drop_body(kt, run):
            s = sc_ref[kt]
            eq = jnp.logical_and(s == thr, tied)
            eqf = jnp.where(eq, 1.0, 0.0)
            rank = run + _dot(eqf.astype(jnp.bfloat16), tri)
            sc_ref[kt] = jnp.where(jnp.logical_and(eq, rank > need), neg_inf, s)
            return run + jnp.sum(eqf, axis=1, keepdims=True)

        lax.fori_loop(0, n_kt, drop_body, jnp.zeros((tq, 1), jnp.float32))

    m_ref[...] = jnp.full_like(m_ref, neg_inf)
    l_ref[...] = jnp.zeros_like(l_ref)
    acc_ref[...] = jnp.zeros_like(acc_ref)
    lane = lax.broadcasted_iota(jnp.int32, (tq, LANES), 1)
    low_half = lane < HEAD_DIM

    def att_tile(kt, carry):
        ks = pl.ds(pl.multiple_of(kt * tk, tk), tk)
        keep = sc_ref[kt] >= thr
        for c in range(n_heads // 2):
            qc = q_ref[:, c * LANES:(c + 1) * LANES]
            alphas = []
            pv = None
            for half in range(2):
                h = 2 * c + half
                kvc = (2 * (h // KV_GROUP) + half) * LANES
                s = jnp.where(keep, _dot_nt(qc, k_ref[ks, kvc:kvc + LANES]), neg_inf)
                m_old = m_ref[h]
                m_new = jnp.maximum(m_old, jnp.max(s, axis=1, keepdims=True))
                m_safe = jnp.where(m_new == neg_inf, 0.0, m_new)
                alpha = jnp.exp(m_old - m_safe)
                p = jnp.exp(s - m_safe)
                l_ref[h] = alpha * l_ref[h] + jnp.sum(p, axis=1, keepdims=True)
                m_ref[h] = m_new
                alphas.append(alpha)
                part = _dot(p.astype(jnp.bfloat16), v_ref[ks, kvc:kvc + LANES])
                pv = part if pv is None else pv + part
            acc_ref[c] = jnp.where(low_half, alphas[0], alphas[1]) * acc_ref[c] + pv
        return carry

    lax.fori_loop(0, n_kt, att_tile, 0)

    for c in range(n_heads // 2):
        inv = jnp.where(low_half, 1.0 / l_ref[2 * c], 1.0 / l_ref[2 * c + 1])
        o_ref[:, c * LANES:(c + 1) * LANES] = (acc_ref[c] * inv).astype(o_ref.dtype)


def _dsa(q, qi, wi, kx, vx, kix, bsz, seq, k_sel):
    t, n_q = q.shape
    n_heads = n_q // HEAD_DIM
    n_idx = qi.shape[1] // IDX_DIM
    tq = min(Q_TILE, seq)
    tk = min(K_TILE, seq)
    nq = seq // tq
    row = lambda b, i: (b * nq + i, 0)
    per_batch = lambda b, i: (b, 0)
    kern = functools.partial(_dsa_kernel, tq=tq, tk=tk, k_sel=k_sel, n_heads=n_heads, n_idx=n_idx)
    return pl.pallas_call(
        kern,
        out_shape=jax.ShapeDtypeStruct((t, n_q), jnp.bfloat16),
        grid=(bsz, nq),
        in_specs=[
            pl.BlockSpec((tq, n_q), row),
            pl.BlockSpec((tq, qi.shape[1]), row),
            pl.BlockSpec((tq, LANES), row),
            pl.BlockSpec((seq, kx.shape[1]), per_batch),
            pl.BlockSpec((seq, vx.shape[1]), per_batch),
            pl.BlockSpec((seq, kix.shape[1]), per_batch),
        ],
        out_specs=pl.BlockSpec((tq, n_q), row),
        scratch_shapes=[
            pltpu.VMEM((seq // tk, tq, tk), jnp.float32),
            pltpu.VMEM((n_heads, tq, 1), jnp.float32),
            pltpu.VMEM((n_heads, tq, 1), jnp.float32),
            pltpu.VMEM((n_heads // 2, tq, LANES), jnp.float32),
        ],
        compiler_params=_params("parallel", "arbitrary"),
        name="dsa",
    )(q, qi, wi, kx, vx, kix)


def _att_out_kernel(a_ref, h_ref, g_ref, w_ref, o_ref):
    o_ref[...] = h_ref[...] + _rms(_dot(a_ref[...], w_ref[...]), g_ref[...])


def _att_out(a, h, g, w):
    t, d = h.shape
    n = a.shape[1]
    tm = min(TOKEN_TILE, t)
    return pl.pallas_call(
        _att_out_kernel,
        out_shape=jax.ShapeDtypeStruct(h.shape, jnp.float32),
        grid=(t // tm,),
        in_specs=[
            pl.BlockSpec((tm, n), lambda i: (i, 0)),
            pl.BlockSpec((tm, d), lambda i: (i, 0)),
            pl.BlockSpec((1, d), lambda i: (0, 0)),
            pl.BlockSpec((n, d), lambda i: (0, 0)),
        ],
        out_specs=pl.BlockSpec((tm, d), lambda i: (i, 0)),
        compiler_params=_params("parallel"),
        name="att_out",
    )(a, h, g, w)


def _rope_tables(seq):
    half = HEAD_DIM // 2
    inv_freq = ROPE_THETA ** (-jnp.arange(half, dtype=jnp.float32) * 2.0 / HEAD_DIM)
    ang = jnp.arange(seq, dtype=jnp.float32)[:, None] * inv_freq[None, :]
    cos = jnp.concatenate([jnp.cos(ang)] * 4, axis=-1)
    sin = jnp.concatenate([-jnp.sin(ang), jnp.sin(ang)] * 2, axis=-1)
    return cos, sin


def _half_chunks(w, low):
    d, n = w.shape
    w3 = w.reshape(d, n // HEAD_DIM, HEAD_DIM)
    z = jnp.zeros_like(w3)
    return jnp.concatenate([w3, z] if low else [z, w3], axis=-1).reshape(d, 2 * n)


def _att_weights(w_in, d_model):
    n_heads = d_model // HEAD_DIM
    n_kv = n_heads // KV_GROUP
    n_idx = max(4, d_model // 128)
    splits = (n_heads * HEAD_DIM, n_kv * HEAD_DIM, n_kv * HEAD_DIM, n_idx * IDX_DIM, IDX_DIM, n_idx)
    offs = [0]
    for s in splits:
        offs.append(offs[-1] + s)
    wq, wk, wv, wqi, wki, wwi = (w_in[:, offs[j]:offs[j + 1]] for j in range(6))
    d = w_in.shape[0]

    def both_halves(w):
        lo = _half_chunks(w, True).reshape(d, -1, LANES)
        hi = _half_chunks(w, False).reshape(d, -1, LANES)
        return jnp.stack([lo, hi], axis=2).reshape(d, -1)

    wkx, wvx, wkix = both_halves(wk), both_halves(wv), both_halves(wki)
    wwi_p = jnp.pad(wwi, ((0, 0), (0, LANES - n_idx)))
    w_cat = jnp.concatenate([wq, wkx, wvx, wqi, wkix, wwi_p], axis=1).astype(jnp.bfloat16)
    widths = (wq.shape[1], wkx.shape[1], wvx.shape[1], wqi.shape[1], wkix.shape[1])
    return w_cat, widths, n_idx


def kernel(x, norm_g, mlp_w1, mlp_w2, ssm_w_in, ssm_lam_re, ssm_lam_im, ssm_log_dt, ssm_b_re, ssm_b_im,
           ssm_c_re, ssm_c_im, ssm_d, ssm_w_glu, ssm_w_out, att_w_in, att_w_out):
    bsz, seq, d_model = x.shape
    depth = norm_g.shape[0]
    bf = jnp.bfloat16
    h = x.reshape(bsz * seq, d_model)
    cos, sin = _rope_tables(seq)
    k_sel = min(TOPK_MAX, seq // 4)
    for i in range(depth):
        j = i // 2
        g = norm_g[i][:, None, :]
        if i % 2 == 0:
            ar, ai, bmat, cmat, dvec = _ssm_discretise(
                ssm_lam_re[j], ssm_lam_im[j], ssm_log_dt[j], ssm_b_re[j], ssm_b_im[j],
                ssm_c_re[j], ssm_c_im[j], ssm_d[j], bsz)
            e = ssm_w_in.shape[2]
            u = _ssm_in(h, g[0], ssm_w_in[j].astype(bf), bsz, seq)
            z = _ssm_scan(u.reshape(seq * bsz, e), ar, ai, bmat, cmat, dvec, bsz)
            h = _ssm_out(z.reshape(seq, bsz * e), h, g[1], ssm_w_glu[j].astype(bf),
                         ssm_w_out[j].astype(bf), bsz, seq)
        else:
            w_cat, widths, n_idx = _att_weights(att_w_in[j], d_model)
            q, kx, vx, qi, kix, wi = _att_in(h, g[0], w_cat, cos, sin, bsz, seq, widths,
                                             HEAD_DIM ** -0.5, n_idx ** -0.5 * IDX_DIM ** -0.5)
            a = _dsa(q, qi, wi, kx, vx, kix, bsz, seq, k_sel)
            h = _att_out(a, h, g[1], att_w_out[j].astype(bf))
        h = _mlp(h, g[2], g[3], mlp_w1[i].astype(bf), mlp_w2[i].astype(bf))
    return h.reshape(bsz, seq, d_model)
```

```python
import functools
import math

import jax
import jax.numpy as jnp
from jax import lax
from jax.experimental import pallas as pl
from jax.experimental.pallas import tpu as pltpu

NORM_EPS = 1e-6
SSM_CH = 16
SSM_STATE = 64
DT_MIN, DT_MAX = 1e-3, 1e-1
HEAD_DIM = 64
KV_GROUP = 4
IDX_DIM = 64
TOPK_MAX = 256
ROPE_THETA = 10000.0

LANES = 128
SUBLANES = 8
VMEM_LIMIT_BYTES = 56 * 1024 * 1024

TOKEN_TILE = 512
FF_TILE = 1024
SCAN_CHUNK = 64
SCAN_LANES = 512
Q_TILE = 256
K_TILE = 512
BISECT_ITERS = 20
LOG2_E = math.log2(math.e)
BOUND_SLACK = 1.01
MIN_DENOMINATOR = 2.0 ** -60


def _params(*sem):
    return pltpu.CompilerParams(dimension_semantics=sem, vmem_limit_bytes=VMEM_LIMIT_BYTES)


def _rms(x, g):
    return x * lax.rsqrt(jnp.mean(x * x, axis=-1, keepdims=True) + NORM_EPS) * g


def _dot(a, b):
    return jnp.dot(a, b, preferred_element_type=jnp.float32)


def _dot_nt(a, b):
    return lax.dot_general(a, b, (((1,), (1,)), ((), ())), preferred_element_type=jnp.float32)


def _mlp_kernel(h_ref, g_in_ref, g_out_ref, w1_ref, w2_ref, o_ref, xn_ref, acc_ref):
    j = pl.program_id(1)

    @pl.when(j == 0)
    def _():
        xn_ref[...] = _rms(h_ref[...], g_in_ref[...]).astype(jnp.bfloat16)
        acc_ref[...] = jnp.zeros_like(acc_ref)

    a = jnp.maximum(_dot(xn_ref[...], w1_ref[...]), 0.0)
    acc_ref[...] += _dot((a * a).astype(jnp.bfloat16), w2_ref[...])

    @pl.when(j == pl.num_programs(1) - 1)
    def _():
        o_ref[...] = h_ref[...] + _rms(acc_ref[...], g_out_ref[...])


def _mlp(h, g_in, g_out, w1, w2):
    t, d = h.shape
    ff = w1.shape[1]
    tm = min(TOKEN_TILE, t)
    tf = min(FF_TILE, ff)
    return pl.pallas_call(
        _mlp_kernel,
        out_shape=jax.ShapeDtypeStruct((t, d), jnp.float32),
        grid=(t // tm, ff // tf),
        in_specs=[
            pl.BlockSpec((tm, d), lambda i, j: (i, 0)),
            pl.BlockSpec((1, d), lambda i, j: (0, 0)),
            pl.BlockSpec((1, d), lambda i, j: (0, 0)),
            pl.BlockSpec((d, tf), lambda i, j: (0, j)),
            pl.BlockSpec((tf, d), lambda i, j: (j, 0)),
        ],
        out_specs=pl.BlockSpec((tm, d), lambda i, j: (i, 0)),
        scratch_shapes=[pltpu.VMEM((tm, d), jnp.bfloat16), pltpu.VMEM((tm, d), jnp.float32)],
        compiler_params=_params("parallel", "arbitrary"),
        name="mlp",
    )(h, g_in, g_out, w1, w2)


def _ssm_in_kernel(h_ref, g_ref, w_ref, u_ref):
    xn = _rms(h_ref[...], g_ref[...]).astype(jnp.bfloat16)
    u_ref[...] = _dot(xn, w_ref[...])


def _ssm_in(h, g, w, bsz, seq):
    d = h.shape[1]
    e = w.shape[1]
    tm = min(TOKEN_TILE, seq)
    nt = seq // tm
    return pl.pallas_call(
        _ssm_in_kernel,
        out_shape=jax.ShapeDtypeStruct((seq, bsz * e), jnp.float32),
        grid=(bsz, nt),
        in_specs=[
            pl.BlockSpec((tm, d), lambda b, i: (b * nt + i, 0)),
            pl.BlockSpec((1, d), lambda b, i: (0, 0)),
            pl.BlockSpec((d, e), lambda b, i: (0, 0)),
        ],
        out_specs=pl.BlockSpec((tm, e), lambda b, i: (i, b)),
        compiler_params=_params("parallel", "parallel"),
        name="ssm_in",
    )(h, g, w)


def _ssm_scan_kernel(u_ref, ar_ref, ai_ref, bmat_ref, cmat_ref, d_ref, z_ref, x_ref, st_ref,
                     *, bsz, chunk, half, lanes):
    @pl.when(pl.program_id(0) == 0)
    def _():
        st_ref[...] = jnp.zeros_like(st_ref)

    u = u_ref[...]
    x_ref[...] = _dot(u.astype(jnp.bfloat16), bmat_ref[...])

    for c in range(half // lanes):
        re = slice(c * lanes, (c + 1) * lanes)
        im = slice(half + c * lanes, half + (c + 1) * lanes)
        ar = ar_ref[:, re]
        ai = ai_ref[:, re]

        def step(t, carry, re=re, im=im, ar=ar, ai=ai):
            xr, xi = carry
            rows = pl.ds(pl.multiple_of(t * bsz, bsz), bsz)
            nr = ar * xr - ai * xi + x_ref[rows, re]
            ni = ar * xi + ai * xr + x_ref[rows, im]
            x_ref[rows, re] = nr
            x_ref[rows, im] = ni
            return nr, ni

        xr, xi = lax.fori_loop(0, chunk, step, (st_ref[:, re], st_ref[:, im]))
        st_ref[:, re] = xr
        st_ref[:, im] = xi

    y = _dot(x_ref[...].astype(jnp.bfloat16), cmat_ref[...]) + d_ref[...] * u
    z_ref[...] = jax.nn.gelu(y)


def _ssm_scan(u2, ar, ai, bmat, cmat, dvec, bsz):
    rows, e = u2.shape
    seq = rows // bsz
    n2 = bmat.shape[1]
    half = n2 // 2
    chunk = min(SCAN_CHUNK, seq)
    lanes = min(SCAN_LANES, half)
    kern = functools.partial(_ssm_scan_kernel, bsz=bsz, chunk=chunk, half=half, lanes=lanes)
    return pl.pallas_call(
        kern,
        out_shape=jax.ShapeDtypeStruct((rows, e), jnp.float32),
        grid=(seq // chunk,),
        in_specs=[
            pl.BlockSpec((chunk * bsz, e), lambda i: (i, 0)),
            pl.BlockSpec((bsz, half), lambda i: (0, 0)),
            pl.BlockSpec((bsz, half), lambda i: (0, 0)),
            pl.BlockSpec((e, n2), lambda i: (0, 0)),
            pl.BlockSpec((n2, e), lambda i: (0, 0)),
            pl.BlockSpec((1, e), lambda i: (0, 0)),
        ],
        out_specs=pl.BlockSpec((chunk * bsz, e), lambda i: (i, 0)),
        scratch_shapes=[pltpu.VMEM((chunk * bsz, n2), jnp.float32),
                        pltpu.VMEM((bsz, n2), jnp.float32)],
        compiler_params=_params("arbitrary"),
        name="ssm_scan",
    )(u2, ar, ai, bmat, cmat, dvec)


def _ssm_out_kernel(z_ref, h_ref, g_ref, wg_ref, wo_ref, o_ref):
    z = z_ref[...]
    zz = z * jax.nn.sigmoid(_dot(z.astype(jnp.bfloat16), wg_ref[...]))
    mix = _dot(zz.astype(jnp.bfloat16), wo_ref[...])
    o_ref[...] = h_ref[...] + _rms(mix, g_ref[...])


def _ssm_out(z, h, g, wg, wo, bsz, seq):
    d = h.shape[1]
    e = wg.shape[0]
    tm = min(TOKEN_TILE, seq)
    nt = seq // tm
    return pl.pallas_call(
        _ssm_out_kernel,
        out_shape=jax.ShapeDtypeStruct(h.shape, jnp.float32),
        grid=(bsz, nt),
        in_specs=[
            pl.BlockSpec((tm, e), lambda b, i: (i, b)),
            pl.BlockSpec((tm, d), lambda b, i: (b * nt + i, 0)),
            pl.BlockSpec((1, d), lambda b, i: (0, 0)),
            pl.BlockSpec((e, e), lambda b, i: (0, 0)),
            pl.BlockSpec((e, d), lambda b, i: (0, 0)),
        ],
        out_specs=pl.BlockSpec((tm, d), lambda b, i: (b * nt + i, 0)),
        compiler_params=_params("parallel", "parallel"),
        name="ssm_out",
    )(z, h, g, wg, wo)


def _ssm_discretise(lam_re, lam_im, log_dt, b_re, b_im, c_re, c_im, d_skip, bsz):
    g, p = lam_re.shape
    c = b_re.shape[2]
    dt = jnp.exp(log_dt)[:, None]
    mag = jnp.exp(lam_re * dt)
    abar_re = mag * jnp.cos(lam_im * dt)
    abar_im = mag * jnp.sin(lam_im * dt)
    den = lam_re * lam_re + lam_im * lam_im
    nr = abar_re - 1.0
    ni = abar_im
    fr = (nr * lam_re + ni * lam_im) / den
    fi = (ni * lam_re - nr * lam_im) / den
    bbar_re = fr[..., None] * b_re - fi[..., None] * b_im
    bbar_im = fr[..., None] * b_im + fi[..., None] * b_re
    eye = jnp.eye(g, dtype=jnp.float32)
    bd = lambda m: jnp.einsum('gpc,gh->gchp', m, eye).reshape(g * c, g * p)
    bmat = jnp.concatenate([bd(bbar_re), bd(bbar_im)], axis=1).astype(jnp.bfloat16)
    cd = lambda m: jnp.einsum('gcp,gh->gphc', m, eye).reshape(g * p, g * c)
    cmat = jnp.concatenate([cd(c_re), cd(-c_im)], axis=0).astype(jnp.bfloat16)
    ar = jnp.broadcast_to(abar_re.reshape(1, g * p), (bsz, g * p))
    ai = jnp.broadcast_to(abar_im.reshape(1, g * p), (bsz, g * p))
    return ar, ai, bmat, cmat, d_skip.reshape(1, g * c)


def _rope128(x, cos, sin_signed, lane):
    swapped = jnp.where((lane % HEAD_DIM) < HEAD_DIM // 2,
                        pltpu.roll(x, LANES - HEAD_DIM // 2, axis=1),
                        pltpu.roll(x, HEAD_DIM // 2, axis=1))
    return x * cos + swapped * sin_signed


def _att_in_kernel(h_ref, g_ref, w_ref, cos_ref, sin_ref, q_ref, k_ref, v_ref, qi_ref, ki_ref, wi_ref,
                   *, n_q, n_k, n_v, n_qi, n_ki, q_scale, wi_scale):
    xn = _rms(h_ref[...], g_ref[...]).astype(jnp.bfloat16)
    proj = _dot(xn, w_ref[...])
    cos = cos_ref[...]
    sin = sin_ref[...]
    lane = lax.broadcasted_iota(jnp.int32, cos.shape, 1)
    off = 0

    def roped(ref, width, off, scale=None):
        for c in range(width // LANES):
            x = _rope128(proj[:, off + c * LANES: off + (c + 1) * LANES], cos, sin, lane)
            if scale is not None:
                x = x * scale
            ref[:, c * LANES:(c + 1) * LANES] = x.astype(ref.dtype)
        return off + width

    off = roped(q_ref, n_q, off, q_scale)
    for c in range(n_k // LANES):
        x = _rope128(proj[:, off + c * LANES: off + (c + 1) * LANES], cos, sin, lane)
        k_ref[:, c * LANES:(c + 1) * LANES] = jnp.where(lane == HEAD_DIM, 1.0, x).astype(k_ref.dtype)
    off += n_k
    for c in range(n_v // LANES):
        x = proj[:, off + c * LANES: off + (c + 1) * LANES]
        v_ref[:, c * LANES:(c + 1) * LANES] = jnp.where(lane >= HEAD_DIM, 1.0, x).astype(v_ref.dtype)
    off += n_v
    off = roped(qi_ref, n_qi, off)
    off = roped(ki_ref, n_ki, off)
    wi_ref[...] = proj[:, off:off + LANES] * wi_scale


def _att_in(h, g, w, cos, sin, bsz, seq, widths, q_scale, wi_scale):
    d = h.shape[1]
    n_q, n_k, n_v, n_qi, n_ki = widths
    ncols = w.shape[1]
    tm = min(TOKEN_TILE, seq)
    nt = seq // tm
    t = bsz * seq
    row = lambda b, i: (b * nt + i, 0)
    kern = functools.partial(_att_in_kernel, n_q=n_q, n_k=n_k, n_v=n_v, n_qi=n_qi, n_ki=n_ki,
                             q_scale=q_scale, wi_scale=wi_scale)
    bf = jnp.bfloat16
    return pl.pallas_call(
        kern,
        out_shape=[jax.ShapeDtypeStruct((t, n_q), bf), jax.ShapeDtypeStruct((t, n_k), bf),
                   jax.ShapeDtypeStruct((t, n_v), bf), jax.ShapeDtypeStruct((t, n_qi), bf),
                   jax.ShapeDtypeStruct((t, n_ki), bf), jax.ShapeDtypeStruct((t, LANES), jnp.float32)],
        grid=(bsz, nt),
        in_specs=[
            pl.BlockSpec((tm, d), row),
            pl.BlockSpec((1, d), lambda b, i: (0, 0)),
            pl.BlockSpec((d, ncols), lambda b, i: (0, 0)),
            pl.BlockSpec((tm, LANES), lambda b, i: (i, 0)),
            pl.BlockSpec((tm, LANES), lambda b, i: (i, 0)),
        ],
        out_specs=[pl.BlockSpec((tm, n_q), row), pl.BlockSpec((tm, n_k), row), pl.BlockSpec((tm, n_v), row),
                   pl.BlockSpec((tm, n_qi), row), pl.BlockSpec((tm, n_ki), row), pl.BlockSpec((tm, LANES), row)],
        compiler_params=_params("parallel", "parallel"),
        name="att_in",
    )(h, g, w, cos, sin)


def _lane_fold(x, op):
    out = x[:, :LANES]
    for j in range(1, x.shape[1] // LANES):
        out = op(out, x[:, j * LANES:(j + 1) * LANES])
    return out


def _dsa_kernel(q_ref, qi_ref, wi_ref, k_ref, v_ref, ki_ref, o_ref,
                sc_ref, qx_ref, acc_ref, m_ref, kn_ref, *, tq, tk, k_sel, n_heads, n_idx):
    i = pl.program_id(1)
    n_kv = n_heads // KV_GROUP
    lane = lax.broadcasted_iota(jnp.int32, (tq, LANES), 1)
    is_head = lane < HEAD_DIM

    @pl.when(i == 0)
    def _():
        lane_k = lax.broadcasted_iota(jnp.int32, (tk, LANES), 1)
        for n in range(n_kv):
            def body(r, c, n=n):
                x = k_ref[pl.ds(pl.multiple_of(r * tk, tk), tk), n * LANES:(n + 1) * LANES].astype(jnp.float32)
                x = jnp.where(lane_k < HEAD_DIM, x, 0.0)
                ss = jnp.sum(x * x, axis=1, keepdims=True)
                return jnp.maximum(c, jnp.max(ss, axis=0, keepdims=True))
            kmax = lax.fori_loop(0, k_ref.shape[0] // tk, body, jnp.zeros((1, 1), jnp.float32))
            kn_ref[n] = jnp.broadcast_to(kmax, kn_ref.shape[1:])

    n_kt = (i * tq + tq + tk - 1) // tk
    neg_inf = jnp.float32(-jnp.inf)
    row = i * tq + lax.broadcasted_iota(jnp.int32, (tq, tk), 0)
    col0 = lax.broadcasted_iota(jnp.int32, (tq, tk), 1)
    wi = wi_ref[...]

    def score_tile(kt, carry):
        rmax, rmin = carry
        ks = pl.ds(pl.multiple_of(kt * tk, tk), tk)
        acc = jnp.zeros((tq, tk), jnp.float32)
        for hh in range(n_idx):
            qc = qi_ref[:, (hh // 2) * LANES:(hh // 2 + 1) * LANES]
            kc = ki_ref[ks, (hh % 2) * LANES:(hh % 2 + 1) * LANES]
            acc = acc + jnp.maximum(_dot_nt(qc, kc), 0.0) * wi[:, hh:hh + 1]
        causal = (col0 + kt * tk) <= row
        sc_ref[kt] = jnp.where(causal, acc, neg_inf)
        rmax = jnp.maximum(rmax, jnp.max(jnp.where(causal, acc, neg_inf), axis=1, keepdims=True))
        rmin = jnp.minimum(rmin, jnp.min(jnp.where(causal, acc, -neg_inf), axis=1, keepdims=True))
        return rmax, rmin

    rmax, rmin = lax.fori_loop(0, n_kt, score_tile,
                               (jnp.full((tq, 1), neg_inf), jnp.full((tq, 1), -neg_inf)))

    kf = jnp.float32(k_sel)

    zeros_part = jnp.zeros((tq, LANES), jnp.float32)
    ninf_part = jnp.full((tq, LANES), neg_inf)

    def count_ge(x):
        def body(kt, c):
            return c + _lane_fold(jnp.where(sc_ref[kt] >= x, 1.0, 0.0), jnp.add)
        return jnp.sum(lax.fori_loop(0, n_kt, body, zeros_part), axis=1, keepdims=True)

    def bisect(_, carry):
        lo, hi = carry
        mid = 0.5 * lo + 0.5 * hi
        ge = count_ge(mid) >= kf
        return jnp.where(ge, mid, lo), jnp.where(ge, hi, mid)

    lo, hi = lax.fori_loop(0, BISECT_ITERS, bisect, (rmin, rmax))

    few = (i * tq + lax.broadcasted_iota(jnp.int32, (tq, 1), 0)) < k_sel

    def max_le(x):
        def body(kt, c):
            s = sc_ref[kt]
            return jnp.maximum(c, _lane_fold(jnp.where(s <= x, s, neg_inf), jnp.maximum))
        return jnp.max(lax.fori_loop(0, n_kt, body, ninf_part), axis=1, keepdims=True)

    def probe(x):
        def body(kt, c):
            cnt, nxt = c
            s = sc_ref[kt]
            cnt = cnt + _lane_fold(jnp.where(s >= x, 1.0, 0.0), jnp.add)
            nxt = jnp.maximum(nxt, _lane_fold(jnp.where(s < x, s, neg_inf), jnp.maximum))
            return cnt, nxt
        cnt, nxt = lax.fori_loop(0, n_kt, body, (zeros_part, ninf_part))
        return jnp.sum(cnt, axis=1, keepdims=True), jnp.max(nxt, axis=1, keepdims=True)

    cand0 = jnp.where(few, rmin, max_le(hi))
    cnt0, nxt0 = probe(cand0)

    def unresolved(cnt):
        return jnp.logical_and(jnp.logical_not(few), cnt < kf)

    def finish_cond(c):
        _, cnt, _ = c
        return jnp.max(jnp.where(unresolved(cnt), 1.0, 0.0)) > 0.0

    def finish_body(c):
        cand, cnt, nxt = c
        cand = jnp.where(unresolved(cnt), nxt, cand)
        cnt, nxt = probe(cand)
        return cand, cnt, nxt

    thr, cnt_ge, _ = lax.while_loop(finish_cond, finish_body, (cand0, cnt0, nxt0))

    tied = jnp.logical_and(jnp.logical_not(few), cnt_ge > kf)

    @pl.when(jnp.max(jnp.where(tied, 1.0, 0.0)) > 0.0)
    def _():
        def gt_body(kt, c):
            return c + _lane_fold(jnp.where(sc_ref[kt] > thr, 1.0, 0.0), jnp.add)
        cnt_gt = jnp.sum(lax.fori_loop(0, n_kt, gt_body, zeros_part), axis=1, keepdims=True)
        need = kf - cnt_gt
        tri = (lax.broadcasted_iota(jnp.int32, (tk, tk), 0)
               <= lax.broadcasted_iota(jnp.int32, (tk, tk), 1)).astype(jnp.bfloat16)

        def drop_body(kt, run):
            s = sc_ref[kt]
            eq = jnp.logical_and(s == thr, tied)
            eqf = jnp.where(eq, 1.0, 0.0)
            rank = run + _dot(eqf.astype(jnp.bfloat16), tri)
            sc_ref[kt] = jnp.where(jnp.logical_and(eq, rank > need), neg_inf, s)
            return run + jnp.sum(eqf, axis=1, keepdims=True)

        lax.fori_loop(0, n_kt, drop_body, jnp.zeros((tq, 1), jnp.float32))

    for c in range(n_heads // 2):
        qc = q_ref[:, c * LANES:(c + 1) * LANES].astype(jnp.float32)
        for half in range(2):
            h = 2 * c + half
            n, g = h // KV_GROUP, h % KV_GROUP
            x = jnp.where(is_head, qc if half == 0 else pltpu.roll(qc, HEAD_DIM, axis=1), 0.0)
            qss = jnp.sum(x * x, axis=1, keepdims=True)
            bound = jnp.sqrt(qss * kn_ref[n][0:1, 0:1]) * BOUND_SLACK
            qx_ref[n, g * tq:(g + 1) * tq, :] = jnp.where(lane == HEAD_DIM, -bound, x).astype(qx_ref.dtype)

    def attend(online):
        acc_ref[...] = jnp.zeros_like(acc_ref)
        if online:
            m_ref[...] = jnp.full_like(m_ref, neg_inf)

        def att_tile(kt, carry):
            ks = pl.ds(pl.multiple_of(kt * tk, tk), tk)
            keep = sc_ref[kt] >= thr
            keep_b = jnp.where(keep, 1.0, 0.0).astype(jnp.bfloat16)
            for n in range(n_kv):
                s = _dot_nt(qx_ref[n], k_ref[ks, n * LANES:(n + 1) * LANES])
                vc = v_ref[ks, n * LANES:(n + 1) * LANES]
                if online:
                    s = jnp.where(jnp.concatenate([keep] * KV_GROUP, axis=0), s, neg_inf)
                    m_old = m_ref[n]
                    m_new = jnp.maximum(m_old, jnp.max(s, axis=1, keepdims=True))
                    m_safe = jnp.where(m_new == neg_inf, 0.0, m_new)
                    p = jnp.exp2(s - m_safe).astype(jnp.bfloat16)
                    acc_ref[n] = jnp.exp2(m_old - m_safe) * acc_ref[n] + _dot(p, vc)
                    m_ref[n] = m_new
                else:
                    p = jnp.exp2(s).astype(jnp.bfloat16).reshape(KV_GROUP, tq, tk) * keep_b[None]
                    acc_ref[n] += _dot(p.reshape(KV_GROUP * tq, tk), vc)
            return carry

        lax.fori_loop(0, n_kt, att_tile, 0)

    attend(False)
    den = acc_ref[0]
    for n in range(1, n_kv):
        den = jnp.minimum(den, acc_ref[n])
    den_lane = lax.broadcasted_iota(jnp.int32, den.shape, 1) >= HEAD_DIM
    safe = jnp.min(jnp.where(den_lane, den, 1.0)) > MIN_DENOMINATOR

    @pl.when(jnp.logical_not(safe))
    def _():
        attend(True)

    for c in range(n_heads // 2):
        parts = []
        for half in range(2):
            h = 2 * c + half
            n, g = h // KV_GROUP, h % KV_GROUP
            a = acc_ref[n, g * tq:(g + 1) * tq, :]
            parts.append(a / pltpu.roll(a, HEAD_DIM, axis=1))
        out = jnp.where(is_head, parts[0], pltpu.roll(parts[1], HEAD_DIM, axis=1))
        o_ref[:, c * LANES:(c + 1) * LANES] = out.astype(o_ref.dtype)


def _dsa(q, qi, wi, kx, vx, kix, bsz, seq, k_sel):
    t, n_q = q.shape
    n_heads = n_q // HEAD_DIM
    n_kv = n_heads // KV_GROUP
    n_idx = qi.shape[1] // IDX_DIM
    tq = min(Q_TILE, seq)
    tk = min(K_TILE, seq)
    nq = seq // tq
    row = lambda b, i: (b * nq + i, 0)
    per_batch = lambda b, i: (b, 0)
    kern = functools.partial(_dsa_kernel, tq=tq, tk=tk, k_sel=k_sel, n_heads=n_heads, n_idx=n_idx)
    return pl.pallas_call(
        kern,
        out_shape=jax.ShapeDtypeStruct((t, n_q), jnp.bfloat16),
        grid=(bsz, nq),
        in_specs=[
            pl.BlockSpec((tq, n_q), row),
            pl.BlockSpec((tq, qi.shape[1]), row),
            pl.BlockSpec((tq, LANES), row),
            pl.BlockSpec((seq, kx.shape[1]), per_batch),
            pl.BlockSpec((seq, vx.shape[1]), per_batch),
            pl.BlockSpec((seq, kix.shape[1]), per_batch),
        ],
        out_specs=pl.BlockSpec((tq, n_q), row),
        scratch_shapes=[
            pltpu.VMEM((seq // tk, tq, tk), jnp.float32),
            pltpu.VMEM((n_kv, KV_GROUP * tq, LANES), jnp.bfloat16),
            pltpu.VMEM((n_kv, KV_GROUP * tq, LANES), jnp.float32),
            pltpu.VMEM((n_kv, KV_GROUP * tq, 1), jnp.float32),
            pltpu.VMEM((n_kv, SUBLANES, LANES), jnp.float32),
        ],
        compiler_params=_params("parallel", "arbitrary"),
        name="dsa",
    )(q, qi, wi, kx, vx, kix)


def _att_out_kernel(a_ref, h_ref, g_ref, w_ref, o_ref):
    o_ref[...] = h_ref[...] + _rms(_dot(a_ref[...], w_ref[...]), g_ref[...])


def _att_out(a, h, g, w):
    t, d = h.shape
    n = a.shape[1]
    tm = min(TOKEN_TILE, t)
    return pl.pallas_call(
        _att_out_kernel,
        out_shape=jax.ShapeDtypeStruct(h.shape, jnp.float32),
        grid=(t // tm,),
        in_specs=[
            pl.BlockSpec((tm, n), lambda i: (i, 0)),
            pl.BlockSpec((tm, d), lambda i: (i, 0)),
            pl.BlockSpec((1, d), lambda i: (0, 0)),
            pl.BlockSpec((n, d), lambda i: (0, 0)),
        ],
        out_specs=pl.BlockSpec((tm, d), lambda i: (i, 0)),
        compiler_params=_params("parallel"),
        name="att_out",
    )(a, h, g, w)


def _rope_tables(seq):
    half = HEAD_DIM // 2
    inv_freq = ROPE_THETA ** (-jnp.arange(half, dtype=jnp.float32) * 2.0 / HEAD_DIM)
    ang = jnp.arange(seq, dtype=jnp.float32)[:, None] * inv_freq[None, :]
    cos = jnp.concatenate([jnp.cos(ang)] * 4, axis=-1)
    sin = jnp.concatenate([-jnp.sin(ang), jnp.sin(ang)] * 2, axis=-1)
    return cos, sin


def _half_chunks(w, low):
    d, n = w.shape
    w3 = w.reshape(d, n // HEAD_DIM, HEAD_DIM)
    z = jnp.zeros_like(w3)
    return jnp.concatenate([w3, z] if low else [z, w3], axis=-1).reshape(d, 2 * n)


def _att_weights(w_in, d_model):
    n_heads = d_model // HEAD_DIM
    n_kv = n_heads // KV_GROUP
    n_idx = max(4, d_model // 128)
    splits = (n_heads * HEAD_DIM, n_kv * HEAD_DIM, n_kv * HEAD_DIM, n_idx * IDX_DIM, IDX_DIM, n_idx)
    offs = [0]
    for s in splits:
        offs.append(offs[-1] + s)
    wq, wk, wv, wqi, wki, wwi = (w_in[:, offs[j]:offs[j + 1]] for j in range(6))
    d = w_in.shape[0]

    def both_halves(w):
        lo = _half_chunks(w, True).reshape(d, -1, LANES)
        hi = _half_chunks(w, False).reshape(d, -1, LANES)
        return jnp.stack([lo, hi], axis=2).reshape(d, -1)

    wkx, wvx, wkix = _half_chunks(wk, True), _half_chunks(wv, True), both_halves(wki)
    wwi_p = jnp.pad(wwi, ((0, 0), (0, LANES - n_idx)))
    w_cat = jnp.concatenate([wq, wkx, wvx, wqi, wkix, wwi_p], axis=1).astype(jnp.bfloat16)
    widths = (wq.shape[1], wkx.shape[1], wvx.shape[1], wqi.shape[1], wkix.shape[1])
    return w_cat, widths, n_idx


def kernel(x, norm_g, mlp_w1, mlp_w2, ssm_w_in, ssm_lam_re, ssm_lam_im, ssm_log_dt, ssm_b_re, ssm_b_im,
           ssm_c_re, ssm_c_im, ssm_d, ssm_w_glu, ssm_w_out, att_w_in, att_w_out):
    bsz, seq, d_model = x.shape
    depth = norm_g.shape[0]
    bf = jnp.bfloat16
    h = x.reshape(bsz * seq, d_model)
    cos, sin = _rope_tables(seq)
    k_sel = min(TOPK_MAX, seq // 4)
    for i in range(depth):
        j = i // 2
        g = norm_g[i][:, None, :]
        if i % 2 == 0:
            ar, ai, bmat, cmat, dvec = _ssm_discretise(
                ssm_lam_re[j], ssm_lam_im[j], ssm_log_dt[j], ssm_b_re[j], ssm_b_im[j],
                ssm_c_re[j], ssm_c_im[j], ssm_d[j], bsz)
            e = ssm_w_in.shape[2]
            u = _ssm_in(h, g[0], ssm_w_in[j].astype(bf), bsz, seq)
            z = _ssm_scan(u.reshape(seq * bsz, e), ar, ai, bmat, cmat, dvec, bsz)
            h = _ssm_out(z.reshape(seq, bsz * e), h, g[1], ssm_w_glu[j].astype(bf),
                         ssm_w_out[j].astype(bf), bsz, seq)
        else:
            w_cat, widths, n_idx = _att_weights(att_w_in[j], d_model)
            q, kx, vx, qi, kix, wi = _att_in(h, g[0], w_cat, cos, sin, bsz, seq, widths,
                                             HEAD_DIM ** -0.5 * LOG2_E, n_idx ** -0.5 * IDX_DIM ** -0.5)
            a = _dsa(q, qi, wi, kx, vx, kix, bsz, seq, k_sel)
            h = _att_out(a, h, g[1], att_w_out[j].astype(bf))
        h = _mlp(h, g[2], g[3], mlp_w1[i].astype(bf), mlp_w2[i].astype(bf))
    return h.reshape(bsz, seq, d_model)
```

```python
import functools
import math

import jax
import jax.numpy as jnp
from jax import lax
from jax.experimental import pallas as pl
from jax.experimental.pallas import tpu as pltpu

NORM_EPS = 1e-6
SSM_CH = 16
SSM_STATE = 64
DT_MIN, DT_MAX = 1e-3, 1e-1
HEAD_DIM = 64
KV_GROUP = 4
IDX_DIM = 64
TOPK_MAX = 256
ROPE_THETA = 10000.0

LANES = 128
SUBLANES = 8
VMEM_LIMIT_BYTES = 56 * 1024 * 1024

TOKEN_TILE = 512
FF_TILE = 1024
SCAN_CHUNK = 64
SCAN_LANES = 512
Q_TILE = 256
K_TILE = 512
BISECT_ITERS = 20
FOLD_ROWS = 64
LOG2_E = math.log2(math.e)
BOUND_SLACK = 1.01
MIN_DENOMINATOR = 2.0 ** -60


def _params(*sem):
    return pltpu.CompilerParams(dimension_semantics=sem, vmem_limit_bytes=VMEM_LIMIT_BYTES)


def _rms(x, g):
    return x * lax.rsqrt(jnp.mean(x * x, axis=-1, keepdims=True) + NORM_EPS) * g


def _dot(a, b):
    return jnp.dot(a, b, preferred_element_type=jnp.float32)


def _dot_nt(a, b):
    return lax.dot_general(a, b, (((1,), (1,)), ((), ())), preferred_element_type=jnp.float32)


def _mlp_kernel(h_ref, g_in_ref, g_out_ref, w1_ref, w2_ref, o_ref, xn_ref, acc_ref):
    j = pl.program_id(1)

    @pl.when(j == 0)
    def _():
        xn_ref[...] = _rms(h_ref[...], g_in_ref[...]).astype(jnp.bfloat16)
        acc_ref[...] = jnp.zeros_like(acc_ref)

    a = jnp.maximum(_dot(xn_ref[...], w1_ref[...]), 0.0)
    acc_ref[...] += _dot((a * a).astype(jnp.bfloat16), w2_ref[...])

    @pl.when(j == pl.num_programs(1) - 1)
    def _():
        o_ref[...] = h_ref[...] + _rms(acc_ref[...], g_out_ref[...])


def _mlp(h, g_in, g_out, w1, w2):
    t, d = h.shape
    ff = w1.shape[1]
    tm = min(TOKEN_TILE, t)
    tf = min(FF_TILE, ff)
    return pl.pallas_call(
        _mlp_kernel,
        out_shape=jax.ShapeDtypeStruct((t, d), jnp.float32),
        grid=(t // tm, ff // tf),
        in_specs=[
            pl.BlockSpec((tm, d), lambda i, j: (i, 0)),
            pl.BlockSpec((1, d), lambda i, j: (0, 0)),
            pl.BlockSpec((1, d), lambda i, j: (0, 0)),
            pl.BlockSpec((d, tf), lambda i, j: (0, j)),
            pl.BlockSpec((tf, d), lambda i, j: (j, 0)),
        ],
        out_specs=pl.BlockSpec((tm, d), lambda i, j: (i, 0)),
        scratch_shapes=[pltpu.VMEM((tm, d), jnp.bfloat16), pltpu.VMEM((tm, d), jnp.float32)],
        compiler_params=_params("parallel", "arbitrary"),
        name="mlp",
    )(h, g_in, g_out, w1, w2)


def _ssm_in_kernel(h_ref, g_ref, w_ref, u_ref):
    xn = _rms(h_ref[...], g_ref[...]).astype(jnp.bfloat16)
    u_ref[...] = _dot(xn, w_ref[...])


def _ssm_in(h, g, w, bsz, seq):
    d = h.shape[1]
    e = w.shape[1]
    tm = min(TOKEN_TILE, seq)
    nt = seq // tm
    return pl.pallas_call(
        _ssm_in_kernel,
        out_shape=jax.ShapeDtypeStruct((seq, bsz * e), jnp.float32),
        grid=(bsz, nt),
        in_specs=[
            pl.BlockSpec((tm, d), lambda b, i: (b * nt + i, 0)),
            pl.BlockSpec((1, d), lambda b, i: (0, 0)),
            pl.BlockSpec((d, e), lambda b, i: (0, 0)),
        ],
        out_specs=pl.BlockSpec((tm, e), lambda b, i: (i, b)),
        compiler_params=_params("parallel", "parallel"),
        name="ssm_in",
    )(h, g, w)


def _ssm_scan_kernel(u_ref, ar_ref, ai_ref, bmat_ref, cmat_ref, d_ref, z_ref, x_ref, st_ref,
                     *, bsz, chunk, half, lanes):
    @pl.when(pl.program_id(0) == 0)
    def _():
        st_ref[...] = jnp.zeros_like(st_ref)

    u = u_ref[...]
    x_ref[...] = _dot(u.astype(jnp.bfloat16), bmat_ref[...])

    for c in range(half // lanes):
        re = slice(c * lanes, (c + 1) * lanes)
        im = slice(half + c * lanes, half + (c + 1) * lanes)
        ar = ar_ref[:, re]
        ai = ai_ref[:, re]

        def step(t, carry, re=re, im=im, ar=ar, ai=ai):
            xr, xi = carry
            rows = pl.ds(pl.multiple_of(t * bsz, bsz), bsz)
            nr = ar * xr - ai * xi + x_ref[rows, re]
            ni = ar * xi + ai * xr + x_ref[rows, im]
            x_ref[rows, re] = nr
            x_ref[rows, im] = ni
            return nr, ni

        xr, xi = lax.fori_loop(0, chunk, step, (st_ref[:, re], st_ref[:, im]))
        st_ref[:, re] = xr
        st_ref[:, im] = xi

    y = _dot(x_ref[...].astype(jnp.bfloat16), cmat_ref[...]) + d_ref[...] * u
    z_ref[...] = jax.nn.gelu(y)


def _ssm_scan(u2, ar, ai, bmat, cmat, dvec, bsz):
    rows, e = u2.shape
    seq = rows // bsz
    n2 = bmat.shape[1]
    half = n2 // 2
    chunk = min(SCAN_CHUNK, seq)
    lanes = min(SCAN_LANES, half)
    kern = functools.partial(_ssm_scan_kernel, bsz=bsz, chunk=chunk, half=half, lanes=lanes)
    return pl.pallas_call(
        kern,
        out_shape=jax.ShapeDtypeStruct((rows, e), jnp.float32),
        grid=(seq // chunk,),
        in_specs=[
            pl.BlockSpec((chunk * bsz, e), lambda i: (i, 0)),
            pl.BlockSpec((bsz, half), lambda i: (0, 0)),
            pl.BlockSpec((bsz, half), lambda i: (0, 0)),
            pl.BlockSpec((e, n2), lambda i: (0, 0)),
            pl.BlockSpec((n2, e), lambda i: (0, 0)),
            pl.BlockSpec((1, e), lambda i: (0, 0)),
        ],
        out_specs=pl.BlockSpec((chunk * bsz, e), lambda i: (i, 0)),
        scratch_shapes=[pltpu.VMEM((chunk * bsz, n2), jnp.float32),
                        pltpu.VMEM((bsz, n2), jnp.float32)],
        compiler_params=_params("arbitrary"),
        name="ssm_scan",
    )(u2, ar, ai, bmat, cmat, dvec)


def _ssm_out_kernel(z_ref, h_ref, g_ref, wg_ref, wo_ref, o_ref):
    z = z_ref[...]
    zz = z * jax.nn.sigmoid(_dot(z.astype(jnp.bfloat16), wg_ref[...]))
    mix = _dot(zz.astype(jnp.bfloat16), wo_ref[...])
    o_ref[...] = h_ref[...] + _rms(mix, g_ref[...])


def _ssm_out(z, h, g, wg, wo, bsz, seq):
    d = h.shape[1]
    e = wg.shape[0]
    tm = min(TOKEN_TILE, seq)
    nt = seq // tm
    return pl.pallas_call(
        _ssm_out_kernel,
        out_shape=jax.ShapeDtypeStruct(h.shape, jnp.float32),
        grid=(bsz, nt),
        in_specs=[
            pl.BlockSpec((tm, e), lambda b, i: (i, b)),
            pl.BlockSpec((tm, d), lambda b, i: (b * nt + i, 0)),
            pl.BlockSpec((1, d), lambda b, i: (0, 0)),
            pl.BlockSpec((e, e), lambda b, i: (0, 0)),
            pl.BlockSpec((e, d), lambda b, i: (0, 0)),
        ],
        out_specs=pl.BlockSpec((tm, d), lambda b, i: (b * nt + i, 0)),
        compiler_params=_params("parallel", "parallel"),
        name="ssm_out",
    )(z, h, g, wg, wo)


def _ssm_discretise(lam_re, lam_im, log_dt, b_re, b_im, c_re, c_im, d_skip, bsz):
    g, p = lam_re.shape
    c = b_re.shape[2]
    dt = jnp.exp(log_dt)[:, None]
    mag = jnp.exp(lam_re * dt)
    abar_re = mag * jnp.cos(lam_im * dt)
    abar_im = mag * jnp.sin(lam_im * dt)
    den = lam_re * lam_re + lam_im * lam_im
    nr = abar_re - 1.0
    ni = abar_im
    fr = (nr * lam_re + ni * lam_im) / den
    fi = (ni * lam_re - nr * lam_im) / den
    bbar_re = fr[..., None] * b_re - fi[..., None] * b_im
    bbar_im = fr[..., None] * b_im + fi[..., None] * b_re
    eye = jnp.eye(g, dtype=jnp.float32)
    bd = lambda m: jnp.einsum('gpc,gh->gchp', m, eye).reshape(g * c, g * p)
    bmat = jnp.concatenate([bd(bbar_re), bd(bbar_im)], axis=1).astype(jnp.bfloat16)
    cd = lambda m: jnp.einsum('gcp,gh->gphc', m, eye).reshape(g * p, g * c)
    cmat = jnp.concatenate([cd(c_re), cd(-c_im)], axis=0).astype(jnp.bfloat16)
    ar = jnp.broadcast_to(abar_re.reshape(1, g * p), (bsz, g * p))
    ai = jnp.broadcast_to(abar_im.reshape(1, g * p), (bsz, g * p))
    return ar, ai, bmat, cmat, d_skip.reshape(1, g * c)


def _rope128(x, cos, sin_signed, lane):
    swapped = jnp.where((lane % HEAD_DIM) < HEAD_DIM // 2,
                        pltpu.roll(x, LANES - HEAD_DIM // 2, axis=1),
                        pltpu.roll(x, HEAD_DIM // 2, axis=1))
    return x * cos + swapped * sin_signed


def _att_in_kernel(h_ref, g_ref, w_ref, cos_ref, sin_ref, q_ref, k_ref, v_ref, qi_ref, ki_ref, wi_ref,
                   *, n_q, n_k, n_v, n_qi, n_ki, q_scale, wi_scale):
    xn = _rms(h_ref[...], g_ref[...]).astype(jnp.bfloat16)
    proj = _dot(xn, w_ref[...])
    cos = cos_ref[...]
    sin = sin_ref[...]
    lane = lax.broadcasted_iota(jnp.int32, cos.shape, 1)
    off = 0

    def roped(ref, width, off, scale=None):
        for c in range(width // LANES):
            x = _rope128(proj[:, off + c * LANES: off + (c + 1) * LANES], cos, sin, lane)
            if scale is not None:
                x = x * scale
            ref[:, c * LANES:(c + 1) * LANES] = x.astype(ref.dtype)
        return off + width

    off = roped(q_ref, n_q, off, q_scale)
    for c in range(n_k // LANES):
        x = _rope128(proj[:, off + c * LANES: off + (c + 1) * LANES], cos, sin, lane)
        k_ref[:, c * LANES:(c + 1) * LANES] = jnp.where(lane == HEAD_DIM, 1.0, x).astype(k_ref.dtype)
    off += n_k
    for c in range(n_v // LANES):
        x = proj[:, off + c * LANES: off + (c + 1) * LANES]
        v_ref[:, c * LANES:(c + 1) * LANES] = jnp.where(lane >= HEAD_DIM, 1.0, x).astype(v_ref.dtype)
    off += n_v
    off = roped(qi_ref, n_qi, off)
    off = roped(ki_ref, n_ki, off)
    wi_ref[...] = proj[:, off:off + LANES] * wi_scale


def _att_in(h, g, w, cos, sin, bsz, seq, widths, q_scale, wi_scale):
    d = h.shape[1]
    n_q, n_k, n_v, n_qi, n_ki = widths
    ncols = w.shape[1]
    tm = min(TOKEN_TILE, seq)
    nt = seq // tm
    t = bsz * seq
    row = lambda b, i: (b * nt + i, 0)
    kern = functools.partial(_att_in_kernel, n_q=n_q, n_k=n_k, n_v=n_v, n_qi=n_qi, n_ki=n_ki,
                             q_scale=q_scale, wi_scale=wi_scale)
    bf = jnp.bfloat16
    return pl.pallas_call(
        kern,
        out_shape=[jax.ShapeDtypeStruct((t, n_q), bf), jax.ShapeDtypeStruct((t, n_k), bf),
                   jax.ShapeDtypeStruct((t, n_v), bf), jax.ShapeDtypeStruct((t, n_qi), bf),
                   jax.ShapeDtypeStruct((t, n_ki), bf), jax.ShapeDtypeStruct((t, LANES), jnp.float32)],
        grid=(bsz, nt),
        in_specs=[
            pl.BlockSpec((tm, d), row),
            pl.BlockSpec((1, d), lambda b, i: (0, 0)),
            pl.BlockSpec((d, ncols), lambda b, i: (0, 0)),
            pl.BlockSpec((tm, LANES), lambda b, i: (i, 0)),
            pl.BlockSpec((tm, LANES), lambda b, i: (i, 0)),
        ],
        out_specs=[pl.BlockSpec((tm, n_q), row), pl.BlockSpec((tm, n_k), row), pl.BlockSpec((tm, n_v), row),
                   pl.BlockSpec((tm, n_qi), row), pl.BlockSpec((tm, n_ki), row), pl.BlockSpec((tm, LANES), row)],
        compiler_params=_params("parallel", "parallel"),
        name="att_in",
    )(h, g, w, cos, sin)


def _lane_fold(x, op):
    out = x[:, :LANES]
    for j in range(1, x.shape[1] // LANES):
        out = op(out, x[:, j * LANES:(j + 1) * LANES])
    return out


def _row_fold(x, op):
    out = x[:FOLD_ROWS]
    for j in range(1, x.shape[0] // FOLD_ROWS):
        out = op(out, x[j * FOLD_ROWS:(j + 1) * FOLD_ROWS])
    return out


def _dsa_kernel(q_ref, qi_ref, wi_ref, k_ref, v_ref, ki_ref, o_ref,
                sc_ref, sct_ref, qx_ref, acc_ref, m_ref, kn_ref, *, tq, tk, k_sel, n_heads, n_idx):
    i = pl.program_id(1)
    n_kv = n_heads // KV_GROUP
    lane = lax.broadcasted_iota(jnp.int32, (tq, LANES), 1)
    is_head = lane < HEAD_DIM

    @pl.when(i == 0)
    def _():
        lane_k = lax.broadcasted_iota(jnp.int32, (tk, LANES), 1)
        head_lane = lax.broadcasted_iota(jnp.int32, kn_ref.shape, 1)
        kn = jnp.zeros(kn_ref.shape, jnp.float32)
        for n in range(n_kv):
            def body(r, c, n=n):
                x = k_ref[pl.ds(pl.multiple_of(r * tk, tk), tk), n * LANES:(n + 1) * LANES].astype(jnp.float32)
                x = jnp.where(lane_k < HEAD_DIM, x, 0.0)
                ss = jnp.sum(x * x, axis=1, keepdims=True)
                return jnp.maximum(c, jnp.max(ss, axis=0, keepdims=True))
            kmax = lax.fori_loop(0, k_ref.shape[0] // tk, body, jnp.zeros((1, 1), jnp.float32))
            kn = jnp.where(head_lane // KV_GROUP == n, kmax, kn)
        kn_ref[...] = kn

    n_kt = (i * tq + tq + tk - 1) // tk
    neg_inf = jnp.float32(-jnp.inf)
    row = i * tq + lax.broadcasted_iota(jnp.int32, (tq, tk), 0)
    col0 = lax.broadcasted_iota(jnp.int32, (tq, tk), 1)
    wi = wi_ref[...]

    def score_tile(kt, carry):
        rmax, rmin = carry
        ks = pl.ds(pl.multiple_of(kt * tk, tk), tk)
        acc = jnp.zeros((tq, tk), jnp.float32)
        for hh in range(n_idx):
            qc = qi_ref[:, (hh // 2) * LANES:(hh // 2 + 1) * LANES]
            kc = ki_ref[ks, (hh % 2) * LANES:(hh % 2 + 1) * LANES]
            acc = acc + jnp.maximum(_dot_nt(qc, kc), 0.0) * wi[:, hh:hh + 1]
        causal = (col0 + kt * tk) <= row
        masked = jnp.where(causal, acc, neg_inf)
        sc_ref[kt] = masked
        masked_t = masked.T
        sct_ref[kt] = masked_t
        rmax = jnp.maximum(rmax, _row_fold(masked_t, jnp.maximum))
        rmin = jnp.minimum(rmin, _row_fold(jnp.where(masked_t == neg_inf, -neg_inf, masked_t), jnp.minimum))
        return rmax, rmin

    rmax, rmin = lax.fori_loop(0, n_kt, score_tile,
                               (jnp.full((FOLD_ROWS, tq), neg_inf), jnp.full((FOLD_ROWS, tq), -neg_inf)))

    kf = jnp.float32(k_sel)

    def sweep(fn, init, x):
        def body(kt, c):
            for j in range(tk // FOLD_ROWS):
                c = fn(c, sct_ref[kt, j * FOLD_ROWS:(j + 1) * FOLD_ROWS, :], x)
            return c
        return lax.fori_loop(0, n_kt, body, tuple(jnp.full((FOLD_ROWS, tq), v, jnp.float32) for v in init))

    def col_sum(part):
        return jnp.sum(part, axis=0, keepdims=True)

    def col_max(part):
        return jnp.max(part, axis=0, keepdims=True)

    def count_ge(x):
        (c,) = sweep(lambda c, s, x: (c[0] + jnp.where(s >= x, 1.0, 0.0),), (0.0,), x)
        return col_sum(c)

    def bisect(_, carry):
        lo, hi = carry
        mid = 0.5 * lo + 0.5 * hi
        ge = count_ge(mid) >= kf
        return jnp.where(ge, mid, lo), jnp.where(ge, hi, mid)

    rmin = -col_max(-rmin)
    rmax = col_max(rmax)
    lo, hi = lax.fori_loop(0, BISECT_ITERS, bisect, (rmin, rmax))

    few = (i * tq + lax.broadcasted_iota(jnp.int32, (1, tq), 1)) < k_sel

    def max_le(x):
        (c,) = sweep(lambda c, s, x: (jnp.maximum(c[0], jnp.where(s <= x, s, neg_inf)),), (-jnp.inf,), x)
        return col_max(c)

    def probe(x):
        cnt, nxt = sweep(lambda c, s, x: (c[0] + jnp.where(s >= x, 1.0, 0.0),
                                          jnp.maximum(c[1], jnp.where(s < x, s, neg_inf))),
                         (0.0, -jnp.inf), x)
        return col_sum(cnt), col_max(nxt)

    cand0 = jnp.where(few, rmin, max_le(hi))
    cnt0, nxt0 = probe(cand0)

    def unresolved(cnt):
        return jnp.logical_and(jnp.logical_not(few), cnt < kf)

    def finish_cond(c):
        _, cnt, _ = c
        return jnp.max(jnp.where(unresolved(cnt), 1.0, 0.0)) > 0.0

    def finish_body(c):
        cand, cnt, nxt = c
        cand = jnp.where(unresolved(cnt), nxt, cand)
        cnt, nxt = probe(cand)
        return cand, cnt, nxt

    thr_t, cnt_ge, _ = lax.while_loop(finish_cond, finish_body, (cand0, cnt0, nxt0))

    def to_rows(x):
        return jnp.broadcast_to(x, (LANES, tq)).T

    thr = to_rows(thr_t)

    tied_t = jnp.logical_and(jnp.logical_not(few), cnt_ge > kf)

    @pl.when(jnp.max(jnp.where(tied_t, 1.0, 0.0)) > 0.0)
    def _():
        (c,) = sweep(lambda c, s, x: (c[0] + jnp.where(s > x, 1.0, 0.0),), (0.0,), thr_t)
        need = to_rows(kf - col_sum(c))[:, :1]
        thr1 = thr[:, :1]
        tied1 = to_rows(jnp.where(tied_t, 1.0, 0.0))[:, :1] > 0.0
        tri = (lax.broadcasted_iota(jnp.int32, (tk, tk), 0)
               <= lax.broadcasted_iota(jnp.int32, (tk, tk), 1)).astype(jnp.bfloat16)

        def drop_body(kt, run):
            s = sc_ref[kt]
            eq = jnp.logical_and(s == thr1, tied1)
            eqf = jnp.where(eq, 1.0, 0.0)
            rank = run + _dot(eqf.astype(jnp.bfloat16), tri)
            sc_ref[kt] = jnp.where(jnp.logical_and(eq, rank > need), neg_inf, s)
            return run + jnp.sum(eqf, axis=1, keepdims=True)

        lax.fori_loop(0, n_kt, drop_body, jnp.zeros((tq, 1), jnp.float32))

    qf = q_ref[...].astype(jnp.float32)
    n_q = qf.shape[1]
    head_of_col = lax.broadcasted_iota(jnp.int32, (n_q, LANES), 0) // HEAD_DIM
    head_sel = jnp.where(head_of_col == lax.broadcasted_iota(jnp.int32, (n_q, LANES), 1), 1.0, 0.0)
    qss = _dot((qf * qf).astype(jnp.bfloat16), head_sel.astype(jnp.bfloat16))
    bound = jnp.sqrt(qss * kn_ref[0:1, :]) * BOUND_SLACK
    for c in range(n_heads // 2):
        qc = qf[:, c * LANES:(c + 1) * LANES]
        for half in range(2):
            h = 2 * c + half
            n, g = h // KV_GROUP, h % KV_GROUP
            x = qc if half == 0 else pltpu.roll(qc, HEAD_DIM, axis=1)
            b = pltpu.roll(bound, (HEAD_DIM - h) % LANES, axis=1)
            qx = jnp.where(is_head, x, jnp.where(lane == HEAD_DIM, -b, 0.0))
            qx_ref[n, g * tq:(g + 1) * tq, :] = qx.astype(qx_ref.dtype)

    thr_tile = jnp.concatenate([thr] * (tk // LANES), axis=1)

    def attend(online):
        acc_ref[...] = jnp.zeros_like(acc_ref)
        if online:
            m_ref[...] = jnp.full_like(m_ref, neg_inf)

        def att_tile(kt, carry):
            ks = pl.ds(pl.multiple_of(kt * tk, tk), tk)
            keep = sc_ref[kt] >= thr_tile
            keep_b = jnp.where(keep, 1.0, 0.0).astype(jnp.bfloat16)
            for n in range(n_kv):
                s = _dot_nt(qx_ref[n], k_ref[ks, n * LANES:(n + 1) * LANES])
                vc = v_ref[ks, n * LANES:(n + 1) * LANES]
                if online:
                    s = jnp.where(jnp.concatenate([keep] * KV_GROUP, axis=0), s, neg_inf)
                    m_old = m_ref[n]
                    m_new = jnp.maximum(m_old, jnp.max(s, axis=1, keepdims=True))
                    m_safe = jnp.where(m_new == neg_inf, 0.0, m_new)
                    p = jnp.exp2(s - m_safe).astype(jnp.bfloat16)
                    acc_ref[n] = jnp.exp2(m_old - m_safe) * acc_ref[n] + _dot(p, vc)
                    m_ref[n] = m_new
                else:
                    p = jnp.exp2(s).astype(jnp.bfloat16).reshape(KV_GROUP, tq, tk) * keep_b[None]
                    acc_ref[n] += _dot(p.reshape(KV_GROUP * tq, tk), vc)
            return carry

        lax.fori_loop(0, n_kt, att_tile, 0)

    attend(False)
    den = acc_ref[0]
    for n in range(1, n_kv):
        den = jnp.minimum(den, acc_ref[n])
    den_lane = lax.broadcasted_iota(jnp.int32, den.shape, 1) >= HEAD_DIM
    safe = jnp.min(jnp.where(den_lane, den, 1.0)) > MIN_DENOMINATOR

    @pl.when(jnp.logical_not(safe))
    def _():
        attend(True)

    for c in range(n_heads // 2):
        parts = []
        for half in range(2):
            h = 2 * c + half
            n, g = h // KV_GROUP, h % KV_GROUP
            a = acc_ref[n, g * tq:(g + 1) * tq, :]
            parts.append(a / pltpu.roll(a, HEAD_DIM, axis=1))
        out = jnp.where(is_head, parts[0], pltpu.roll(parts[1], HEAD_DIM, axis=1))
        o_ref[:, c * LANES:(c + 1) * LANES] = out.astype(o_ref.dtype)


def _dsa(q, qi, wi, kx, vx, kix, bsz, seq, k_sel):
    t, n_q = q.shape
    n_heads = n_q // HEAD_DIM
    n_kv = n_heads // KV_GROUP
    n_idx = qi.shape[1] // IDX_DIM
    tq = min(Q_TILE, seq)
    tk = min(K_TILE, seq)
    nq = seq // tq
    row = lambda b, i: (b * nq + i, 0)
    per_batch = lambda b, i: (b, 0)
    kern = functools.partial(_dsa_kernel, tq=tq, tk=tk, k_sel=k_sel, n_heads=n_heads, n_idx=n_idx)
    return pl.pallas_call(
        kern,
        out_shape=jax.ShapeDtypeStruct((t, n_q), jnp.bfloat16),
        grid=(bsz, nq),
        in_specs=[
            pl.BlockSpec((tq, n_q), row),
            pl.BlockSpec((tq, qi.shape[1]), row),
            pl.BlockSpec((tq, LANES), row),
            pl.BlockSpec((seq, kx.shape[1]), per_batch),
            pl.BlockSpec((seq, vx.shape[1]), per_batch),
            pl.BlockSpec((seq, kix.shape[1]), per_batch),
        ],
        out_specs=pl.BlockSpec((tq, n_q), row),
        scratch_shapes=[
            pltpu.VMEM((seq // tk, tq, tk), jnp.float32),
            pltpu.VMEM((seq // tk, tk, tq), jnp.float32),
            pltpu.VMEM((n_kv, KV_GROUP * tq, LANES), jnp.bfloat16),
            pltpu.VMEM((n_kv, KV_GROUP * tq, LANES), jnp.float32),
            pltpu.VMEM((n_kv, KV_GROUP * tq, 1), jnp.float32),
            pltpu.VMEM((SUBLANES, LANES), jnp.float32),
        ],
        compiler_params=_params("parallel", "arbitrary"),
        name="dsa",
    )(q, qi, wi, kx, vx, kix)


def _att_out_kernel(a_ref, h_ref, g_ref, w_ref, o_ref):
    o_ref[...] = h_ref[...] + _rms(_dot(a_ref[...], w_ref[...]), g_ref[...])


def _att_out(a, h, g, w):
    t, d = h.shape
    n = a.shape[1]
    tm = min(TOKEN_TILE, t)
    return pl.pallas_call(
        _att_out_kernel,
        out_shape=jax.ShapeDtypeStruct(h.shape, jnp.float32),
        grid=(t // tm,),
        in_specs=[
            pl.BlockSpec((tm, n), lambda i: (i, 0)),
            pl.BlockSpec((tm, d), lambda i: (i, 0)),
            pl.BlockSpec((1, d), lambda i: (0, 0)),
            pl.BlockSpec((n, d), lambda i: (0, 0)),
        ],
        out_specs=pl.BlockSpec((tm, d), lambda i: (i, 0)),
        compiler_params=_params("parallel"),
        name="att_out",
    )(a, h, g, w)


def _rope_tables(seq):
    half = HEAD_DIM // 2
    inv_freq = ROPE_THETA ** (-jnp.arange(half, dtype=jnp.float32) * 2.0 / HEAD_DIM)
    ang = jnp.arange(seq, dtype=jnp.float32)[:, None] * inv_freq[None, :]
    cos = jnp.concatenate([jnp.cos(ang)] * 4, axis=-1)
    sin = jnp.concatenate([-jnp.sin(ang), jnp.sin(ang)] * 2, axis=-1)
    return cos, sin


def _half_chunks(w, low):
    d, n = w.shape
    w3 = w.reshape(d, n // HEAD_DIM, HEAD_DIM)
    z = jnp.zeros_like(w3)
    return jnp.concatenate([w3, z] if low else [z, w3], axis=-1).reshape(d, 2 * n)


def _att_weights(w_in, d_model):
    n_heads = d_model // HEAD_DIM
    n_kv = n_heads // KV_GROUP
    n_idx = max(4, d_model // 128)
    splits = (n_heads * HEAD_DIM, n_kv * HEAD_DIM, n_kv * HEAD_DIM, n_idx * IDX_DIM, IDX_DIM, n_idx)
    offs = [0]
    for s in splits:
        offs.append(offs[-1] + s)
    wq, wk, wv, wqi, wki, wwi = (w_in[:, offs[j]:offs[j + 1]] for j in range(6))
    d = w_in.shape[0]

    def both_halves(w):
        lo = _half_chunks(w, True).reshape(d, -1, LANES)
        hi = _half_chunks(w, False).reshape(d, -1, LANES)
        return jnp.stack([lo, hi], axis=2).reshape(d, -1)

    wkx, wvx, wkix = _half_chunks(wk, True), _half_chunks(wv, True), both_halves(wki)
    wwi_p = jnp.pad(wwi, ((0, 0), (0, LANES - n_idx)))
    w_cat = jnp.concatenate([wq, wkx, wvx, wqi, wkix, wwi_p], axis=1).astype(jnp.bfloat16)
    widths = (wq.shape[1], wkx.shape[1], wvx.shape[1], wqi.shape[1], wkix.shape[1])
    return w_cat, widths, n_idx


def kernel(x, norm_g, mlp_w1, mlp_w2, ssm_w_in, ssm_lam_re, ssm_lam_im, ssm_log_dt, ssm_b_re, ssm_b_im,
           ssm_c_re, ssm_c_im, ssm_d, ssm_w_glu, ssm_w_out, att_w_in, att_w_out):
    bsz, seq, d_model = x.shape
    depth = norm_g.shape[0]
    bf = jnp.bfloat16
    h = x.reshape(bsz * seq, d_model)
    cos, sin = _rope_tables(seq)
    k_sel = min(TOPK_MAX, seq // 4)
    for i in range(depth):
        j = i // 2
        g = norm_g[i][:, None, :]
        if i % 2 == 0:
            ar, ai, bmat, cmat, dvec = _ssm_discretise(
                ssm_lam_re[j], ssm_lam_im[j], ssm_log_dt[j], ssm_b_re[j], ssm_b_im[j],
                ssm_c_re[j], ssm_c_im[j], ssm_d[j], bsz)
            e = ssm_w_in.shape[2]
            u = _ssm_in(h, g[0], ssm_w_in[j].astype(bf), bsz, seq)
            z = _ssm_scan(u.reshape(seq * bsz, e), ar, ai, bmat, cmat, dvec, bsz)
            h = _ssm_out(z.reshape(seq, bsz * e), h, g[1], ssm_w_glu[j].astype(bf),
                         ssm_w_out[j].astype(bf), bsz, seq)
        else:
            w_cat, widths, n_idx = _att_weights(att_w_in[j], d_model)
            q, kx, vx, qi, kix, wi = _att_in(h, g[0], w_cat, cos, sin, bsz, seq, widths,
                                             HEAD_DIM ** -0.5 * LOG2_E, n_idx ** -0.5 * IDX_DIM ** -0.5)
            a = _dsa(q, qi, wi, kx, vx, kix, bsz, seq, k_sel)
            h = _att_out(a, h, g[1], att_w_out[j].astype(bf))
        h = _mlp(h, g[2], g[3], mlp_w1[i].astype(bf), mlp_w2[i].astype(bf))
    return h.reshape(bsz, seq, d_model)
```

```python
import functools
import math

import jax
import jax.numpy as jnp
from jax import lax
from jax.experimental import pallas as pl
from jax.experimental.pallas import tpu as pltpu

NORM_EPS = 1e-6
SSM_CH = 16
SSM_STATE = 64
DT_MIN, DT_MAX = 1e-3, 1e-1
HEAD_DIM = 64
KV_GROUP = 4
IDX_DIM = 64
TOPK_MAX = 256
ROPE_THETA = 10000.0

LANES = 128
SUBLANES = 8
MXU_DIM = 256
VMEM_LIMIT_BYTES = 56 * 1024 * 1024

TOKEN_TILE = 512
FF_TILE = 1024
SCAN_CHUNK = 64
SCAN_LANES = 512
Q_TILE = 256
K_TILE = 512
BISECT_ITERS = 16
FOLD_ROWS = 64
LOG2_E = math.log2(math.e)
BOUND_SLACK = 1.01
MIN_DENOMINATOR = 2.0 ** -60


def _params(*sem):
    return pltpu.CompilerParams(dimension_semantics=sem, vmem_limit_bytes=VMEM_LIMIT_BYTES)


def _rms(x, g):
    return x * lax.rsqrt(jnp.mean(x * x, axis=-1, keepdims=True) + NORM_EPS) * g


def _dot(a, b):
    return jnp.dot(a, b, preferred_element_type=jnp.float32)


def _dot_nt(a, b):
    return lax.dot_general(a, b, (((1,), (1,)), ((), ())), preferred_element_type=jnp.float32)


def _mlp_kernel(h_ref, g_in_ref, g_out_ref, w1_ref, w2_ref, o_ref, xn_ref, acc_ref):
    j = pl.program_id(1)

    @pl.when(j == 0)
    def _():
        xn_ref[...] = _rms(h_ref[...], g_in_ref[...]).astype(jnp.bfloat16)
        acc_ref[...] = jnp.zeros_like(acc_ref)

    a = jnp.maximum(_dot(xn_ref[...], w1_ref[...]), 0.0)
    acc_ref[...] += _dot((a * a).astype(jnp.bfloat16), w2_ref[...])

    @pl.when(j == pl.num_programs(1) - 1)
    def _():
        o_ref[...] = h_ref[...] + _rms(acc_ref[...], g_out_ref[...])


def _mlp(h, g_in, g_out, w1, w2):
    t, d = h.shape
    ff = w1.shape[1]
    tm = min(TOKEN_TILE, t)
    tf = min(FF_TILE, ff)
    return pl.pallas_call(
        _mlp_kernel,
        out_shape=jax.ShapeDtypeStruct((t, d), jnp.float32),
        grid=(t // tm, ff // tf),
        in_specs=[
            pl.BlockSpec((tm, d), lambda i, j: (i, 0)),
            pl.BlockSpec((1, d), lambda i, j: (0, 0)),
            pl.BlockSpec((1, d), lambda i, j: (0, 0)),
            pl.BlockSpec((d, tf), lambda i, j: (0, j)),
            pl.BlockSpec((tf, d), lambda i, j: (j, 0)),
        ],
        out_specs=pl.BlockSpec((tm, d), lambda i, j: (i, 0)),
        scratch_shapes=[pltpu.VMEM((tm, d), jnp.bfloat16), pltpu.VMEM((tm, d), jnp.float32)],
        compiler_params=_params("parallel", "arbitrary"),
        name="mlp",
    )(h, g_in, g_out, w1, w2)


def _ssm_in_kernel(h_ref, g_ref, w_ref, u_ref):
    xn = _rms(h_ref[...], g_ref[...]).astype(jnp.bfloat16)
    u_ref[...] = _dot(xn, w_ref[...])


def _ssm_in(h, g, w, bsz, seq):
    d = h.shape[1]
    e = w.shape[1]
    tm = min(TOKEN_TILE, seq)
    nt = seq // tm
    return pl.pallas_call(
        _ssm_in_kernel,
        out_shape=jax.ShapeDtypeStruct((seq, bsz * e), jnp.float32),
        grid=(bsz, nt),
        in_specs=[
            pl.BlockSpec((tm, d), lambda b, i: (b * nt + i, 0)),
            pl.BlockSpec((1, d), lambda b, i: (0, 0)),
            pl.BlockSpec((d, e), lambda b, i: (0, 0)),
        ],
        out_specs=pl.BlockSpec((tm, e), lambda b, i: (i, b)),
        compiler_params=_params("parallel", "parallel"),
        name="ssm_in",
    )(h, g, w)


def _ssm_scan_kernel(u_ref, ar_ref, ai_ref, bmat_ref, cmat_ref, d_ref, z_ref, x_ref, st_ref,
                     *, bsz, chunk, half, lanes):
    @pl.when(pl.program_id(0) == 0)
    def _():
        st_ref[...] = jnp.zeros_like(st_ref)

    e = u_ref.shape[1]
    n_blk = e // MXU_DIM
    sl = half // n_blk

    def state_lanes(j):
        return slice(j * sl, (j + 1) * sl), slice(half + j * sl, half + (j + 1) * sl)

    u = u_ref[...]
    ub = u.astype(jnp.bfloat16)
    for j in range(n_blk):
        ch = slice(j * MXU_DIM, (j + 1) * MXU_DIM)
        for lanes_j in state_lanes(j):
            x_ref[:, lanes_j] = _dot(ub[:, ch], bmat_ref[ch, lanes_j])

    for c in range(half // lanes):
        re = slice(c * lanes, (c + 1) * lanes)
        im = slice(half + c * lanes, half + (c + 1) * lanes)
        ar = ar_ref[:, re]
        ai = ai_ref[:, re]

        def step(t, carry, re=re, im=im, ar=ar, ai=ai):
            xr, xi = carry
            rows = pl.ds(pl.multiple_of(t * bsz, bsz), bsz)
            nr = ar * xr - ai * xi + x_ref[rows, re]
            ni = ar * xi + ai * xr + x_ref[rows, im]
            x_ref[rows, re] = nr
            x_ref[rows, im] = ni
            return nr, ni

        xr, xi = lax.fori_loop(0, chunk, step, (st_ref[:, re], st_ref[:, im]))
        st_ref[:, re] = xr
        st_ref[:, im] = xi

    for j in range(n_blk):
        ch = slice(j * MXU_DIM, (j + 1) * MXU_DIM)
        re_j, im_j = state_lanes(j)
        y = (_dot(x_ref[:, re_j].astype(jnp.bfloat16), cmat_ref[re_j, ch])
             + _dot(x_ref[:, im_j].astype(jnp.bfloat16), cmat_ref[im_j, ch])
             + d_ref[:, ch] * u[:, ch])
        z_ref[:, ch] = jax.nn.gelu(y)


def _ssm_scan(u2, ar, ai, bmat, cmat, dvec, bsz):
    rows, e = u2.shape
    seq = rows // bsz
    n2 = bmat.shape[1]
    half = n2 // 2
    chunk = min(SCAN_CHUNK, seq)
    lanes = min(SCAN_LANES, half)
    kern = functools.partial(_ssm_scan_kernel, bsz=bsz, chunk=chunk, half=half, lanes=lanes)
    return pl.pallas_call(
        kern,
        out_shape=jax.ShapeDtypeStruct((rows, e), jnp.float32),
        grid=(seq // chunk,),
        in_specs=[
            pl.BlockSpec((chunk * bsz, e), lambda i: (i, 0)),
            pl.BlockSpec((bsz, half), lambda i: (0, 0)),
            pl.BlockSpec((bsz, half), lambda i: (0, 0)),
            pl.BlockSpec((e, n2), lambda i: (0, 0)),
            pl.BlockSpec((n2, e), lambda i: (0, 0)),
            pl.BlockSpec((1, e), lambda i: (0, 0)),
        ],
        out_specs=pl.BlockSpec((chunk * bsz, e), lambda i: (i, 0)),
        scratch_shapes=[pltpu.VMEM((chunk * bsz, n2), jnp.float32),
                        pltpu.VMEM((bsz, n2), jnp.float32)],
        compiler_params=_params("arbitrary"),
        name="ssm_scan",
    )(u2, ar, ai, bmat, cmat, dvec)


def _ssm_out_kernel(z_ref, h_ref, g_ref, wg_ref, wo_ref, o_ref):
    z = z_ref[...]
    zz = z * jax.nn.sigmoid(_dot(z.astype(jnp.bfloat16), wg_ref[...]))
    mix = _dot(zz.astype(jnp.bfloat16), wo_ref[...])
    o_ref[...] = h_ref[...] + _rms(mix, g_ref[...])


def _ssm_out(z, h, g, wg, wo, bsz, seq):
    d = h.shape[1]
    e = wg.shape[0]
    tm = min(TOKEN_TILE, seq)
    nt = seq // tm
    return pl.pallas_call(
        _ssm_out_kernel,
        out_shape=jax.ShapeDtypeStruct(h.shape, jnp.float32),
        grid=(bsz, nt),
        in_specs=[
            pl.BlockSpec((tm, e), lambda b, i: (i, b)),
            pl.BlockSpec((tm, d), lambda b, i: (b * nt + i, 0)),
            pl.BlockSpec((1, d), lambda b, i: (0, 0)),
            pl.BlockSpec((e, e), lambda b, i: (0, 0)),
            pl.BlockSpec((e, d), lambda b, i: (0, 0)),
        ],
        out_specs=pl.BlockSpec((tm, d), lambda b, i: (b * nt + i, 0)),
        compiler_params=_params("parallel", "parallel"),
        name="ssm_out",
    )(z, h, g, wg, wo)


def _ssm_discretise(lam_re, lam_im, log_dt, b_re, b_im, c_re, c_im, d_skip, bsz):
    g, p = lam_re.shape
    c = b_re.shape[2]
    dt = jnp.exp(log_dt)[:, None]
    mag = jnp.exp(lam_re * dt)
    abar_re = mag * jnp.cos(lam_im * dt)
    abar_im = mag * jnp.sin(lam_im * dt)
    den = lam_re * lam_re + lam_im * lam_im
    nr = abar_re - 1.0
    ni = abar_im
    fr = (nr * lam_re + ni * lam_im) / den
    fi = (ni * lam_re - nr * lam_im) / den
    bbar_re = fr[..., None] * b_re - fi[..., None] * b_im
    bbar_im = fr[..., None] * b_im + fi[..., None] * b_re
    bf = jnp.bfloat16
    state_of_col = jnp.arange(g * p) % p
    spread_p = (jnp.arange(p)[:, None] == state_of_col[None, :]).astype(bf)
    same_group = (jnp.arange(g * c)[:, None] // c) == (jnp.arange(g * p)[None, :] // p)

    def bd(m):
        rows = m.transpose(0, 2, 1).reshape(g * c, p).astype(bf)
        return jnp.where(same_group, jnp.dot(rows, spread_p, preferred_element_type=jnp.float32), 0.0).astype(bf)

    def cd(m):
        cols = m.transpose(2, 0, 1).reshape(p, g * c).astype(bf)
        return jnp.where(same_group.T, jnp.dot(spread_p.T, cols, preferred_element_type=jnp.float32), 0.0).astype(bf)

    bmat = jnp.concatenate([bd(bbar_re), bd(bbar_im)], axis=1)
    cmat = jnp.concatenate([cd(c_re), cd(-c_im)], axis=0)
    ar = jnp.broadcast_to(abar_re.reshape(1, g * p), (bsz, g * p))
    ai = jnp.broadcast_to(abar_im.reshape(1, g * p), (bsz, g * p))
    return ar, ai, bmat, cmat, d_skip.reshape(1, g * c)


def _rope128(x, cos, sin_signed, lane):
    swapped = jnp.where((lane % HEAD_DIM) < HEAD_DIM // 2,
                        pltpu.roll(x, LANES - HEAD_DIM // 2, axis=1),
                        pltpu.roll(x, HEAD_DIM // 2, axis=1))
    return x * cos + swapped * sin_signed


def _att_in_kernel(h_ref, g_ref, w_ref, cos_ref, sin_ref, q_ref, k_ref, v_ref, qi_ref, ki_ref, wi_ref,
                   *, n_q, n_k, n_v, n_qi, n_ki, q_scale, wi_scale):
    xn = _rms(h_ref[...], g_ref[...]).astype(jnp.bfloat16)
    proj = _dot(xn, w_ref[...])
    cos = cos_ref[...]
    sin = sin_ref[...]
    lane = lax.broadcasted_iota(jnp.int32, cos.shape, 1)
    off = 0

    def roped(ref, width, off, scale=None):
        for c in range(width // LANES):
            x = _rope128(proj[:, off + c * LANES: off + (c + 1) * LANES], cos, sin, lane)
            if scale is not None:
                x = x * scale
            ref[:, c * LANES:(c + 1) * LANES] = x.astype(ref.dtype)
        return off + width

    off = roped(q_ref, n_q, off, q_scale)
    for c in range(n_k // LANES):
        x = _rope128(proj[:, off + c * LANES: off + (c + 1) * LANES], cos, sin, lane)
        k_ref[:, c * LANES:(c + 1) * LANES] = jnp.where(lane == HEAD_DIM, 1.0, x).astype(k_ref.dtype)
    off += n_k
    for c in range(n_v // LANES):
        x = proj[:, off + c * LANES: off + (c + 1) * LANES]
        v_ref[:, c * LANES:(c + 1) * LANES] = jnp.where(lane >= HEAD_DIM, 1.0, x).astype(v_ref.dtype)
    off += n_v
    off = roped(qi_ref, n_qi, off)
    off = roped(ki_ref, n_ki, off)
    wi_ref[...] = proj[:, off:off + LANES] * wi_scale


def _att_in(h, g, w, cos, sin, bsz, seq, widths, q_scale, wi_scale):
    d = h.shape[1]
    n_q, n_k, n_v, n_qi, n_ki = widths
    ncols = w.shape[1]
    tm = min(TOKEN_TILE, seq)
    nt = seq // tm
    t = bsz * seq
    row = lambda b, i: (b * nt + i, 0)
    kern = functools.partial(_att_in_kernel, n_q=n_q, n_k=n_k, n_v=n_v, n_qi=n_qi, n_ki=n_ki,
                             q_scale=q_scale, wi_scale=wi_scale)
    bf = jnp.bfloat16
    return pl.pallas_call(
        kern,
        out_shape=[jax.ShapeDtypeStruct((t, n_q), bf), jax.ShapeDtypeStruct((t, n_k), bf),
                   jax.ShapeDtypeStruct((t, n_v), bf), jax.ShapeDtypeStruct((t, n_qi), bf),
                   jax.ShapeDtypeStruct((t, n_ki), bf), jax.ShapeDtypeStruct((t, LANES), jnp.float32)],
        grid=(bsz, nt),
        in_specs=[
            pl.BlockSpec((tm, d), row),
            pl.BlockSpec((1, d), lambda b, i: (0, 0)),
            pl.BlockSpec((d, ncols), lambda b, i: (0, 0)),
            pl.BlockSpec((tm, LANES), lambda b, i: (i, 0)),
            pl.BlockSpec((tm, LANES), lambda b, i: (i, 0)),
        ],
        out_specs=[pl.BlockSpec((tm, n_q), row), pl.BlockSpec((tm, n_k), row), pl.BlockSpec((tm, n_v), row),
                   pl.BlockSpec((tm, n_qi), row), pl.BlockSpec((tm, n_ki), row), pl.BlockSpec((tm, LANES), row)],
        compiler_params=_params("parallel", "parallel"),
        name="att_in",
    )(h, g, w, cos, sin)


def _lane_fold(x, op):
    out = x[:, :LANES]
    for j in range(1, x.shape[1] // LANES):
        out = op(out, x[:, j * LANES:(j + 1) * LANES])
    return out


def _row_fold(x, op):
    out = x[:FOLD_ROWS]
    for j in range(1, x.shape[0] // FOLD_ROWS):
        out = op(out, x[j * FOLD_ROWS:(j + 1) * FOLD_ROWS])
    return out


def _dsa_kernel(q_ref, qi_ref, wi_ref, k_ref, v_ref, ki_ref, o_ref,
                sc_ref, sct_ref, qx_ref, acc_ref, m_ref, kn_ref, *, tq, tk, k_sel, n_heads, n_idx):
    i = pl.program_id(1)
    n_kv = n_heads // KV_GROUP
    lane = lax.broadcasted_iota(jnp.int32, (tq, LANES), 1)
    is_head = lane < HEAD_DIM

    @pl.when(i == 0)
    def _():
        lane_k = lax.broadcasted_iota(jnp.int32, (tk, LANES), 1)
        head_lane = lax.broadcasted_iota(jnp.int32, kn_ref.shape, 1)
        kn = jnp.zeros(kn_ref.shape, jnp.float32)
        for n in range(n_kv):
            def body(r, c, n=n):
                x = k_ref[pl.ds(pl.multiple_of(r * tk, tk), tk), n * LANES:(n + 1) * LANES].astype(jnp.float32)
                x = jnp.where(lane_k < HEAD_DIM, x, 0.0)
                ss = jnp.sum(x * x, axis=1, keepdims=True)
                return jnp.maximum(c, jnp.max(ss, axis=0, keepdims=True))
            kmax = lax.fori_loop(0, k_ref.shape[0] // tk, body, jnp.zeros((1, 1), jnp.float32))
            kn = jnp.where(head_lane // KV_GROUP == n, kmax, kn)
        kn_ref[...] = kn

    n_kt = (i * tq + tq + tk - 1) // tk
    neg_inf = jnp.float32(-jnp.inf)
    row = i * tq + lax.broadcasted_iota(jnp.int32, (tq, tk), 0)
    col0 = lax.broadcasted_iota(jnp.int32, (tq, tk), 1)
    wi = wi_ref[...]

    def score_tile(kt, carry):
        rmax, rmin = carry
        ks = pl.ds(pl.multiple_of(kt * tk, tk), tk)
        acc = jnp.zeros((tq, tk), jnp.float32)
        for hh in range(n_idx):
            qc = qi_ref[:, (hh // 2) * LANES:(hh // 2 + 1) * LANES]
            kc = ki_ref[ks, (hh % 2) * LANES:(hh % 2 + 1) * LANES]
            acc = acc + jnp.maximum(_dot_nt(qc, kc), 0.0) * wi[:, hh:hh + 1]
        causal = (col0 + kt * tk) <= row
        masked = jnp.where(causal, acc, neg_inf)
        sc_ref[kt] = masked
        masked_t = masked.T
        sct_ref[kt] = masked_t
        rmax = jnp.maximum(rmax, _row_fold(masked_t, jnp.maximum))
        rmin = jnp.minimum(rmin, _row_fold(jnp.where(masked_t == neg_inf, -neg_inf, masked_t), jnp.minimum))
        return rmax, rmin

    rmax, rmin = lax.fori_loop(0, n_kt, score_tile,
                               (jnp.full((FOLD_ROWS, tq), neg_inf), jnp.full((FOLD_ROWS, tq), -neg_inf)))

    kf = jnp.float32(k_sel)

    def sweep(fn, init, x):
        def body(kt, c):
            for j in range(tk // FOLD_ROWS):
                c = fn(c, sct_ref[kt, j * FOLD_ROWS:(j + 1) * FOLD_ROWS, :], x)
            return c
        return lax.fori_loop(0, n_kt, body, tuple(jnp.full((FOLD_ROWS, tq), v, jnp.float32) for v in init))

    def col_sum(part):
        return jnp.sum(part, axis=0, keepdims=True)

    def col_max(part):
        return jnp.max(part, axis=0, keepdims=True)

    def count_ge(x):
        (c,) = sweep(lambda c, s, x: (c[0] + jnp.where(s >= x, 1.0, 0.0),), (0.0,), x)
        return col_sum(c)

    def bisect(_, carry):
        lo, hi = carry
        mid = 0.5 * lo + 0.5 * hi
        ge = count_ge(mid) >= kf
        return jnp.where(ge, mid, lo), jnp.where(ge, hi, mid)

    rmin = -col_max(-rmin)
    rmax = col_max(rmax)
    lo, hi = lax.fori_loop(0, BISECT_ITERS, bisect, (rmin, rmax))

    few = (i * tq + lax.broadcasted_iota(jnp.int32, (1, tq), 1)) < k_sel

    def max_le(x):
        (c,) = sweep(lambda c, s, x: (jnp.maximum(c[0], jnp.where(s <= x, s, neg_inf)),), (-jnp.inf,), x)
        return col_max(c)

    def probe(x):
        cnt, nxt = sweep(lambda c, s, x: (c[0] + jnp.where(s >= x, 1.0, 0.0),
                                          jnp.maximum(c[1], jnp.where(s < x, s, neg_inf))),
                         (0.0, -jnp.inf), x)
        return col_sum(cnt), col_max(nxt)

    cand0 = jnp.where(few, rmin, max_le(hi))
    cnt0, nxt0 = probe(cand0)

    def unresolved(cnt):
        return jnp.logical_and(jnp.logical_not(few), cnt < kf)

    def finish_cond(c):
        _, cnt, _ = c
        return jnp.max(jnp.where(unresolved(cnt), 1.0, 0.0)) > 0.0

    def finish_body(c):
        cand, cnt, nxt = c
        cand = jnp.where(unresolved(cnt), nxt, cand)
        cnt, nxt = probe(cand)
        return cand, cnt, nxt

    thr_t, cnt_ge, _ = lax.while_loop(finish_cond, finish_body, (cand0, cnt0, nxt0))

    def to_rows(x):
        return jnp.broadcast_to(x, (LANES, tq)).T

    thr = to_rows(thr_t)

    tied_t = jnp.logical_and(jnp.logical_not(few), cnt_ge > kf)

    @pl.when(jnp.max(jnp.where(tied_t, 1.0, 0.0)) > 0.0)
    def _():
        (c,) = sweep(lambda c, s, x: (c[0] + jnp.where(s > x, 1.0, 0.0),), (0.0,), thr_t)
        need = to_rows(kf - col_sum(c))[:, :1]
        thr1 = thr[:, :1]
        tied1 = to_rows(jnp.where(tied_t, 1.0, 0.0))[:, :1] > 0.0
        tri = (lax.broadcasted_iota(jnp.int32, (tk, tk), 0)
               <= lax.broadcasted_iota(jnp.int32, (tk, tk), 1)).astype(jnp.bfloat16)

        def drop_body(kt, run):
            s = sc_ref[kt]
            eq = jnp.logical_and(s == thr1, tied1)
            eqf = jnp.where(eq, 1.0, 0.0)
            rank = run + _dot(eqf.astype(jnp.bfloat16), tri)
            sc_ref[kt] = jnp.where(jnp.logical_and(eq, rank > need), neg_inf, s)
            return run + jnp.sum(eqf, axis=1, keepdims=True)

        lax.fori_loop(0, n_kt, drop_body, jnp.zeros((tq, 1), jnp.float32))

    qf = q_ref[...].astype(jnp.float32)
    n_q = qf.shape[1]
    head_of_col = lax.broadcasted_iota(jnp.int32, (n_q, LANES), 0) // HEAD_DIM
    head_sel = jnp.where(head_of_col == lax.broadcasted_iota(jnp.int32, (n_q, LANES), 1), 1.0, 0.0)
    qss = _dot((qf * qf).astype(jnp.bfloat16), head_sel.astype(jnp.bfloat16))
    bound = jnp.sqrt(qss * kn_ref[0:1, :]) * BOUND_SLACK
    for c in range(n_heads // 2):
        qc = qf[:, c * LANES:(c + 1) * LANES]
        for half in range(2):
            h = 2 * c + half
            n, g = h // KV_GROUP, h % KV_GROUP
            x = qc if half == 0 else pltpu.roll(qc, HEAD_DIM, axis=1)
            b = pltpu.roll(bound, (HEAD_DIM - h) % LANES, axis=1)
            qx = jnp.where(is_head, x, jnp.where(lane == HEAD_DIM, -b, 0.0))
            qx_ref[n, g * tq:(g + 1) * tq, :] = qx.astype(qx_ref.dtype)

    thr_tile = jnp.concatenate([thr] * (tk // LANES), axis=1)

    def attend(online):
        acc_ref[...] = jnp.zeros_like(acc_ref)
        if online:
            m_ref[...] = jnp.full_like(m_ref, neg_inf)

        def att_tile(kt, carry):
            ks = pl.ds(pl.multiple_of(kt * tk, tk), tk)
            keep = sc_ref[kt] >= thr_tile
            keep_b = jnp.where(keep, 1.0, 0.0).astype(jnp.bfloat16)
            for n in range(n_kv):
                s = _dot_nt(qx_ref[n], k_ref[ks, n * LANES:(n + 1) * LANES])
                vc = v_ref[ks, n * LANES:(n + 1) * LANES]
                if online:
                    s = jnp.where(jnp.concatenate([keep] * KV_GROUP, axis=0), s, neg_inf)
                    m_old = m_ref[n]
                    m_new = jnp.maximum(m_old, jnp.max(s, axis=1, keepdims=True))
                    m_safe = jnp.where(m_new == neg_inf, 0.0, m_new)
                    p = jnp.exp2(s - m_safe).astype(jnp.bfloat16)
                    acc_ref[n] = jnp.exp2(m_old - m_safe) * acc_ref[n] + _dot(p, vc)
                    m_ref[n] = m_new
                else:
                    p = jnp.exp2(s).astype(jnp.bfloat16).reshape(KV_GROUP, tq, tk) * keep_b[None]
                    acc_ref[n] += _dot(p.reshape(KV_GROUP * tq, tk), vc)
            return carry

        lax.fori_loop(0, n_kt, att_tile, 0)

    attend(False)
    den = acc_ref[0]
    for n in range(1, n_kv):
        den = jnp.minimum(den, acc_ref[n])
    den_lane = lax.broadcasted_iota(jnp.int32, den.shape, 1) >= HEAD_DIM
    safe = jnp.min(jnp.where(den_lane, den, 1.0)) > MIN_DENOMINATOR

    @pl.when(jnp.logical_not(safe))
    def _():
        attend(True)

    for c in range(n_heads // 2):
        parts = []
        for half in range(2):
            h = 2 * c + half
            n, g = h // KV_GROUP, h % KV_GROUP
            a = acc_ref[n, g * tq:(g + 1) * tq, :]
            parts.append(a / pltpu.roll(a, HEAD_DIM, axis=1))
        out = jnp.where(is_head, parts[0], pltpu.roll(parts[1], HEAD_DIM, axis=1))
        o_ref[:, c * LANES:(c + 1) * LANES] = out.astype(o_ref.dtype)


def _dsa(q, qi, wi, kx, vx, kix, bsz, seq, k_sel):
    t, n_q = q.shape
    n_heads = n_q // HEAD_DIM
    n_kv = n_heads // KV_GROUP
    n_idx = qi.shape[1] // IDX_DIM
    tq = min(Q_TILE, seq)
    tk = min(K_TILE, seq)
    nq = seq // tq
    row = lambda b, i: (b * nq + i, 0)
    per_batch = lambda b, i: (b, 0)
    kern = functools.partial(_dsa_kernel, tq=tq, tk=tk, k_sel=k_sel, n_heads=n_heads, n_idx=n_idx)
    return pl.pallas_call(
        kern,
        out_shape=jax.ShapeDtypeStruct((t, n_q), jnp.bfloat16),
        grid=(bsz, nq),
        in_specs=[
            pl.BlockSpec((tq, n_q), row),
            pl.BlockSpec((tq, qi.shape[1]), row),
            pl.BlockSpec((tq, LANES), row),
            pl.BlockSpec((seq, kx.shape[1]), per_batch),
            pl.BlockSpec((seq, vx.shape[1]), per_batch),
            pl.BlockSpec((seq, kix.shape[1]), per_batch),
        ],
        out_specs=pl.BlockSpec((tq, n_q), row),
        scratch_shapes=[
            pltpu.VMEM((seq // tk, tq, tk), jnp.float32),
            pltpu.VMEM((seq // tk, tk, tq), jnp.float32),
            pltpu.VMEM((n_kv, KV_GROUP * tq, LANES), jnp.bfloat16),
            pltpu.VMEM((n_kv, KV_GROUP * tq, LANES), jnp.float32),
            pltpu.VMEM((n_kv, KV_GROUP * tq, 1), jnp.float32),
            pltpu.VMEM((SUBLANES, LANES), jnp.float32),
        ],
        compiler_params=_params("parallel", "arbitrary"),
        name="dsa",
    )(q, qi, wi, kx, vx, kix)


def _att_out_kernel(a_ref, h_ref, g_ref, w_ref, o_ref):
    o_ref[...] = h_ref[...] + _rms(_dot(a_ref[...], w_ref[...]), g_ref[...])


def _att_out(a, h, g, w):
    t, d = h.shape
    n = a.shape[1]
    tm = min(TOKEN_TILE, t)
    return pl.pallas_call(
        _att_out_kernel,
        out_shape=jax.ShapeDtypeStruct(h.shape, jnp.float32),
        grid=(t // tm,),
        in_specs=[
            pl.BlockSpec((tm, n), lambda i: (i, 0)),
            pl.BlockSpec((tm, d), lambda i: (i, 0)),
            pl.BlockSpec((1, d), lambda i: (0, 0)),
            pl.BlockSpec((n, d), lambda i: (0, 0)),
        ],
        out_specs=pl.BlockSpec((tm, d), lambda i: (i, 0)),
        compiler_params=_params("parallel"),
        name="att_out",
    )(a, h, g, w)


def _rope_tables(seq):
    half = HEAD_DIM // 2
    inv_freq = ROPE_THETA ** (-jnp.arange(half, dtype=jnp.float32) * 2.0 / HEAD_DIM)
    ang = jnp.arange(seq, dtype=jnp.float32)[:, None] * inv_freq[None, :]
    cos = jnp.concatenate([jnp.cos(ang)] * 4, axis=-1)
    sin = jnp.concatenate([-jnp.sin(ang), jnp.sin(ang)] * 2, axis=-1)
    return cos, sin


def _half_chunks(w, low):
    d, n = w.shape
    w3 = w.reshape(d, n // HEAD_DIM, HEAD_DIM)
    z = jnp.zeros_like(w3)
    return jnp.concatenate([w3, z] if low else [z, w3], axis=-1).reshape(d, 2 * n)


def _att_weights(w_in, d_model):
    n_heads = d_model // HEAD_DIM
    n_kv = n_heads // KV_GROUP
    n_idx = max(4, d_model // 128)
    splits = (n_heads * HEAD_DIM, n_kv * HEAD_DIM, n_kv * HEAD_DIM, n_idx * IDX_DIM, IDX_DIM, n_idx)
    offs = [0]
    for s in splits:
        offs.append(offs[-1] + s)
    wq, wk, wv, wqi, wki, wwi = (w_in[:, offs[j]:offs[j + 1]] for j in range(6))
    d = w_in.shape[0]

    def both_halves(w):
        lo = _half_chunks(w, True).reshape(d, -1, LANES)
        hi = _half_chunks(w, False).reshape(d, -1, LANES)
        return jnp.stack([lo, hi], axis=2).reshape(d, -1)

    wkx, wvx, wkix = _half_chunks(wk, True), _half_chunks(wv, True), both_halves(wki)
    wwi_p = jnp.pad(wwi, ((0, 0), (0, LANES - n_idx)))
    w_cat = jnp.concatenate([wq, wkx, wvx, wqi, wkix, wwi_p], axis=1).astype(jnp.bfloat16)
    widths = (wq.shape[1], wkx.shape[1], wvx.shape[1], wqi.shape[1], wkix.shape[1])
    return w_cat, widths, n_idx


def kernel(x, norm_g, mlp_w1, mlp_w2, ssm_w_in, ssm_lam_re, ssm_lam_im, ssm_log_dt, ssm_b_re, ssm_b_im,
           ssm_c_re, ssm_c_im, ssm_d, ssm_w_glu, ssm_w_out, att_w_in, att_w_out):
    bsz, seq, d_model = x.shape
    depth = norm_g.shape[0]
    bf = jnp.bfloat16
    h = x.reshape(bsz * seq, d_model)
    cos, sin = _rope_tables(seq)
    k_sel = min(TOPK_MAX, seq // 4)
    for i in range(depth):
        j = i // 2
        g = norm_g[i][:, None, :]
        if i % 2 == 0:
            ar, ai, bmat, cmat, dvec = _ssm_discretise(
                ssm_lam_re[j], ssm_lam_im[j], ssm_log_dt[j], ssm_b_re[j], ssm_b_im[j],
                ssm_c_re[j], ssm_c_im[j], ssm_d[j], bsz)
            e = ssm_w_in.shape[2]
            u = _ssm_in(h, g[0], ssm_w_in[j].astype(bf), bsz, seq)
            z = _ssm_scan(u.reshape(seq * bsz, e), ar, ai, bmat, cmat, dvec, bsz)
            h = _ssm_out(z.reshape(seq, bsz * e), h, g[1], ssm_w_glu[j].astype(bf),
                         ssm_w_out[j].astype(bf), bsz, seq)
        else:
            w_cat, widths, n_idx = _att_weights(att_w_in[j], d_model)
            q, kx, vx, qi, kix, wi = _att_in(h, g[0], w_cat, cos, sin, bsz, seq, widths,
                                             HEAD_DIM ** -0.5 * LOG2_E, n_idx ** -0.5 * IDX_DIM ** -0.5)
            a = _dsa(q, qi, wi, kx, vx, kix, bsz, seq, k_sel)
            h = _att_out(a, h, g[1], att_w_out[j].astype(bf))
        h = _mlp(h, g[2], g[3], mlp_w1[i].astype(bf), mlp_w2[i].astype(bf))
    return h.reshape(bsz, seq, d_model)
```

```python
import functools
import math

import jax
import jax.numpy as jnp
from jax import lax
from jax.experimental import pallas as pl
from jax.experimental.pallas import tpu as pltpu

NORM_EPS = 1e-6
SSM_CH = 16
SSM_STATE = 64
DT_MIN, DT_MAX = 1e-3, 1e-1
HEAD_DIM = 64
KV_GROUP = 4
IDX_DIM = 64
TOPK_MAX = 256
ROPE_THETA = 10000.0

LANES = 128
SUBLANES = 8
MXU_DIM = 256
VMEM_LIMIT_BYTES = 56 * 1024 * 1024

TOKEN_TILE = 512
MLP_TOKEN_TILE = 1024
MLP_SUB_TILE = 512
FF_TILE = 1024
SCAN_CHUNK = 64
SCAN_LANES = 512
Q_TILE = 256
K_TILE = 512
BISECT_ITERS = 16
FOLD_ROWS = 64
LOG2_E = math.log2(math.e)
BOUND_SLACK = 1.01
MIN_DENOMINATOR = 2.0 ** -60


def _params(*sem):
    return pltpu.CompilerParams(dimension_semantics=sem, vmem_limit_bytes=VMEM_LIMIT_BYTES)


def _rms(x, g):
    return x * lax.rsqrt(jnp.mean(x * x, axis=-1, keepdims=True) + NORM_EPS) * g


def _dot(a, b):
    return jnp.dot(a, b, preferred_element_type=jnp.float32)


def _dot_nt(a, b):
    return lax.dot_general(a, b, (((1,), (1,)), ((), ())), preferred_element_type=jnp.float32)


def _mlp_kernel(h_ref, g_in_ref, g_out_ref, w1_ref, w2_ref, o_ref, *, sub, tf):
    ff = w1_ref.shape[1]
    for r in range(h_ref.shape[0] // sub):
        rows = slice(r * sub, (r + 1) * sub)
        h = h_ref[rows, :]
        xn = _rms(h, g_in_ref[...]).astype(jnp.bfloat16)
        acc = None
        for c in range(ff // tf):
            cols = slice(c * tf, (c + 1) * tf)
            a = jnp.maximum(_dot(xn, w1_ref[:, cols]), 0.0)
            part = _dot((a * a).astype(jnp.bfloat16), w2_ref[cols, :])
            acc = part if acc is None else acc + part
        o_ref[rows, :] = h + _rms(acc, g_out_ref[...])


def _mlp(h, g_in, g_out, w1, w2):
    t, d = h.shape
    ff = w1.shape[1]
    tm = min(MLP_TOKEN_TILE, t)
    kern = functools.partial(_mlp_kernel, sub=min(MLP_SUB_TILE, tm), tf=min(FF_TILE, ff))
    resident = dict(pipeline_mode=pl.Buffered(1))
    return pl.pallas_call(
        kern,
        out_shape=jax.ShapeDtypeStruct((t, d), jnp.float32),
        grid=(t // tm,),
        in_specs=[
            pl.BlockSpec((tm, d), lambda i: (i, 0)),
            pl.BlockSpec((1, d), lambda i: (0, 0)),
            pl.BlockSpec((1, d), lambda i: (0, 0)),
            pl.BlockSpec((d, ff), lambda i: (0, 0), **resident),
            pl.BlockSpec((ff, d), lambda i: (0, 0), **resident),
        ],
        out_specs=pl.BlockSpec((tm, d), lambda i: (i, 0)),
        compiler_params=_params("parallel"),
        name="mlp",
    )(h, g_in, g_out, w1, w2)


def _ssm_in_kernel(h_ref, g_ref, w_ref, u_ref):
    xn = _rms(h_ref[...], g_ref[...]).astype(jnp.bfloat16)
    u_ref[...] = _dot(xn, w_ref[...])


def _ssm_in(h, g, w, bsz, seq):
    d = h.shape[1]
    e = w.shape[1]
    tm = min(TOKEN_TILE, seq)
    nt = seq // tm
    return pl.pallas_call(
        _ssm_in_kernel,
        out_shape=jax.ShapeDtypeStruct((seq, bsz * e), jnp.float32),
        grid=(bsz, nt),
        in_specs=[
            pl.BlockSpec((tm, d), lambda b, i: (b * nt + i, 0)),
            pl.BlockSpec((1, d), lambda b, i: (0, 0)),
            pl.BlockSpec((d, e), lambda b, i: (0, 0)),
        ],
        out_specs=pl.BlockSpec((tm, e), lambda b, i: (i, b)),
        compiler_params=_params("parallel", "parallel"),
        name="ssm_in",
    )(h, g, w)


def _ssm_scan_kernel(u_ref, ar_ref, ai_ref, bmat_ref, cmat_ref, d_ref, z_ref, x_ref, st_ref, io_ref,
                     *, bsz, chunk, half, lanes):
    @pl.when(pl.program_id(0) == 0)
    def _():
        st_ref[...] = jnp.zeros_like(st_ref)

    e = d_ref.shape[1]
    n_blk = e // MXU_DIM
    sl = half // n_blk

    def state_lanes(j):
        return slice(j * sl, (j + 1) * sl), slice(half + j * sl, half + (j + 1) * sl)

    n_slab = e // LANES
    for b in range(bsz):
        for c in range(n_slab):
            io_ref[c, pl.ds(b, chunk, stride=bsz), :] = u_ref[:, b * e + c * LANES:b * e + (c + 1) * LANES]
    u = jnp.concatenate([io_ref[c] for c in range(n_slab)], axis=1)
    ub = u.astype(jnp.bfloat16)
    for j in range(n_blk):
        ch = slice(j * MXU_DIM, (j + 1) * MXU_DIM)
        for lanes_j in state_lanes(j):
            x_ref[:, lanes_j] = _dot(ub[:, ch], bmat_ref[ch, lanes_j])

    for c in range(half // lanes):
        re = slice(c * lanes, (c + 1) * lanes)
        im = slice(half + c * lanes, half + (c + 1) * lanes)
        ar = ar_ref[:, re]
        ai = ai_ref[:, re]

        def step(t, carry, re=re, im=im, ar=ar, ai=ai):
            xr, xi = carry
            rows = pl.ds(pl.multiple_of(t * bsz, bsz), bsz)
            nr = ar * xr - ai * xi + x_ref[rows, re]
            ni = ar * xi + ai * xr + x_ref[rows, im]
            x_ref[rows, re] = nr
            x_ref[rows, im] = ni
            return nr, ni

        xr, xi = lax.fori_loop(0, chunk, step, (st_ref[:, re], st_ref[:, im]))
        st_ref[:, re] = xr
        st_ref[:, im] = xi

    for j in range(n_blk):
        ch = slice(j * MXU_DIM, (j + 1) * MXU_DIM)
        re_j, im_j = state_lanes(j)
        y = (_dot(x_ref[:, re_j].astype(jnp.bfloat16), cmat_ref[re_j, ch])
             + _dot(x_ref[:, im_j].astype(jnp.bfloat16), cmat_ref[im_j, ch])
             + d_ref[:, ch] * u[:, ch])
        z = jax.nn.gelu(y)
        for c in range(MXU_DIM // LANES):
            io_ref[j * (MXU_DIM // LANES) + c] = z[:, c * LANES:(c + 1) * LANES]
    for b in range(bsz):
        for c in range(n_slab):
            z_ref[:, b * e + c * LANES:b * e + (c + 1) * LANES] = io_ref[c, pl.ds(b, chunk, stride=bsz), :]


def _ssm_scan(u, ar, ai, bmat, cmat, dvec, bsz):
    seq = u.shape[0]
    e = u.shape[1] // bsz
    n2 = bmat.shape[1]
    half = n2 // 2
    chunk = min(SCAN_CHUNK, seq)
    lanes = min(SCAN_LANES, half)
    kern = functools.partial(_ssm_scan_kernel, bsz=bsz, chunk=chunk, half=half, lanes=lanes)
    return pl.pallas_call(
        kern,
        out_shape=jax.ShapeDtypeStruct(u.shape, jnp.float32),
        grid=(seq // chunk,),
        in_specs=[
            pl.BlockSpec((chunk, bsz * e), lambda i: (i, 0)),
            pl.BlockSpec((bsz, half), lambda i: (0, 0)),
            pl.BlockSpec((bsz, half), lambda i: (0, 0)),
            pl.BlockSpec((e, n2), lambda i: (0, 0)),
            pl.BlockSpec((n2, e), lambda i: (0, 0)),
            pl.BlockSpec((1, e), lambda i: (0, 0)),
        ],
        out_specs=pl.BlockSpec((chunk, bsz * e), lambda i: (i, 0)),
        scratch_shapes=[pltpu.VMEM((chunk * bsz, n2), jnp.float32),
                        pltpu.VMEM((bsz, n2), jnp.float32),
                        pltpu.VMEM((e // LANES, chunk * bsz, LANES), jnp.float32)],
        compiler_params=_params("arbitrary"),
        name="ssm_scan",
    )(u, ar, ai, bmat, cmat, dvec)


def _ssm_out_kernel(z_ref, h_ref, g_ref, wg_ref, wo_ref, o_ref):
    z = z_ref[...]
    zz = z * jax.nn.sigmoid(_dot(z.astype(jnp.bfloat16), wg_ref[...]))
    mix = _dot(zz.astype(jnp.bfloat16), wo_ref[...])
    o_ref[...] = h_ref[...] + _rms(mix, g_ref[...])


def _ssm_out(z, h, g, wg, wo, bsz, seq):
    d = h.shape[1]
    e = wg.shape[0]
    tm = min(TOKEN_TILE, seq)
    nt = seq // tm
    return pl.pallas_call(
        _ssm_out_kernel,
        out_shape=jax.ShapeDtypeStruct(h.shape, jnp.float32),
        grid=(bsz, nt),
        in_specs=[
            pl.BlockSpec((tm, e), lambda b, i: (i, b)),
            pl.BlockSpec((tm, d), lambda b, i: (b * nt + i, 0)),
            pl.BlockSpec((1, d), lambda b, i: (0, 0)),
            pl.BlockSpec((e, e), lambda b, i: (0, 0)),
            pl.BlockSpec((e, d), lambda b, i: (0, 0)),
        ],
        out_specs=pl.BlockSpec((tm, d), lambda b, i: (b * nt + i, 0)),
        compiler_params=_params("parallel", "parallel"),
        name="ssm_out",
    )(z, h, g, wg, wo)


def _ssm_discretise(lam_re, lam_im, log_dt, b_re, b_im, c_re, c_im, d_skip, bsz):
    g, p = lam_re.shape
    c = b_re.shape[2]
    dt = jnp.exp(log_dt)[:, None]
    mag = jnp.exp(lam_re * dt)
    abar_re = mag * jnp.cos(lam_im * dt)
    abar_im = mag * jnp.sin(lam_im * dt)
    den = lam_re * lam_re + lam_im * lam_im
    nr = abar_re - 1.0
    ni = abar_im
    fr = (nr * lam_re + ni * lam_im) / den
    fi = (ni * lam_re - nr * lam_im) / den
    bbar_re = fr[..., None] * b_re - fi[..., None] * b_im
    bbar_im = fr[..., None] * b_im + fi[..., None] * b_re
    bf = jnp.bfloat16
    state_of_col = jnp.arange(g * p) % p
    spread_p = (jnp.arange(p)[:, None] == state_of_col[None, :]).astype(bf)
    same_group = (jnp.arange(g * c)[:, None] // c) == (jnp.arange(g * p)[None, :] // p)

    def bd(m):
        rows = m.transpose(0, 2, 1).reshape(g * c, p).astype(bf)
        return jnp.where(same_group, jnp.dot(rows, spread_p, preferred_element_type=jnp.float32), 0.0).astype(bf)

    def cd(m):
        cols = m.transpose(2, 0, 1).reshape(p, g * c).astype(bf)
        return jnp.where(same_group.T, jnp.dot(spread_p.T, cols, preferred_element_type=jnp.float32), 0.0).astype(bf)

    bmat = jnp.concatenate([bd(bbar_re), bd(bbar_im)], axis=1)
    cmat = jnp.concatenate([cd(c_re), cd(-c_im)], axis=0)
    ar = jnp.broadcast_to(abar_re.reshape(1, g * p), (bsz, g * p))
    ai = jnp.broadcast_to(abar_im.reshape(1, g * p), (bsz, g * p))
    return ar, ai, bmat, cmat, d_skip.reshape(1, g * c)


def _rope128(x, cos, sin_signed, lane):
    swapped = jnp.where((lane % HEAD_DIM) < HEAD_DIM // 2,
                        pltpu.roll(x, LANES - HEAD_DIM // 2, axis=1),
                        pltpu.roll(x, HEAD_DIM // 2, axis=1))
    return x * cos + swapped * sin_signed


def _att_in_kernel(h_ref, g_ref, w_ref, cos_ref, sin_ref, q_ref, k_ref, v_ref, qi_ref, ki_ref, wi_ref,
                   *, n_q, n_k, n_v, n_qi, n_ki, q_scale, wi_scale):
    xn = _rms(h_ref[...], g_ref[...]).astype(jnp.bfloat16)
    proj = _dot(xn, w_ref[...])
    cos = cos_ref[...]
    sin = sin_ref[...]
    lane = lax.broadcasted_iota(jnp.int32, cos.shape, 1)
    off = 0

    def roped(ref, width, off, scale=None):
        for c in range(width // LANES):
            x = _rope128(proj[:, off + c * LANES: off + (c + 1) * LANES], cos, sin, lane)
            if scale is not None:
                x = x * scale
            ref[:, c * LANES:(c + 1) * LANES] = x.astype(ref.dtype)
        return off + width

    off = roped(q_ref, n_q, off, q_scale)
    for c in range(n_k // LANES):
        x = _rope128(proj[:, off + c * LANES: off + (c + 1) * LANES], cos, sin, lane)
        k_ref[:, c * LANES:(c + 1) * LANES] = jnp.where(lane == HEAD_DIM, 1.0, x).astype(k_ref.dtype)
    off += n_k
    for c in range(n_v // LANES):
        x = proj[:, off + c * LANES: off + (c + 1) * LANES]
        v_ref[:, c * LANES:(c + 1) * LANES] = jnp.where(lane >= HEAD_DIM, 1.0, x).astype(v_ref.dtype)
    off += n_v
    off = roped(qi_ref, n_qi, off)
    off = roped(ki_ref, n_ki, off)
    wi_ref[...] = proj[:, off:off + LANES] * wi_scale


def _att_in(h, g, w, cos, sin, bsz, seq, widths, q_scale, wi_scale):
    d = h.shape[1]
    n_q, n_k, n_v, n_qi, n_ki = widths
    ncols = w.shape[1]
    tm = min(TOKEN_TILE, seq)
    nt = seq // tm
    t = bsz * seq
    row = lambda b, i: (b * nt + i, 0)
    kern = functools.partial(_att_in_kernel, n_q=n_q, n_k=n_k, n_v=n_v, n_qi=n_qi, n_ki=n_ki,
                             q_scale=q_scale, wi_scale=wi_scale)
    bf = jnp.bfloat16
    return pl.pallas_call(
        kern,
        out_shape=[jax.ShapeDtypeStruct((t, n_q), bf), jax.ShapeDtypeStruct((t, n_k), bf),
                   jax.ShapeDtypeStruct((t, n_v), bf), jax.ShapeDtypeStruct((t, n_qi), bf),
                   jax.ShapeDtypeStruct((t, n_ki), bf), jax.ShapeDtypeStruct((t, LANES), jnp.float32)],
        grid=(bsz, nt),
        in_specs=[
            pl.BlockSpec((tm, d), row),
            pl.BlockSpec((1, d), lambda b, i: (0, 0)),
            pl.BlockSpec((d, ncols), lambda b, i: (0, 0)),
            pl.BlockSpec((tm, LANES), lambda b, i: (i, 0)),
            pl.BlockSpec((tm, LANES), lambda b, i: (i, 0)),
        ],
        out_specs=[pl.BlockSpec((tm, n_q), row), pl.BlockSpec((tm, n_k), row), pl.BlockSpec((tm, n_v), row),
                   pl.BlockSpec((tm, n_qi), row), pl.BlockSpec((tm, n_ki), row), pl.BlockSpec((tm, LANES), row)],
        compiler_params=_params("parallel", "parallel"),
        name="att_in",
    )(h, g, w, cos, sin)


def _lane_fold(x, op):
    out = x[:, :LANES]
    for j in range(1, x.shape[1] // LANES):
        out = op(out, x[:, j * LANES:(j + 1) * LANES])
    return out


def _row_fold(x, op):
    out = x[:FOLD_ROWS]
    for j in range(1, x.shape[0] // FOLD_ROWS):
        out = op(out, x[j * FOLD_ROWS:(j + 1) * FOLD_ROWS])
    return out


def _dsa_kernel(q_ref, qi_ref, wi_ref, k_ref, v_ref, ki_ref, o_ref,
                sc_ref, sct_ref, qx_ref, acc_ref, m_ref, kn_ref, *, tq, tk, k_sel, n_heads, n_idx):
    i = pl.program_id(1)
    n_kv = n_heads // KV_GROUP
    lane = lax.broadcasted_iota(jnp.int32, (tq, LANES), 1)
    is_head = lane < HEAD_DIM

    @pl.when(i == 0)
    def _():
        lane_k = lax.broadcasted_iota(jnp.int32, (tk, LANES), 1)
        head_lane = lax.broadcasted_iota(jnp.int32, kn_ref.shape, 1)
        kn = jnp.zeros(kn_ref.shape, jnp.float32)
        for n in range(n_kv):
            def body(r, c, n=n):
                x = k_ref[pl.ds(pl.multiple_of(r * tk, tk), tk), n * LANES:(n + 1) * LANES].astype(jnp.float32)
                x = jnp.where(lane_k < HEAD_DIM, x, 0.0)
                ss = jnp.sum(x * x, axis=1, keepdims=True)
                return jnp.maximum(c, jnp.max(ss, axis=0, keepdims=True))
            kmax = lax.fori_loop(0, k_ref.shape[0] // tk, body, jnp.zeros((1, 1), jnp.float32))
            kn = jnp.where(head_lane // KV_GROUP == n, kmax, kn)
        kn_ref[...] = kn

    n_kt = (i * tq + tq + tk - 1) // tk
    neg_inf = jnp.float32(-jnp.inf)
    row = i * tq + lax.broadcasted_iota(jnp.int32, (tq, tk), 0)
    col0 = lax.broadcasted_iota(jnp.int32, (tq, tk), 1)
    wi = wi_ref[...]

    def score_tile(kt, carry):
        rmax, rmin = carry
        ks = pl.ds(pl.multiple_of(kt * tk, tk), tk)
        acc = jnp.zeros((tq, tk), jnp.float32)
        for hh in range(n_idx):
            qc = qi_ref[:, (hh // 2) * LANES:(hh // 2 + 1) * LANES]
            kc = ki_ref[ks, (hh % 2) * LANES:(hh % 2 + 1) * LANES]
            acc = acc + jnp.maximum(_dot_nt(qc, kc), 0.0) * wi[:, hh:hh + 1]
        causal = (col0 + kt * tk) <= row
        masked = jnp.where(causal, acc, neg_inf)
        sc_ref[kt] = masked
        masked_t = masked.T
        sct_ref[kt] = masked_t
        rmax = jnp.maximum(rmax, _row_fold(masked_t, jnp.maximum))
        rmin = jnp.minimum(rmin, _row_fold(jnp.where(masked_t == neg_inf, -neg_inf, masked_t), jnp.minimum))
        return rmax, rmin

    rmax, rmin = lax.fori_loop(0, n_kt, score_tile,
                               (jnp.full((FOLD_ROWS, tq), neg_inf), jnp.full((FOLD_ROWS, tq), -neg_inf)))

    kf = jnp.float32(k_sel)

    def sweep(fn, init, x):
        def body(kt, c):
            for j in range(tk // FOLD_ROWS):
                c = fn(c, sct_ref[kt, j * FOLD_ROWS:(j + 1) * FOLD_ROWS, :], x)
            return c
        return lax.fori_loop(0, n_kt, body, tuple(jnp.full((FOLD_ROWS, tq), v, jnp.float32) for v in init))

    def col_sum(part):
        return jnp.sum(part, axis=0, keepdims=True)

    def col_max(part):
        return jnp.max(part, axis=0, keepdims=True)

    def count_ge(x):
        (c,) = sweep(lambda c, s, x: (c[0] + jnp.where(s >= x, 1.0, 0.0),), (0.0,), x)
        return col_sum(c)

    def bisect(_, carry):
        lo, hi = carry
        mid = 0.5 * lo + 0.5 * hi
        ge = count_ge(mid) >= kf
        return jnp.where(ge, mid, lo), jnp.where(ge, hi, mid)

    rmin = -col_max(-rmin)
    rmax = col_max(rmax)
    lo, hi = lax.fori_loop(0, BISECT_ITERS, bisect, (rmin, rmax))

    few = (i * tq + lax.broadcasted_iota(jnp.int32, (1, tq), 1)) < k_sel

    def max_le(x):
        (c,) = sweep(lambda c, s, x: (jnp.maximum(c[0], jnp.where(s <= x, s, neg_inf)),), (-jnp.inf,), x)
        return col_max(c)

    def probe(x):
        cnt, nxt = sweep(lambda c, s, x: (c[0] + jnp.where(s >= x, 1.0, 0.0),
                                          jnp.maximum(c[1], jnp.where(s < x, s, neg_inf))),
                         (0.0, -jnp.inf), x)
        return col_sum(cnt), col_max(nxt)

    cand0 = jnp.where(few, rmin, max_le(hi))
    cnt0, nxt0 = probe(cand0)

    def unresolved(cnt):
        return jnp.logical_and(jnp.logical_not(few), cnt < kf)

    def finish_cond(c):
        _, cnt, _ = c
        return jnp.max(jnp.where(unresolved(cnt), 1.0, 0.0)) > 0.0

    def finish_body(c):
        cand, cnt, nxt = c
        cand = jnp.where(unresolved(cnt), nxt, cand)
        cnt, nxt = probe(cand)
        return cand, cnt, nxt

    thr_t, cnt_ge, _ = lax.while_loop(finish_cond, finish_body, (cand0, cnt0, nxt0))

    def to_rows(x):
        return jnp.broadcast_to(x, (LANES, tq)).T

    thr = to_rows(thr_t)

    tied_t = jnp.logical_and(jnp.logical_not(few), cnt_ge > kf)

    @pl.when(jnp.max(jnp.where(tied_t, 1.0, 0.0)) > 0.0)
    def _():
        (c,) = sweep(lambda c, s, x: (c[0] + jnp.where(s > x, 1.0, 0.0),), (0.0,), thr_t)
        need = to_rows(kf - col_sum(c))[:, :1]
        thr1 = thr[:, :1]
        tied1 = to_rows(jnp.where(tied_t, 1.0, 0.0))[:, :1] > 0.0
        tri = (lax.broadcasted_iota(jnp.int32, (tk, tk), 0)
               <= lax.broadcasted_iota(jnp.int32, (tk, tk), 1)).astype(jnp.bfloat16)

        def drop_body(kt, run):
            s = sc_ref[kt]
            eq = jnp.logical_and(s == thr1, tied1)
            eqf = jnp.where(eq, 1.0, 0.0)
            rank = run + _dot(eqf.astype(jnp.bfloat16), tri)
            sc_ref[kt] = jnp.where(jnp.logical_and(eq, rank > need), neg_inf, s)
            return run + jnp.sum(eqf, axis=1, keepdims=True)

        lax.fori_loop(0, n_kt, drop_body, jnp.zeros((tq, 1), jnp.float32))

    qf = q_ref[...].astype(jnp.float32)
    n_q = qf.shape[1]
    head_of_col = lax.broadcasted_iota(jnp.int32, (n_q, LANES), 0) // HEAD_DIM
    head_sel = jnp.where(head_of_col == lax.broadcasted_iota(jnp.int32, (n_q, LANES), 1), 1.0, 0.0)
    qss = _dot((qf * qf).astype(jnp.bfloat16), head_sel.astype(jnp.bfloat16))
    bound = jnp.sqrt(qss * kn_ref[0:1, :]) * BOUND_SLACK
    for c in range(n_heads // 2):
        qc = qf[:, c * LANES:(c + 1) * LANES]
        for half in range(2):
            h = 2 * c + half
            n, g = h // KV_GROUP, h % KV_GROUP
            x = qc if half == 0 else pltpu.roll(qc, HEAD_DIM, axis=1)
            b = pltpu.roll(bound, (HEAD_DIM - h) % LANES, axis=1)
            qx = jnp.where(is_head, x, jnp.where(lane == HEAD_DIM, -b, 0.0))
            qx_ref[n, g * tq:(g + 1) * tq, :] = qx.astype(qx_ref.dtype)

    thr_tile = jnp.concatenate([thr] * (tk // LANES), axis=1)

    def attend(online):
        acc_ref[...] = jnp.zeros_like(acc_ref)
        if online:
            m_ref[...] = jnp.full_like(m_ref, neg_inf)

        def att_tile(kt, carry):
            ks = pl.ds(pl.multiple_of(kt * tk, tk), tk)
            keep = sc_ref[kt] >= thr_tile
            keep_b = jnp.where(keep, 1.0, 0.0).astype(jnp.bfloat16)
            for n in range(n_kv):
                s = _dot_nt(qx_ref[n], k_ref[ks, n * LANES:(n + 1) * LANES])
                vc = v_ref[ks, n * LANES:(n + 1) * LANES]
                if online:
                    s = jnp.where(jnp.concatenate([keep] * KV_GROUP, axis=0), s, neg_inf)
                    m_old = m_ref[n]
                    m_new = jnp.maximum(m_old, jnp.max(s, axis=1, keepdims=True))
                    m_safe = jnp.where(m_new == neg_inf, 0.0, m_new)
                    p = jnp.exp2(s - m_safe).astype(jnp.bfloat16)
                    acc_ref[n] = jnp.exp2(m_old - m_safe) * acc_ref[n] + _dot(p, vc)
                    m_ref[n] = m_new
                else:
                    p = jnp.exp2(s).astype(jnp.bfloat16).reshape(KV_GROUP, tq, tk) * keep_b[None]
                    acc_ref[n] += _dot(p.reshape(KV_GROUP * tq, tk), vc)
            return carry

        lax.fori_loop(0, n_kt, att_tile, 0)

    attend(False)
    den = acc_ref[0]
    for n in range(1, n_kv):
        den = jnp.minimum(den, acc_ref[n])
    den_lane = lax.broadcasted_iota(jnp.int32, den.shape, 1) >= HEAD_DIM
    safe = jnp.min(jnp.where(den_lane, den, 1.0)) > MIN_DENOMINATOR

    @pl.when(jnp.logical_not(safe))
    def _():
        attend(True)

    for c in range(n_heads // 2):
        parts = []
        for half in range(2):
            h = 2 * c + half
            n, g = h // KV_GROUP, h % KV_GROUP
            a = acc_ref[n, g * tq:(g + 1) * tq, :]
            parts.append(a / pltpu.roll(a, HEAD_DIM, axis=1))
        out = jnp.where(is_head, parts[0], pltpu.roll(parts[1], HEAD_DIM, axis=1))
        o_ref[:, c * LANES:(c + 1) * LANES] = out.astype(o_ref.dtype)


def _dsa(q, qi, wi, kx, vx, kix, bsz, seq, k_sel):
    t, n_q = q.shape
    n_heads = n_q // HEAD_DIM
    n_kv = n_heads // KV_GROUP
    n_idx = qi.shape[1] // IDX_DIM
    tq = min(Q_TILE, seq)
    tk = min(K_TILE, seq)
    nq = seq // tq
    row = lambda b, i: (b * nq + i, 0)
    per_batch = lambda b, i: (b, 0)
    kern = functools.partial(_dsa_kernel, tq=tq, tk=tk, k_sel=k_sel, n_heads=n_heads, n_idx=n_idx)
    return pl.pallas_call(
        kern,
        out_shape=jax.ShapeDtypeStruct((t, n_q), jnp.bfloat16),
        grid=(bsz, nq),
        in_specs=[
            pl.BlockSpec((tq, n_q), row),
            pl.BlockSpec((tq, qi.shape[1]), row),
            pl.BlockSpec((tq, LANES), row),
            pl.BlockSpec((seq, kx.shape[1]), per_batch),
            pl.BlockSpec((seq, vx.shape[1]), per_batch),
            pl.BlockSpec((seq, kix.shape[1]), per_batch),
        ],
        out_specs=pl.BlockSpec((tq, n_q), row),
        scratch_shapes=[
            pltpu.VMEM((seq // tk, tq, tk), jnp.float32),
            pltpu.VMEM((seq // tk, tk, tq), jnp.float32),
            pltpu.VMEM((n_kv, KV_GROUP * tq, LANES), jnp.bfloat16),
            pltpu.VMEM((n_kv, KV_GROUP * tq, LANES), jnp.float32),
            pltpu.VMEM((n_kv, KV_GROUP * tq, 1), jnp.float32),
            pltpu.VMEM((SUBLANES, LANES), jnp.float32),
        ],
        compiler_params=_params("parallel", "arbitrary"),
        name="dsa",
    )(q, qi, wi, kx, vx, kix)


def _att_out_kernel(a_ref, h_ref, g_ref, w_ref, o_ref):
    o_ref[...] = h_ref[...] + _rms(_dot(a_ref[...], w_ref[...]), g_ref[...])


def _att_out(a, h, g, w):
    t, d = h.shape
    n = a.shape[1]
    tm = min(TOKEN_TILE, t)
    return pl.pallas_call(
        _att_out_kernel,
        out_shape=jax.ShapeDtypeStruct(h.shape, jnp.float32),
        grid=(t // tm,),
        in_specs=[
            pl.BlockSpec((tm, n), lambda i: (i, 0)),
            pl.BlockSpec((tm, d), lambda i: (i, 0)),
            pl.BlockSpec((1, d), lambda i: (0, 0)),
            pl.BlockSpec((n, d), lambda i: (0, 0)),
        ],
        out_specs=pl.BlockSpec((tm, d), lambda i: (i, 0)),
        compiler_params=_params("parallel"),
        name="att_out",
    )(a, h, g, w)


def _rope_tables(seq):
    half = HEAD_DIM // 2
    inv_freq = ROPE_THETA ** (-jnp.arange(half, dtype=jnp.float32) * 2.0 / HEAD_DIM)
    ang = jnp.arange(seq, dtype=jnp.float32)[:, None] * inv_freq[None, :]
    cos = jnp.concatenate([jnp.cos(ang)] * 4, axis=-1)
    sin = jnp.concatenate([-jnp.sin(ang), jnp.sin(ang)] * 2, axis=-1)
    return cos, sin


def _half_chunks(w, low):
    d, n = w.shape
    w3 = w.reshape(d, n // HEAD_DIM, HEAD_DIM)
    z = jnp.zeros_like(w3)
    return jnp.concatenate([w3, z] if low else [z, w3], axis=-1).reshape(d, 2 * n)


def _att_weights(w_in, d_model):
    n_heads = d_model // HEAD_DIM
    n_kv = n_heads // KV_GROUP
    n_idx = max(4, d_model // 128)
    splits = (n_heads * HEAD_DIM, n_kv * HEAD_DIM, n_kv * HEAD_DIM, n_idx * IDX_DIM, IDX_DIM, n_idx)
    offs = [0]
    for s in splits:
        offs.append(offs[-1] + s)
    wq, wk, wv, wqi, wki, wwi = (w_in[:, offs[j]:offs[j + 1]] for j in range(6))
    d = w_in.shape[0]

    def both_halves(w):
        lo = _half_chunks(w, True).reshape(d, -1, LANES)
        hi = _half_chunks(w, False).reshape(d, -1, LANES)
        return jnp.stack([lo, hi], axis=2).reshape(d, -1)

    wkx, wvx, wkix = _half_chunks(wk, True), _half_chunks(wv, True), both_halves(wki)
    wwi_p = jnp.pad(wwi, ((0, 0), (0, LANES - n_idx)))
    w_cat = jnp.concatenate([wq, wkx, wvx, wqi, wkix, wwi_p], axis=1).astype(jnp.bfloat16)
    widths = (wq.shape[1], wkx.shape[1], wvx.shape[1], wqi.shape[1], wkix.shape[1])
    return w_cat, widths, n_idx


def kernel(x, norm_g, mlp_w1, mlp_w2, ssm_w_in, ssm_lam_re, ssm_lam_im, ssm_log_dt, ssm_b_re, ssm_b_im,
           ssm_c_re, ssm_c_im, ssm_d, ssm_w_glu, ssm_w_out, att_w_in, att_w_out):
    bsz, seq, d_model = x.shape
    depth = norm_g.shape[0]
    bf = jnp.bfloat16
    h = x.reshape(bsz * seq, d_model)
    cos, sin = _rope_tables(seq)
    k_sel = min(TOPK_MAX, seq // 4)
    for i in range(depth):
        j = i // 2
        g = norm_g[i][:, None, :]
        if i % 2 == 0:
            ar, ai, bmat, cmat, dvec = _ssm_discretise(
                ssm_lam_re[j], ssm_lam_im[j], ssm_log_dt[j], ssm_b_re[j], ssm_b_im[j],
                ssm_c_re[j], ssm_c_im[j], ssm_d[j], bsz)
            e = ssm_w_in.shape[2]
            u = _ssm_in(h, g[0], ssm_w_in[j].astype(bf), bsz, seq)
            z = _ssm_scan(u, ar, ai, bmat, cmat, dvec, bsz)
            h = _ssm_out(z, h, g[1], ssm_w_glu[j].astype(bf),
                         ssm_w_out[j].astype(bf), bsz, seq)
        else:
            w_cat, widths, n_idx = _att_weights(att_w_in[j], d_model)
            q, kx, vx, qi, kix, wi = _att_in(h, g[0], w_cat, cos, sin, bsz, seq, widths,
                                             HEAD_DIM ** -0.5 * LOG2_E, n_idx ** -0.5 * IDX_DIM ** -0.5)
            a = _dsa(q, qi, wi, kx, vx, kix, bsz, seq, k_sel)
            h = _att_out(a, h, g[1], att_w_out[j].astype(bf))
        h = _mlp(h, g[2], g[3], mlp_w1[i].astype(bf), mlp_w2[i].astype(bf))
    return h.reshape(bsz, seq, d_model)
```

```python
import functools
import math

import jax
import jax.numpy as jnp
from jax import lax
from jax.experimental import pallas as pl
from jax.experimental.pallas import tpu as pltpu

NORM_EPS = 1e-6
SSM_CH = 16
SSM_STATE = 64
DT_MIN, DT_MAX = 1e-3, 1e-1
HEAD_DIM = 64
KV_GROUP = 4
IDX_DIM = 64
TOPK_MAX = 256
ROPE_THETA = 10000.0

LANES = 128
SUBLANES = 8
MXU_DIM = 256
VMEM_LIMIT_BYTES = 56 * 1024 * 1024

TOKEN_TILE = 512
MLP_TOKEN_TILE = 1024
MLP_SUB_TILE = 512
FF_TILE = 1024
SCAN_CHUNK = 64
SCAN_LANES = 512
Q_TILE = 256
K_TILE = 512
BISECT_ITERS = 16
FOLD_ROWS = 64
LOG2_E = math.log2(math.e)
BOUND_SLACK = 1.01
MIN_DENOMINATOR = 2.0 ** -60


def _params(*sem):
    return pltpu.CompilerParams(dimension_semantics=sem, vmem_limit_bytes=VMEM_LIMIT_BYTES)


def _rms(x, g):
    return x * lax.rsqrt(jnp.mean(x * x, axis=-1, keepdims=True) + NORM_EPS) * g


def _dot(a, b):
    return jnp.dot(a, b, preferred_element_type=jnp.float32)


def _dot_nt(a, b):
    return lax.dot_general(a, b, (((1,), (1,)), ((), ())), preferred_element_type=jnp.float32)


def _mix_mlp_kernel(x_ref, h_ref, g_ref, *rest, gated, sub, tf):
    if gated:
        wg_ref, wo_ref, w1_ref, w2_ref, o_ref = rest
    else:
        wo_ref, w1_ref, w2_ref, o_ref = rest
    g = g_ref[...]
    ff = w1_ref.shape[1]
    for r in range(h_ref.shape[0] // sub):
        rows = slice(r * sub, (r + 1) * sub)
        x = x_ref[rows, :]
        if gated:
            x = x * jax.nn.sigmoid(_dot(x.astype(jnp.bfloat16), wg_ref[...]))
        h = h_ref[rows, :] + _rms(_dot(x.astype(jnp.bfloat16), wo_ref[...]), g[0:1])
        xn = _rms(h, g[1:2]).astype(jnp.bfloat16)
        acc = None
        for c in range(ff // tf):
            cols = slice(c * tf, (c + 1) * tf)
            a = jnp.maximum(_dot(xn, w1_ref[:, cols]), 0.0)
            part = _dot((a * a).astype(jnp.bfloat16), w2_ref[cols, :])
            acc = part if acc is None else acc + part
        o_ref[rows, :] = h + _rms(acc, g[2:3])


def _mix_mlp(x, h, gains, w_glu, w_out, w1, w2, seq, time_major):
    t, d = h.shape
    ff = w1.shape[1]
    n = w_out.shape[0]
    tm = min(MLP_TOKEN_TILE, seq)
    nt = seq // tm
    gated = w_glu is not None
    kern = functools.partial(_mix_mlp_kernel, gated=gated, sub=min(MLP_SUB_TILE, tm), tf=min(FF_TILE, ff))
    resident = dict(pipeline_mode=pl.Buffered(1))
    whole = lambda a: pl.BlockSpec(a.shape, lambda i: (0, 0), **resident)
    x_map = (lambda i: (i % nt, i // nt)) if time_major else (lambda i: (i, 0))
    weights = ([w_glu] if gated else []) + [w_out, w1, w2]
    return pl.pallas_call(
        kern,
        out_shape=jax.ShapeDtypeStruct((t, d), jnp.float32),
        grid=(t // tm,),
        in_specs=[pl.BlockSpec((tm, n), x_map),
                  pl.BlockSpec((tm, d), lambda i: (i, 0)),
                  pl.BlockSpec(gains.shape, lambda i: (0, 0))] + [whole(w) for w in weights],
        out_specs=pl.BlockSpec((tm, d), lambda i: (i, 0)),
        compiler_params=_params("parallel"),
        name="mix_mlp",
    )(x, h, gains, *weights)


def _ssm_in_kernel(h_ref, g_ref, w_ref, u_ref):
    xn = _rms(h_ref[...], g_ref[...]).astype(jnp.bfloat16)
    u_ref[...] = _dot(xn, w_ref[...])


def _ssm_in(h, g, w, bsz, seq):
    d = h.shape[1]
    e = w.shape[1]
    tm = min(TOKEN_TILE, seq)
    nt = seq // tm
    return pl.pallas_call(
        _ssm_in_kernel,
        out_shape=jax.ShapeDtypeStruct((seq, bsz * e), jnp.float32),
        grid=(bsz, nt),
        in_specs=[
            pl.BlockSpec((tm, d), lambda b, i: (b * nt + i, 0)),
            pl.BlockSpec((1, d), lambda b, i: (0, 0)),
            pl.BlockSpec((d, e), lambda b, i: (0, 0)),
        ],
        out_specs=pl.BlockSpec((tm, e), lambda b, i: (i, b)),
        compiler_params=_params("parallel", "parallel"),
        name="ssm_in",
    )(h, g, w)


def _ssm_scan_kernel(u_ref, ar_ref, ai_ref, bmat_ref, cmat_ref, d_ref, z_ref, x_ref, st_ref, io_ref,
                     *, bsz, chunk, half, lanes):
    @pl.when(pl.program_id(0) == 0)
    def _():
        st_ref[...] = jnp.zeros_like(st_ref)

    e = d_ref.shape[1]
    n_blk = e // MXU_DIM
    sl = half // n_blk

    def state_lanes(j):
        return slice(j * sl, (j + 1) * sl), slice(half + j * sl, half + (j + 1) * sl)

    n_slab = e // LANES
    for b in range(bsz):
        for c in range(n_slab):
            io_ref[c, pl.ds(b, chunk, stride=bsz), :] = u_ref[:, b * e + c * LANES:b * e + (c + 1) * LANES]
    u = jnp.concatenate([io_ref[c] for c in range(n_slab)], axis=1)
    ub = u.astype(jnp.bfloat16)
    for j in range(n_blk):
        ch = slice(j * MXU_DIM, (j + 1) * MXU_DIM)
        for lanes_j in state_lanes(j):
            x_ref[:, lanes_j] = _dot(ub[:, ch], bmat_ref[ch, lanes_j])

    for c in range(half // lanes):
        re = slice(c * lanes, (c + 1) * lanes)
        im = slice(half + c * lanes, half + (c + 1) * lanes)
        ar = ar_ref[:, re]
        ai = ai_ref[:, re]

        def step(t, carry, re=re, im=im, ar=ar, ai=ai):
            xr, xi = carry
            rows = pl.ds(pl.multiple_of(t * bsz, bsz), bsz)
            nr = ar * xr - ai * xi + x_ref[rows, re]
            ni = ar * xi + ai * xr + x_ref[rows, im]
            x_ref[rows, re] = nr
            x_ref[rows, im] = ni
            return nr, ni

        xr, xi = lax.fori_loop(0, chunk, step, (st_ref[:, re], st_ref[:, im]))
        st_ref[:, re] = xr
        st_ref[:, im] = xi

    for j in range(n_blk):
        ch = slice(j * MXU_DIM, (j + 1) * MXU_DIM)
        re_j, im_j = state_lanes(j)
        y = (_dot(x_ref[:, re_j].astype(jnp.bfloat16), cmat_ref[re_j, ch])
             + _dot(x_ref[:, im_j].astype(jnp.bfloat16), cmat_ref[im_j, ch])
             + d_ref[:, ch] * u[:, ch])
        z = jax.nn.gelu(y)
        for c in range(MXU_DIM // LANES):
            io_ref[j * (MXU_DIM // LANES) + c] = z[:, c * LANES:(c + 1) * LANES]
    for b in range(bsz):
        for c in range(n_slab):
            z_ref[:, b * e + c * LANES:b * e + (c + 1) * LANES] = io_ref[c, pl.ds(b, chunk, stride=bsz), :]


def _ssm_scan(u, ar, ai, bmat, cmat, dvec, bsz):
    seq = u.shape[0]
    e = u.shape[1] // bsz
    n2 = bmat.shape[1]
    half = n2 // 2
    chunk = min(SCAN_CHUNK, seq)
    lanes = min(SCAN_LANES, half)
    kern = functools.partial(_ssm_scan_kernel, bsz=bsz, chunk=chunk, half=half, lanes=lanes)
    return pl.pallas_call(
        kern,
        out_shape=jax.ShapeDtypeStruct(u.shape, jnp.float32),
        grid=(seq // chunk,),
        in_specs=[
            pl.BlockSpec((chunk, bsz * e), lambda i: (i, 0)),
            pl.BlockSpec((bsz, half), lambda i: (0, 0)),
            pl.BlockSpec((bsz, half), lambda i: (0, 0)),
            pl.BlockSpec((e, n2), lambda i: (0, 0)),
            pl.BlockSpec((n2, e), lambda i: (0, 0)),
            pl.BlockSpec((1, e), lambda i: (0, 0)),
        ],
        out_specs=pl.BlockSpec((chunk, bsz * e), lambda i: (i, 0)),
        scratch_shapes=[pltpu.VMEM((chunk * bsz, n2), jnp.float32),
                        pltpu.VMEM((bsz, n2), jnp.float32),
                        pltpu.VMEM((e // LANES, chunk * bsz, LANES), jnp.float32)],
        compiler_params=_params("arbitrary"),
        name="ssm_scan",
    )(u, ar, ai, bmat, cmat, dvec)


def _ssm_discretise(lam_re, lam_im, log_dt, b_re, b_im, c_re, c_im, d_skip, bsz):
    g, p = lam_re.shape
    c = b_re.shape[2]
    dt = jnp.exp(log_dt)[:, None]
    mag = jnp.exp(lam_re * dt)
    abar_re = mag * jnp.cos(lam_im * dt)
    abar_im = mag * jnp.sin(lam_im * dt)
    den = lam_re * lam_re + lam_im * lam_im
    nr = abar_re - 1.0
    ni = abar_im
    fr = (nr * lam_re + ni * lam_im) / den
    fi = (ni * lam_re - nr * lam_im) / den
    bbar_re = fr[..., None] * b_re - fi[..., None] * b_im
    bbar_im = fr[..., None] * b_im + fi[..., None] * b_re
    bf = jnp.bfloat16
    state_of_col = jnp.arange(g * p) % p
    spread_p = (jnp.arange(p)[:, None] == state_of_col[None, :]).astype(bf)
    same_group = (jnp.arange(g * c)[:, None] // c) == (jnp.arange(g * p)[None, :] // p)

    def bd(m):
        rows = m.transpose(0, 2, 1).reshape(g * c, p).astype(bf)
        return jnp.where(same_group, jnp.dot(rows, spread_p, preferred_element_type=jnp.float32), 0.0).astype(bf)

    def cd(m):
        cols = m.transpose(2, 0, 1).reshape(p, g * c).astype(bf)
        return jnp.where(same_group.T, jnp.dot(spread_p.T, cols, preferred_element_type=jnp.float32), 0.0).astype(bf)

    bmat = jnp.concatenate([bd(bbar_re), bd(bbar_im)], axis=1)
    cmat = jnp.concatenate([cd(c_re), cd(-c_im)], axis=0)
    ar = jnp.broadcast_to(abar_re.reshape(1, g * p), (bsz, g * p))
    ai = jnp.broadcast_to(abar_im.reshape(1, g * p), (bsz, g * p))
    return ar, ai, bmat, cmat, d_skip.reshape(1, g * c)


def _rope128(x, cos, sin_signed, lane):
    swapped = jnp.where((lane % HEAD_DIM) < HEAD_DIM // 2,
                        pltpu.roll(x, LANES - HEAD_DIM // 2, axis=1),
                        pltpu.roll(x, HEAD_DIM // 2, axis=1))
    return x * cos + swapped * sin_signed


def _att_in_kernel(h_ref, g_ref, w_ref, cos_ref, sin_ref, q_ref, k_ref, v_ref, qi_ref, ki_ref, wi_ref,
                   *, n_q, n_k, n_v, n_qi, n_ki, q_scale, wi_scale):
    xn = _rms(h_ref[...], g_ref[...]).astype(jnp.bfloat16)
    proj = _dot(xn, w_ref[...])
    cos = cos_ref[...]
    sin = sin_ref[...]
    lane = lax.broadcasted_iota(jnp.int32, cos.shape, 1)
    off = 0

    def roped(ref, width, off, scale=None):
        for c in range(width // LANES):
            x = _rope128(proj[:, off + c * LANES: off + (c + 1) * LANES], cos, sin, lane)
            if scale is not None:
                x = x * scale
            ref[:, c * LANES:(c + 1) * LANES] = x.astype(ref.dtype)
        return off + width

    off = roped(q_ref, n_q, off, q_scale)
    for c in range(n_k // LANES):
        x = _rope128(proj[:, off + c * LANES: off + (c + 1) * LANES], cos, sin, lane)
        k_ref[:, c * LANES:(c + 1) * LANES] = jnp.where(lane == HEAD_DIM, 1.0, x).astype(k_ref.dtype)
    off += n_k
    for c in range(n_v // LANES):
        x = proj[:, off + c * LANES: off + (c + 1) * LANES]
        v_ref[:, c * LANES:(c + 1) * LANES] = jnp.where(lane >= HEAD_DIM, 1.0, x).astype(v_ref.dtype)
    off += n_v
    off = roped(qi_ref, n_qi, off)
    off = roped(ki_ref, n_ki, off)
    wi_ref[...] = proj[:, off:off + LANES] * wi_scale


def _att_in(h, g, w, cos, sin, bsz, seq, widths, q_scale, wi_scale):
    d = h.shape[1]
    n_q, n_k, n_v, n_qi, n_ki = widths
    ncols = w.shape[1]
    tm = min(TOKEN_TILE, seq)
    nt = seq // tm
    t = bsz * seq
    row = lambda b, i: (b * nt + i, 0)
    kern = functools.partial(_att_in_kernel, n_q=n_q, n_k=n_k, n_v=n_v, n_qi=n_qi, n_ki=n_ki,
                             q_scale=q_scale, wi_scale=wi_scale)
    bf = jnp.bfloat16
    return pl.pallas_call(
        kern,
        out_shape=[jax.ShapeDtypeStruct((t, n_q), bf), jax.ShapeDtypeStruct((t, n_k), bf),
                   jax.ShapeDtypeStruct((t, n_v), bf), jax.ShapeDtypeStruct((t, n_qi), bf),
                   jax.ShapeDtypeStruct((t, n_ki), bf), jax.ShapeDtypeStruct((t, LANES), jnp.float32)],
        grid=(bsz, nt),
        in_specs=[
            pl.BlockSpec((tm, d), row),
            pl.BlockSpec((1, d), lambda b, i: (0, 0)),
            pl.BlockSpec((d, ncols), lambda b, i: (0, 0)),
            pl.BlockSpec((tm, LANES), lambda b, i: (i, 0)),
            pl.BlockSpec((tm, LANES), lambda b, i: (i, 0)),
        ],
        out_specs=[pl.BlockSpec((tm, n_q), row), pl.BlockSpec((tm, n_k), row), pl.BlockSpec((tm, n_v), row),
                   pl.BlockSpec((tm, n_qi), row), pl.BlockSpec((tm, n_ki), row), pl.BlockSpec((tm, LANES), row)],
        compiler_params=_params("parallel", "parallel"),
        name="att_in",
    )(h, g, w, cos, sin)


def _lane_fold(x, op):
    out = x[:, :LANES]
    for j in range(1, x.shape[1] // LANES):
        out = op(out, x[:, j * LANES:(j + 1) * LANES])
    return out


def _row_fold(x, op):
    out = x[:FOLD_ROWS]
    for j in range(1, x.shape[0] // FOLD_ROWS):
        out = op(out, x[j * FOLD_ROWS:(j + 1) * FOLD_ROWS])
    return out


def _dsa_kernel(q_ref, qi_ref, wi_ref, k_ref, v_ref, ki_ref, o_ref,
                sc_ref, sct_ref, qx_ref, acc_ref, m_ref, kn_ref, *, tq, tk, k_sel, n_heads, n_idx):
    i = pl.program_id(1)
    n_kv = n_heads // KV_GROUP
    lane = lax.broadcasted_iota(jnp.int32, (tq, LANES), 1)
    is_head = lane < HEAD_DIM

    @pl.when(i == 0)
    def _():
        lane_k = lax.broadcasted_iota(jnp.int32, (tk, LANES), 1)
        head_lane = lax.broadcasted_iota(jnp.int32, kn_ref.shape, 1)
        kn = jnp.zeros(kn_ref.shape, jnp.float32)
        for n in range(n_kv):
            def body(r, c, n=n):
                x = k_ref[pl.ds(pl.multiple_of(r * tk, tk), tk), n * LANES:(n + 1) * LANES].astype(jnp.float32)
                x = jnp.where(lane_k < HEAD_DIM, x, 0.0)
                ss = jnp.sum(x * x, axis=1, keepdims=True)
                return jnp.maximum(c, jnp.max(ss, axis=0, keepdims=True))
            kmax = lax.fori_loop(0, k_ref.shape[0] // tk, body, jnp.zeros((1, 1), jnp.float32))
            kn = jnp.where(head_lane // KV_GROUP == n, kmax, kn)
        kn_ref[...] = kn

    n_kt = (i * tq + tq + tk - 1) // tk
    neg_inf = jnp.float32(-jnp.inf)
    row = i * tq + lax.broadcasted_iota(jnp.int32, (tq, tk), 0)
    col0 = lax.broadcasted_iota(jnp.int32, (tq, tk), 1)
    wi = wi_ref[...]

    def score_tile(kt, carry):
        rmax, rmin = carry
        ks = pl.ds(pl.multiple_of(kt * tk, tk), tk)
        acc = jnp.zeros((tq, tk), jnp.float32)
        for hh in range(n_idx):
            qc = qi_ref[:, (hh // 2) * LANES:(hh // 2 + 1) * LANES]
            kc = ki_ref[ks, (hh % 2) * LANES:(hh % 2 + 1) * LANES]
            acc = acc + jnp.maximum(_dot_nt(qc, kc), 0.0) * wi[:, hh:hh + 1]
        causal = (col0 + kt * tk) <= row
        masked = jnp.where(causal, acc, neg_inf)
        sc_ref[kt] = masked
        masked_t = masked.T
        sct_ref[kt] = masked_t
        rmax = jnp.maximum(rmax, _row_fold(masked_t, jnp.maximum))
        rmin = jnp.minimum(rmin, _row_fold(jnp.where(masked_t == neg_inf, -neg_inf, masked_t), jnp.minimum))
        return rmax, rmin

    rmax, rmin = lax.fori_loop(0, n_kt, score_tile,
                               (jnp.full((FOLD_ROWS, tq), neg_inf), jnp.full((FOLD_ROWS, tq), -neg_inf)))

    kf = jnp.float32(k_sel)

    def sweep(fn, init, x):
        def body(kt, c):
            for j in range(tk // FOLD_ROWS):
                c = fn(c, sct_ref[kt, j * FOLD_ROWS:(j + 1) * FOLD_ROWS, :], x)
            return c
        return lax.fori_loop(0, n_kt, body, tuple(jnp.full((FOLD_ROWS, tq), v, jnp.float32) for v in init))

    def col_sum(part):
        return jnp.sum(part, axis=0, keepdims=True)

    def col_max(part):
        return jnp.max(part, axis=0, keepdims=True)

    def count_ge(x):
        (c,) = sweep(lambda c, s, x: (c[0] + jnp.where(s >= x, 1.0, 0.0),), (0.0,), x)
        return col_sum(c)

    def bisect(_, carry):
        lo, hi = carry
        mid = 0.5 * lo + 0.5 * hi
        ge = count_ge(mid) >= kf
        return jnp.where(ge, mid, lo), jnp.where(ge, hi, mid)

    rmin = -col_max(-rmin)
    rmax = col_max(rmax)
    lo, hi = lax.fori_loop(0, BISECT_ITERS, bisect, (rmin, rmax))

    few = (i * tq + lax.broadcasted_iota(jnp.int32, (1, tq), 1)) < k_sel

    def max_le(x):
        (c,) = sweep(lambda c, s, x: (jnp.maximum(c[0], jnp.where(s <= x, s, neg_inf)),), (-jnp.inf,), x)
        return col_max(c)

    def probe(x):
        cnt, nxt = sweep(lambda c, s, x: (c[0] + jnp.where(s >= x, 1.0, 0.0),
                                          jnp.maximum(c[1], jnp.where(s < x, s, neg_inf))),
                         (0.0, -jnp.inf), x)
        return col_sum(cnt), col_max(nxt)

    cand0 = jnp.where(few, rmin, max_le(hi))
    cnt0, nxt0 = probe(cand0)

    def unresolved(cnt):
        return jnp.logical_and(jnp.logical_not(few), cnt < kf)

    def finish_cond(c):
        _, cnt, _ = c
        return jnp.max(jnp.where(unresolved(cnt), 1.0, 0.0)) > 0.0

    def finish_body(c):
        cand, cnt, nxt = c
        cand = jnp.where(unresolved(cnt), nxt, cand)
        cnt, nxt = probe(cand)
        return cand, cnt, nxt

    thr_t, cnt_ge, _ = lax.while_loop(finish_cond, finish_body, (cand0, cnt0, nxt0))

    def to_rows(x):
        return jnp.broadcast_to(x, (LANES, tq)).T

    thr = to_rows(thr_t)

    tied_t = jnp.logical_and(jnp.logical_not(few), cnt_ge > kf)

    @pl.when(jnp.max(jnp.where(tied_t, 1.0, 0.0)) > 0.0)
    def _():
        (c,) = sweep(lambda c, s, x: (c[0] + jnp.where(s > x, 1.0, 0.0),), (0.0,), thr_t)
        need = to_rows(kf - col_sum(c))[:, :1]
        thr1 = thr[:, :1]
        tied1 = to_rows(jnp.where(tied_t, 1.0, 0.0))[:, :1] > 0.0
        tri = (lax.broadcasted_iota(jnp.int32, (tk, tk), 0)
               <= lax.broadcasted_iota(jnp.int32, (tk, tk), 1)).astype(jnp.bfloat16)

        def drop_body(kt, run):
            s = sc_ref[kt]
            eq = jnp.logical_and(s == thr1, tied1)
            eqf = jnp.where(eq, 1.0, 0.0)
            rank = run + _dot(eqf.astype(jnp.bfloat16), tri)
            sc_ref[kt] = jnp.where(jnp.logical_and(eq, rank > need), neg_inf, s)
            return run + jnp.sum(eqf, axis=1, keepdims=True)

        lax.fori_loop(0, n_kt, drop_body, jnp.zeros((tq, 1), jnp.float32))

    qf = q_ref[...].astype(jnp.float32)
    n_q = qf.shape[1]
    head_of_col = lax.broadcasted_iota(jnp.int32, (n_q, LANES), 0) // HEAD_DIM
    head_sel = jnp.where(head_of_col == lax.broadcasted_iota(jnp.int32, (n_q, LANES), 1), 1.0, 0.0)
    qss = _dot((qf * qf).astype(jnp.bfloat16), head_sel.astype(jnp.bfloat16))
    bound = jnp.sqrt(qss * kn_ref[0:1, :]) * BOUND_SLACK
    for c in range(n_heads // 2):
        qc = qf[:, c * LANES:(c + 1) * LANES]
        for half in range(2):
            h = 2 * c + half
            n, g = h // KV_GROUP, h % KV_GROUP
            x = qc if half == 0 else pltpu.roll(qc, HEAD_DIM, axis=1)
            b = pltpu.roll(bound, (HEAD_DIM - h) % LANES, axis=1)
            qx = jnp.where(is_head, x, jnp.where(lane == HEAD_DIM, -b, 0.0))
            qx_ref[n, g * tq:(g + 1) * tq, :] = qx.astype(qx_ref.dtype)

    thr_tile = jnp.concatenate([thr] * (tk // LANES), axis=1)

    def attend(online):
        acc_ref[...] = jnp.zeros_like(acc_ref)
        if online:
            m_ref[...] = jnp.full_like(m_ref, neg_inf)

        def att_tile(kt, carry):
            ks = pl.ds(pl.multiple_of(kt * tk, tk), tk)
            keep = sc_ref[kt] >= thr_tile
            keep_b = jnp.where(keep, 1.0, 0.0).astype(jnp.bfloat16)
            for n in range(n_kv):
                s = _dot_nt(qx_ref[n], k_ref[ks, n * LANES:(n + 1) * LANES])
                vc = v_ref[ks, n * LANES:(n + 1) * LANES]
                if online:
                    s = jnp.where(jnp.concatenate([keep] * KV_GROUP, axis=0), s, neg_inf)
                    m_old = m_ref[n]
                    m_new = jnp.maximum(m_old, jnp.max(s, axis=1, keepdims=True))
                    m_safe = jnp.where(m_new == neg_inf, 0.0, m_new)
                    p = jnp.exp2(s - m_safe).astype(jnp.bfloat16)
                    acc_ref[n] = jnp.exp2(m_old - m_safe) * acc_ref[n] + _dot(p, vc)
                    m_ref[n] = m_new
                else:
                    p = jnp.exp2(s).astype(jnp.bfloat16).reshape(KV_GROUP, tq, tk) * keep_b[None]
                    acc_ref[n] += _dot(p.reshape(KV_GROUP * tq, tk), vc)
            return carry

        lax.fori_loop(0, n_kt, att_tile, 0)

    attend(False)
    den = acc_ref[0]
    for n in range(1, n_kv):
        den = jnp.minimum(den, acc_ref[n])
    den_lane = lax.broadcasted_iota(jnp.int32, den.shape, 1) >= HEAD_DIM
    safe = jnp.min(jnp.where(den_lane, den, 1.0)) > MIN_DENOMINATOR

    @pl.when(jnp.logical_not(safe))
    def _():
        attend(True)

    for c in range(n_heads // 2):
        parts = []
        for half in range(2):
            h = 2 * c + half
            n, g = h // KV_GROUP, h % KV_GROUP
            a = acc_ref[n, g * tq:(g + 1) * tq, :]
            parts.append(a / pltpu.roll(a, HEAD_DIM, axis=1))
        out = jnp.where(is_head, parts[0], pltpu.roll(parts[1], HEAD_DIM, axis=1))
        o_ref[:, c * LANES:(c + 1) * LANES] = out.astype(o_ref.dtype)


def _dsa(q, qi, wi, kx, vx, kix, bsz, seq, k_sel):
    t, n_q = q.shape
    n_heads = n_q // HEAD_DIM
    n_kv = n_heads // KV_GROUP
    n_idx = qi.shape[1] // IDX_DIM
    tq = min(Q_TILE, seq)
    tk = min(K_TILE, seq)
    nq = seq // tq
    row = lambda b, i: (b * nq + i, 0)
    per_batch = lambda b, i: (b, 0)
    kern = functools.partial(_dsa_kernel, tq=tq, tk=tk, k_sel=k_sel, n_heads=n_heads, n_idx=n_idx)
    return pl.pallas_call(
        kern,
        out_shape=jax.ShapeDtypeStruct((t, n_q), jnp.bfloat16),
        grid=(bsz, nq),
        in_specs=[
            pl.BlockSpec((tq, n_q), row),
            pl.BlockSpec((tq, qi.shape[1]), row),
            pl.BlockSpec((tq, LANES), row),
            pl.BlockSpec((seq, kx.shape[1]), per_batch),
            pl.BlockSpec((seq, vx.shape[1]), per_batch),
            pl.BlockSpec((seq, kix.shape[1]), per_batch),
        ],
        out_specs=pl.BlockSpec((tq, n_q), row),
        scratch_shapes=[
            pltpu.VMEM((seq // tk, tq, tk), jnp.float32),
            pltpu.VMEM((seq // tk, tk, tq), jnp.float32),
            pltpu.VMEM((n_kv, KV_GROUP * tq, LANES), jnp.bfloat16),
            pltpu.VMEM((n_kv, KV_GROUP * tq, LANES), jnp.float32),
            pltpu.VMEM((n_kv, KV_GROUP * tq, 1), jnp.float32),
            pltpu.VMEM((SUBLANES, LANES), jnp.float32),
        ],
        compiler_params=_params("parallel", "arbitrary"),
        name="dsa",
    )(q, qi, wi, kx, vx, kix)


def _rope_tables(seq):
    half = HEAD_DIM // 2
    inv_freq = ROPE_THETA ** (-jnp.arange(half, dtype=jnp.float32) * 2.0 / HEAD_DIM)
    ang = jnp.arange(seq, dtype=jnp.float32)[:, None] * inv_freq[None, :]
    cos = jnp.concatenate([jnp.cos(ang)] * 4, axis=-1)
    sin = jnp.concatenate([-jnp.sin(ang), jnp.sin(ang)] * 2, axis=-1)
    return cos, sin


def _half_chunks(w, low):
    d, n = w.shape
    w3 = w.reshape(d, n // HEAD_DIM, HEAD_DIM)
    z = jnp.zeros_like(w3)
    return jnp.concatenate([w3, z] if low else [z, w3], axis=-1).reshape(d, 2 * n)


def _att_weights(w_in, d_model):
    n_heads = d_model // HEAD_DIM
    n_kv = n_heads // KV_GROUP
    n_idx = max(4, d_model // 128)
    splits = (n_heads * HEAD_DIM, n_kv * HEAD_DIM, n_kv * HEAD_DIM, n_idx * IDX_DIM, IDX_DIM, n_idx)
    offs = [0]
    for s in splits:
        offs.append(offs[-1] + s)
    wq, wk, wv, wqi, wki, wwi = (w_in[:, offs[j]:offs[j + 1]] for j in range(6))
    d = w_in.shape[0]

    def both_halves(w):
        lo = _half_chunks(w, True).reshape(d, -1, LANES)
        hi = _half_chunks(w, False).reshape(d, -1, LANES)
        return jnp.stack([lo, hi], axis=2).reshape(d, -1)

    wkx, wvx, wkix = _half_chunks(wk, True), _half_chunks(wv, True), both_halves(wki)
    wwi_p = jnp.pad(wwi, ((0, 0), (0, LANES - n_idx)))
    w_cat = jnp.concatenate([wq, wkx, wvx, wqi, wkix, wwi_p], axis=1).astype(jnp.bfloat16)
    widths = (wq.shape[1], wkx.shape[1], wvx.shape[1], wqi.shape[1], wkix.shape[1])
    return w_cat, widths, n_idx


def kernel(x, norm_g, mlp_w1, mlp_w2, ssm_w_in, ssm_lam_re, ssm_lam_im, ssm_log_dt, ssm_b_re, ssm_b_im,
           ssm_c_re, ssm_c_im, ssm_d, ssm_w_glu, ssm_w_out, att_w_in, att_w_out):
    bsz, seq, d_model = x.shape
    depth = norm_g.shape[0]
    bf = jnp.bfloat16
    h = x.reshape(bsz * seq, d_model)
    cos, sin = _rope_tables(seq)
    k_sel = min(TOPK_MAX, seq // 4)
    for i in range(depth):
        j = i // 2
        g = norm_g[i][:, None, :]
        if i % 2 == 0:
            ar, ai, bmat, cmat, dvec = _ssm_discretise(
                ssm_lam_re[j], ssm_lam_im[j], ssm_log_dt[j], ssm_b_re[j], ssm_b_im[j],
                ssm_c_re[j], ssm_c_im[j], ssm_d[j], bsz)
            u = _ssm_in(h, g[0], ssm_w_in[j].astype(bf), bsz, seq)
            mixed = _ssm_scan(u, ar, ai, bmat, cmat, dvec, bsz)
            w_glu, w_out = ssm_w_glu[j].astype(bf), ssm_w_out[j].astype(bf)
        else:
            w_cat, widths, n_idx = _att_weights(att_w_in[j], d_model)
            q, kx, vx, qi, kix, wi = _att_in(h, g[0], w_cat, cos, sin, bsz, seq, widths,
                                             HEAD_DIM ** -0.5 * LOG2_E, n_idx ** -0.5 * IDX_DIM ** -0.5)
            mixed = _dsa(q, qi, wi, kx, vx, kix, bsz, seq, k_sel)
            w_glu, w_out = None, att_w_out[j].astype(bf)
        h = _mix_mlp(mixed, h, norm_g[i][1:], w_glu, w_out, mlp_w1[i].astype(bf), mlp_w2[i].astype(bf),
                     seq, time_major=(i % 2 == 0))
    return h.reshape(bsz, seq, d_model)
```

```python
import functools
import math

import jax
import jax.numpy as jnp
from jax import lax
from jax.experimental import pallas as pl
from jax.experimental.pallas import tpu as pltpu

NORM_EPS = 1e-6
SSM_CH = 16
SSM_STATE = 64
DT_MIN, DT_MAX = 1e-3, 1e-1
HEAD_DIM = 64
KV_GROUP = 4
IDX_DIM = 64
TOPK_MAX = 256
ROPE_THETA = 10000.0

LANES = 128
SUBLANES = 8
MXU_DIM = 256
VMEM_LIMIT_BYTES = 56 * 1024 * 1024

TOKEN_TILE = 512
MLP_TOKEN_TILE = 1024
MLP_SUB_TILE = 512
FF_TILE = 1024
SCAN_CHUNK = 64
SCAN_LANES = 1024
SCAN_UNROLL = 2
Q_TILE = 256
K_TILE = 512
BISECT_ITERS = 16
FOLD_ROWS = 64
LOG2_E = math.log2(math.e)
BOUND_SLACK = 1.01
MIN_DENOMINATOR = 2.0 ** -60


def _params(*sem):
    return pltpu.CompilerParams(dimension_semantics=sem, vmem_limit_bytes=VMEM_LIMIT_BYTES)


def _rms(x, g):
    return x * lax.rsqrt(jnp.mean(x * x, axis=-1, keepdims=True) + NORM_EPS) * g


def _dot(a, b):
    return jnp.dot(a, b, preferred_element_type=jnp.float32)


def _dot_nt(a, b):
    return lax.dot_general(a, b, (((1,), (1,)), ((), ())), preferred_element_type=jnp.float32)


def _mix_mlp_kernel(x_ref, h_ref, g_ref, *rest, gated, sub, tf):
    if gated:
        wg_ref, wo_ref, w1_ref, w2_ref, o_ref = rest
    else:
        wo_ref, w1_ref, w2_ref, o_ref = rest
    g = g_ref[...]
    ff = w1_ref.shape[1]
    for r in range(h_ref.shape[0] // sub):
        rows = slice(r * sub, (r + 1) * sub)
        x = x_ref[rows, :]
        if gated:
            x = x * jax.nn.sigmoid(_dot(x.astype(jnp.bfloat16), wg_ref[...]))
        h = h_ref[rows, :] + _rms(_dot(x.astype(jnp.bfloat16), wo_ref[...]), g[0:1])
        xn = _rms(h, g[1:2]).astype(jnp.bfloat16)
        acc = None
        for c in range(ff // tf):
            cols = slice(c * tf, (c + 1) * tf)
            a = jnp.maximum(_dot(xn, w1_ref[:, cols]), 0.0)
            part = _dot((a * a).astype(jnp.bfloat16), w2_ref[cols, :])
            acc = part if acc is None else acc + part
        o_ref[rows, :] = h + _rms(acc, g[2:3])


def _mix_mlp(x, h, gains, w_glu, w_out, w1, w2, seq, time_major):
    t, d = h.shape
    ff = w1.shape[1]
    n = w_out.shape[0]
    tm = min(MLP_TOKEN_TILE, seq)
    nt = seq // tm
    gated = w_glu is not None
    kern = functools.partial(_mix_mlp_kernel, gated=gated, sub=min(MLP_SUB_TILE, tm), tf=min(FF_TILE, ff))
    resident = dict(pipeline_mode=pl.Buffered(1))
    whole = lambda a: pl.BlockSpec(a.shape, lambda i: (0, 0), **resident)
    x_map = (lambda i: (i % nt, i // nt)) if time_major else (lambda i: (i, 0))
    weights = ([w_glu] if gated else []) + [w_out, w1, w2]
    return pl.pallas_call(
        kern,
        out_shape=jax.ShapeDtypeStruct((t, d), jnp.float32),
        grid=(t // tm,),
        in_specs=[pl.BlockSpec((tm, n), x_map),
                  pl.BlockSpec((tm, d), lambda i: (i, 0)),
                  pl.BlockSpec(gains.shape, lambda i: (0, 0))] + [whole(w) for w in weights],
        out_specs=pl.BlockSpec((tm, d), lambda i: (i, 0)),
        compiler_params=_params("parallel"),
        name="mix_mlp",
    )(x, h, gains, *weights)


def _ssm_in_kernel(h_ref, g_ref, w_ref, u_ref):
    xn = _rms(h_ref[...], g_ref[...]).astype(jnp.bfloat16)
    u_ref[...] = _dot(xn, w_ref[...])


def _ssm_in(h, g, w, bsz, seq):
    d = h.shape[1]
    e = w.shape[1]
    tm = min(TOKEN_TILE, seq)
    nt = seq // tm
    return pl.pallas_call(
        _ssm_in_kernel,
        out_shape=jax.ShapeDtypeStruct((seq, bsz * e), jnp.float32),
        grid=(bsz, nt),
        in_specs=[
            pl.BlockSpec((tm, d), lambda b, i: (b * nt + i, 0)),
            pl.BlockSpec((1, d), lambda b, i: (0, 0)),
            pl.BlockSpec((d, e), lambda b, i: (0, 0)),
        ],
        out_specs=pl.BlockSpec((tm, e), lambda b, i: (i, b)),
        compiler_params=_params("parallel", "parallel"),
        name="ssm_in",
    )(h, g, w)


def _ssm_scan_kernel(u_ref, ar_ref, ai_ref, bmat_ref, cmat_ref, d_ref, z_ref, x_ref, st_ref, io_ref,
                     *, bsz, chunk, half, lanes):
    @pl.when(pl.program_id(0) == 0)
    def _():
        st_ref[...] = jnp.zeros_like(st_ref)

    e = d_ref.shape[1]
    n_blk = e // MXU_DIM
    sl = half // n_blk

    def state_lanes(j):
        return slice(j * sl, (j + 1) * sl), slice(half + j * sl, half + (j + 1) * sl)

    n_slab = e // LANES
    for b in range(bsz):
        for c in range(n_slab):
            io_ref[c, pl.ds(b, chunk, stride=bsz), :] = u_ref[:, b * e + c * LANES:b * e + (c + 1) * LANES]
    u = jnp.concatenate([io_ref[c] for c in range(n_slab)], axis=1)
    ub = u.astype(jnp.bfloat16)
    for j in range(n_blk):
        ch = slice(j * MXU_DIM, (j + 1) * MXU_DIM)
        for lanes_j in state_lanes(j):
            x_ref[:, lanes_j] = _dot(ub[:, ch], bmat_ref[ch, lanes_j])

    for c in range(half // lanes):
        re = slice(c * lanes, (c + 1) * lanes)
        im = slice(half + c * lanes, half + (c + 1) * lanes)
        ar = ar_ref[:, re]
        ai = ai_ref[:, re]

        def step(t, carry, re=re, im=im, ar=ar, ai=ai):
            xr, xi = carry
            rows = pl.ds(pl.multiple_of(t * bsz, bsz), bsz)
            nr = ar * xr - ai * xi + x_ref[rows, re]
            ni = ar * xi + ai * xr + x_ref[rows, im]
            x_ref[rows, re] = nr
            x_ref[rows, im] = ni
            return nr, ni

        xr, xi = lax.fori_loop(0, chunk, step, (st_ref[:, re], st_ref[:, im]), unroll=SCAN_UNROLL)
        st_ref[:, re] = xr
        st_ref[:, im] = xi

    for j in range(n_blk):
        ch = slice(j * MXU_DIM, (j + 1) * MXU_DIM)
        re_j, im_j = state_lanes(j)
        y = (_dot(x_ref[:, re_j].astype(jnp.bfloat16), cmat_ref[re_j, ch])
             + _dot(x_ref[:, im_j].astype(jnp.bfloat16), cmat_ref[im_j, ch])
             + d_ref[:, ch] * u[:, ch])
        z = jax.nn.gelu(y)
        for c in range(MXU_DIM // LANES):
            io_ref[j * (MXU_DIM // LANES) + c] = z[:, c * LANES:(c + 1) * LANES]
    for b in range(bsz):
        for c in range(n_slab):
            z_ref[:, b * e + c * LANES:b * e + (c + 1) * LANES] = io_ref[c, pl.ds(b, chunk, stride=bsz), :]


def _ssm_scan(u, ar, ai, bmat, cmat, dvec, bsz):
    seq = u.shape[0]
    e = u.shape[1] // bsz
    n2 = bmat.shape[1]
    half = n2 // 2
    chunk = min(SCAN_CHUNK, seq)
    lanes = min(SCAN_LANES, half)
    kern = functools.partial(_ssm_scan_kernel, bsz=bsz, chunk=chunk, half=half, lanes=lanes)
    return pl.pallas_call(
        kern,
        out_shape=jax.ShapeDtypeStruct(u.shape, jnp.float32),
        grid=(seq // chunk,),
        in_specs=[
            pl.BlockSpec((chunk, bsz * e), lambda i: (i, 0)),
            pl.BlockSpec((bsz, half), lambda i: (0, 0)),
            pl.BlockSpec((bsz, half), lambda i: (0, 0)),
            pl.BlockSpec((e, n2), lambda i: (0, 0)),
            pl.BlockSpec((n2, e), lambda i: (0, 0)),
            pl.BlockSpec((1, e), lambda i: (0, 0)),
        ],
        out_specs=pl.BlockSpec((chunk, bsz * e), lambda i: (i, 0)),
        scratch_shapes=[pltpu.VMEM((chunk * bsz, n2), jnp.float32),
                        pltpu.VMEM((bsz, n2), jnp.float32),
                        pltpu.VMEM((e // LANES, chunk * bsz, LANES), jnp.float32)],
        compiler_params=_params("arbitrary"),
        name="ssm_scan",
    )(u, ar, ai, bmat, cmat, dvec)


def _ssm_discretise(lam_re, lam_im, log_dt, b_re, b_im, c_re, c_im, d_skip, bsz):
    g, p = lam_re.shape
    c = b_re.shape[2]
    dt = jnp.exp(log_dt)[:, None]
    mag = jnp.exp(lam_re * dt)
    abar_re = mag * jnp.cos(lam_im * dt)
    abar_im = mag * jnp.sin(lam_im * dt)
    den = lam_re * lam_re + lam_im * lam_im
    nr = abar_re - 1.0
    ni = abar_im
    fr = (nr * lam_re + ni * lam_im) / den
    fi = (ni * lam_re - nr * lam_im) / den
    bbar_re = fr[..., None] * b_re - fi[..., None] * b_im
    bbar_im = fr[..., None] * b_im + fi[..., None] * b_re
    bf = jnp.bfloat16
    state_of_col = jnp.arange(g * p) % p
    spread_p = (jnp.arange(p)[:, None] == state_of_col[None, :]).astype(bf)
    same_group = (jnp.arange(g * c)[:, None] // c) == (jnp.arange(g * p)[None, :] // p)

    def bd(m):
        rows = m.transpose(0, 2, 1).reshape(g * c, p).astype(bf)
        return jnp.where(same_group, jnp.dot(rows, spread_p, preferred_element_type=jnp.float32), 0.0).astype(bf)

    def cd(m):
        cols = m.transpose(2, 0, 1).reshape(p, g * c).astype(bf)
        return jnp.where(same_group.T, jnp.dot(spread_p.T, cols, preferred_element_type=jnp.float32), 0.0).astype(bf)

    bmat = jnp.concatenate([bd(bbar_re), bd(bbar_im)], axis=1)
    cmat = jnp.concatenate([cd(c_re), cd(-c_im)], axis=0)
    ar = jnp.broadcast_to(abar_re.reshape(1, g * p), (bsz, g * p))
    ai = jnp.broadcast_to(abar_im.reshape(1, g * p), (bsz, g * p))
    return ar, ai, bmat, cmat, d_skip.reshape(1, g * c)


def _rope128(x, cos, sin_signed, lane):
    swapped = jnp.where((lane % HEAD_DIM) < HEAD_DIM // 2,
                        pltpu.roll(x, LANES - HEAD_DIM // 2, axis=1),
                        pltpu.roll(x, HEAD_DIM // 2, axis=1))
    return x * cos + swapped * sin_signed


def _att_in_kernel(h_ref, g_ref, w_ref, cos_ref, sin_ref, q_ref, k_ref, v_ref, qi_ref, ki_ref, wi_ref,
                   *, n_q, n_kv, n_qi, q_scale, wi_scale):
    xn = _rms(h_ref[...], g_ref[...]).astype(jnp.bfloat16)
    proj = _dot(xn, w_ref[...])
    cos = cos_ref[...]
    sin = sin_ref[...]
    lane = lax.broadcasted_iota(jnp.int32, cos.shape, 1)
    is_head = lane < HEAD_DIM

    def chunk(off, c):
        return proj[:, off + c * LANES: off + (c + 1) * LANES]

    def head_of(x, n):
        return x if n % 2 == 0 else pltpu.roll(x, HEAD_DIM, axis=1)

    for c in range(n_q // LANES):
        q_ref[:, c * LANES:(c + 1) * LANES] = (_rope128(chunk(0, c), cos, sin, lane) * q_scale).astype(q_ref.dtype)
    off = n_q
    for n in range(n_kv):
        k2 = _rope128(chunk(off, n // 2), cos, sin, lane)
        kx = jnp.where(is_head, head_of(k2, n), jnp.where(lane == HEAD_DIM, 1.0, 0.0))
        k_ref[:, n * LANES:(n + 1) * LANES] = kx.astype(k_ref.dtype)
    off += n_kv * HEAD_DIM
    for n in range(n_kv):
        vx = jnp.where(is_head, head_of(chunk(off, n // 2), n), 1.0)
        v_ref[:, n * LANES:(n + 1) * LANES] = vx.astype(v_ref.dtype)
    off += n_kv * HEAD_DIM
    for c in range(n_qi // LANES):
        qi_ref[:, c * LANES:(c + 1) * LANES] = _rope128(chunk(off, c), cos, sin, lane).astype(qi_ref.dtype)
    off += n_qi
    last = chunk(off, 0)
    ki_lo = jnp.where(is_head, _rope128(last, cos, sin, lane), 0.0)
    ki_ref[:, :LANES] = ki_lo.astype(ki_ref.dtype)
    ki_ref[:, LANES:] = pltpu.roll(ki_lo, HEAD_DIM, axis=1).astype(ki_ref.dtype)
    wi_ref[...] = pltpu.roll(last, HEAD_DIM, axis=1) * wi_scale


def _att_in(h, g, w, cos, sin, bsz, seq, n_q, n_kv, n_qi, q_scale, wi_scale):
    d = h.shape[1]
    n_k = n_v = n_kv * LANES
    n_ki = 2 * LANES
    ncols = w.shape[1]
    tm = min(TOKEN_TILE, seq)
    nt = seq // tm
    t = bsz * seq
    row = lambda b, i: (b * nt + i, 0)
    kern = functools.partial(_att_in_kernel, n_q=n_q, n_kv=n_kv, n_qi=n_qi, q_scale=q_scale, wi_scale=wi_scale)
    bf = jnp.bfloat16
    return pl.pallas_call(
        kern,
        out_shape=[jax.ShapeDtypeStruct((t, n_q), bf), jax.ShapeDtypeStruct((t, n_k), bf),
                   jax.ShapeDtypeStruct((t, n_v), bf), jax.ShapeDtypeStruct((t, n_qi), bf),
                   jax.ShapeDtypeStruct((t, n_ki), bf), jax.ShapeDtypeStruct((t, LANES), jnp.float32)],
        grid=(bsz, nt),
        in_specs=[
            pl.BlockSpec((tm, d), row),
            pl.BlockSpec((1, d), lambda b, i: (0, 0)),
            pl.BlockSpec((d, ncols), lambda b, i: (0, 0)),
            pl.BlockSpec((tm, LANES), lambda b, i: (i, 0)),
            pl.BlockSpec((tm, LANES), lambda b, i: (i, 0)),
        ],
        out_specs=[pl.BlockSpec((tm, n_q), row), pl.BlockSpec((tm, n_k), row), pl.BlockSpec((tm, n_v), row),
                   pl.BlockSpec((tm, n_qi), row), pl.BlockSpec((tm, n_ki), row), pl.BlockSpec((tm, LANES), row)],
        compiler_params=_params("parallel", "parallel"),
        name="att_in",
    )(h, g, w, cos, sin)


def _lane_fold(x, op):
    out = x[:, :LANES]
    for j in range(1, x.shape[1] // LANES):
        out = op(out, x[:, j * LANES:(j + 1) * LANES])
    return out


def _row_fold(x, op):
    out = x[:FOLD_ROWS]
    for j in range(1, x.shape[0] // FOLD_ROWS):
        out = op(out, x[j * FOLD_ROWS:(j + 1) * FOLD_ROWS])
    return out


def _dsa_kernel(q_ref, qi_ref, wi_ref, k_ref, v_ref, ki_ref, o_ref,
                sc_ref, sct_ref, wb_ref, qx_ref, acc_ref, m_ref, kn_ref, *, tq, tk, k_sel, n_heads, n_idx):
    i = pl.program_id(1)
    n_kv = n_heads // KV_GROUP
    lane = lax.broadcasted_iota(jnp.int32, (tq, LANES), 1)
    is_head = lane < HEAD_DIM

    @pl.when(i == 0)
    def _():
        lane_k = lax.broadcasted_iota(jnp.int32, (tk, LANES), 1)
        head_lane = lax.broadcasted_iota(jnp.int32, kn_ref.shape, 1)
        kn = jnp.zeros(kn_ref.shape, jnp.float32)
        for n in range(n_kv):
            def body(r, c, n=n):
                x = k_ref[pl.ds(pl.multiple_of(r * tk, tk), tk), n * LANES:(n + 1) * LANES].astype(jnp.float32)
                x = jnp.where(lane_k < HEAD_DIM, x, 0.0)
                ss = jnp.sum(x * x, axis=1, keepdims=True)
                return jnp.maximum(c, jnp.max(ss, axis=0, keepdims=True))
            kmax = lax.fori_loop(0, k_ref.shape[0] // tk, body, jnp.zeros((1, 1), jnp.float32))
            kn = jnp.where(head_lane // KV_GROUP == n, kmax, kn)
        kn_ref[...] = kn

    n_kt = (i * tq + tq + tk - 1) // tk
    neg_inf = jnp.float32(-jnp.inf)
    row = i * tq + lax.broadcasted_iota(jnp.int32, (tq, tk), 0)
    col0 = lax.broadcasted_iota(jnp.int32, (tq, tk), 1)
    wi = wi_ref[...]
    for hh in range(n_idx):
        wb_ref[hh] = jnp.broadcast_to(wi[:, hh:hh + 1], (tq, LANES))

    def score_tile(kt, carry):
        rmax, rmin = carry
        ks = pl.ds(pl.multiple_of(kt * tk, tk), tk)
        acc = jnp.zeros((tq, tk), jnp.float32)
        for hh in range(n_idx):
            qc = qi_ref[:, (hh // 2) * LANES:(hh // 2 + 1) * LANES]
            kc = ki_ref[ks, (hh % 2) * LANES:(hh % 2 + 1) * LANES]
            acc = acc + jnp.maximum(_dot_nt(qc, kc), 0.0) * jnp.concatenate([wb_ref[hh]] * (tk // LANES), axis=1)
        causal = (col0 + kt * tk) <= row
        masked = jnp.where(causal, acc, neg_inf)
        sc_ref[kt] = masked
        masked_t = masked.T
        sct_ref[kt] = masked_t
        rmax = jnp.maximum(rmax, _row_fold(masked_t, jnp.maximum))
        rmin = jnp.minimum(rmin, _row_fold(jnp.where(masked_t == neg_inf, -neg_inf, masked_t), jnp.minimum))
        return rmax, rmin

    rmax, rmin = lax.fori_loop(0, n_kt, score_tile,
                               (jnp.full((FOLD_ROWS, tq), neg_inf), jnp.full((FOLD_ROWS, tq), -neg_inf)))

    kf = jnp.float32(k_sel)

    def sweep(fn, init, x):
        def body(kt, c):
            for j in range(tk // FOLD_ROWS):
                c = fn(c, sct_ref[kt, j * FOLD_ROWS:(j + 1) * FOLD_ROWS, :], x)
            return c
        return lax.fori_loop(0, n_kt, body, tuple(jnp.full((FOLD_ROWS, tq), v, jnp.float32) for v in init))

    def col_sum(part):
        return jnp.sum(part, axis=0, keepdims=True)

    def col_max(part):
        return jnp.max(part, axis=0, keepdims=True)

    def count_ge(x):
        (c,) = sweep(lambda c, s, x: (c[0] + jnp.where(s >= x, 1.0, 0.0),), (0.0,), x)
        return col_sum(c)

    def bisect(_, carry):
        lo, hi = carry
        mid = 0.5 * lo + 0.5 * hi
        ge = count_ge(mid) >= kf
        return jnp.where(ge, mid, lo), jnp.where(ge, hi, mid)

    rmin = -col_max(-rmin)
    rmax = col_max(rmax)
    lo, hi = lax.fori_loop(0, BISECT_ITERS, bisect, (rmin, rmax))

    few = (i * tq + lax.broadcasted_iota(jnp.int32, (1, tq), 1)) < k_sel

    def max_le(x):
        (c,) = sweep(lambda c, s, x: (jnp.maximum(c[0], jnp.where(s <= x, s, neg_inf)),), (-jnp.inf,), x)
        return col_max(c)

    def probe(x):
        cnt, nxt = sweep(lambda c, s, x: (c[0] + jnp.where(s >= x, 1.0, 0.0),
                                          jnp.maximum(c[1], jnp.where(s < x, s, neg_inf))),
                         (0.0, -jnp.inf), x)
        return col_sum(cnt), col_max(nxt)

    cand0 = jnp.where(few, rmin, max_le(hi))
    cnt0, nxt0 = probe(cand0)

    def unresolved(cnt):
        return jnp.logical_and(jnp.logical_not(few), cnt < kf)

    def finish_cond(c):
        _, cnt, _ = c
        return jnp.max(jnp.where(unresolved(cnt), 1.0, 0.0)) > 0.0

    def finish_body(c):
        cand, cnt, nxt = c
        cand = jnp.where(unresolved(cnt), nxt, cand)
        cnt, nxt = probe(cand)
        return cand, cnt, nxt

    thr_t, cnt_ge, _ = lax.while_loop(finish_cond, finish_body, (cand0, cnt0, nxt0))

    def to_rows(x):
        return jnp.broadcast_to(x, (LANES, tq)).T

    thr = to_rows(thr_t)

    tied_t = jnp.logical_and(jnp.logical_not(few), cnt_ge > kf)

    @pl.when(jnp.max(jnp.where(tied_t, 1.0, 0.0)) > 0.0)
    def _():
        (c,) = sweep(lambda c, s, x: (c[0] + jnp.where(s > x, 1.0, 0.0),), (0.0,), thr_t)
        need = to_rows(kf - col_sum(c))[:, :1]
        thr1 = thr[:, :1]
        tied1 = to_rows(jnp.where(tied_t, 1.0, 0.0))[:, :1] > 0.0
        tri = (lax.broadcasted_iota(jnp.int32, (tk, tk), 0)
               <= lax.broadcasted_iota(jnp.int32, (tk, tk), 1)).astype(jnp.bfloat16)

        def drop_body(kt, run):
            s = sc_ref[kt]
            eq = jnp.logical_and(s == thr1, tied1)
            eqf = jnp.where(eq, 1.0, 0.0)
            rank = run + _dot(eqf.astype(jnp.bfloat16), tri)
            sc_ref[kt] = jnp.where(jnp.logical_and(eq, rank > need), neg_inf, s)
            return run + jnp.sum(eqf, axis=1, keepdims=True)

        lax.fori_loop(0, n_kt, drop_body, jnp.zeros((tq, 1), jnp.float32))

    qf = q_ref[...].astype(jnp.float32)
    n_q = qf.shape[1]
    head_of_col = lax.broadcasted_iota(jnp.int32, (n_q, LANES), 0) // HEAD_DIM
    head_sel = jnp.where(head_of_col == lax.broadcasted_iota(jnp.int32, (n_q, LANES), 1), 1.0, 0.0)
    qss = _dot((qf * qf).astype(jnp.bfloat16), head_sel.astype(jnp.bfloat16))
    bound = jnp.sqrt(qss * kn_ref[0:1, :]) * BOUND_SLACK
    for c in range(n_heads // 2):
        qc = qf[:, c * LANES:(c + 1) * LANES]
        for half in range(2):
            h = 2 * c + half
            n, g = h // KV_GROUP, h % KV_GROUP
            x = qc if half == 0 else pltpu.roll(qc, HEAD_DIM, axis=1)
            b = pltpu.roll(bound, (HEAD_DIM - h) % LANES, axis=1)
            qx = jnp.where(is_head, x, jnp.where(lane == HEAD_DIM, -b, 0.0))
            qx_ref[n, g * tq:(g + 1) * tq, :] = qx.astype(qx_ref.dtype)

    thr_tile = jnp.concatenate([thr] * (tk // LANES), axis=1)

    def attend(online):
        acc_ref[...] = jnp.zeros_like(acc_ref)
        if online:
            m_ref[...] = jnp.full_like(m_ref, neg_inf)

        def att_tile(kt, carry):
            ks = pl.ds(pl.multiple_of(kt * tk, tk), tk)
            keep = sc_ref[kt] >= thr_tile
            keep_b = jnp.where(keep, 1.0, 0.0).astype(jnp.bfloat16)
            for n in range(n_kv):
                s = _dot_nt(qx_ref[n], k_ref[ks, n * LANES:(n + 1) * LANES])
                vc = v_ref[ks, n * LANES:(n + 1) * LANES]
                if online:
                    s = jnp.where(jnp.concatenate([keep] * KV_GROUP, axis=0), s, neg_inf)
                    m_old = m_ref[n]
                    m_new = jnp.maximum(m_old, jnp.max(s, axis=1, keepdims=True))
                    m_safe = jnp.where(m_new == neg_inf, 0.0, m_new)
                    p = jnp.exp2(s - m_safe).astype(jnp.bfloat16)
                    acc_ref[n] = jnp.exp2(m_old - m_safe) * acc_ref[n] + _dot(p, vc)
                    m_ref[n] = m_new
                else:
                    p = jnp.exp2(s).astype(jnp.bfloat16).reshape(KV_GROUP, tq, tk) * keep_b[None]
                    acc_ref[n] += _dot(p.reshape(KV_GROUP * tq, tk), vc)
            return carry

        lax.fori_loop(0, n_kt, att_tile, 0)

    attend(False)
    den = acc_ref[0]
    for n in range(1, n_kv):
        den = jnp.minimum(den, acc_ref[n])
    den_lane = lax.broadcasted_iota(jnp.int32, den.shape, 1) >= HEAD_DIM
    safe = jnp.min(jnp.where(den_lane, den, 1.0)) > MIN_DENOMINATOR

    @pl.when(jnp.logical_not(safe))
    def _():
        attend(True)

    for c in range(n_heads // 2):
        parts = []
        for half in range(2):
            h = 2 * c + half
            n, g = h // KV_GROUP, h % KV_GROUP
            a = acc_ref[n, g * tq:(g + 1) * tq, :]
            parts.append(a / pltpu.roll(a, HEAD_DIM, axis=1))
        out = jnp.where(is_head, parts[0], pltpu.roll(parts[1], HEAD_DIM, axis=1))
        o_ref[:, c * LANES:(c + 1) * LANES] = out.astype(o_ref.dtype)


def _dsa(q, qi, wi, kx, vx, kix, bsz, seq, k_sel):
    t, n_q = q.shape
    n_heads = n_q // HEAD_DIM
    n_kv = n_heads // KV_GROUP
    n_idx = qi.shape[1] // IDX_DIM
    tq = min(Q_TILE, seq)
    tk = min(K_TILE, seq)
    nq = seq // tq
    row = lambda b, i: (b * nq + i, 0)
    per_batch = lambda b, i: (b, 0)
    kern = functools.partial(_dsa_kernel, tq=tq, tk=tk, k_sel=k_sel, n_heads=n_heads, n_idx=n_idx)
    return pl.pallas_call(
        kern,
        out_shape=jax.ShapeDtypeStruct((t, n_q), jnp.bfloat16),
        grid=(bsz, nq),
        in_specs=[
            pl.BlockSpec((tq, n_q), row),
            pl.BlockSpec((tq, qi.shape[1]), row),
            pl.BlockSpec((tq, LANES), row),
            pl.BlockSpec((seq, kx.shape[1]), per_batch),
            pl.BlockSpec((seq, vx.shape[1]), per_batch),
            pl.BlockSpec((seq, kix.shape[1]), per_batch),
        ],
        out_specs=pl.BlockSpec((tq, n_q), row),
        scratch_shapes=[
            pltpu.VMEM((seq // tk, tq, tk), jnp.float32),
            pltpu.VMEM((seq // tk, tk, tq), jnp.float32),
            pltpu.VMEM((n_idx, tq, LANES), jnp.float32),
            pltpu.VMEM((n_kv, KV_GROUP * tq, LANES), jnp.bfloat16),
            pltpu.VMEM((n_kv, KV_GROUP * tq, LANES), jnp.float32),
            pltpu.VMEM((n_kv, KV_GROUP * tq, 1), jnp.float32),
            pltpu.VMEM((SUBLANES, LANES), jnp.float32),
        ],
        compiler_params=_params("parallel", "arbitrary"),
        name="dsa",
    )(q, qi, wi, kx, vx, kix)


def _rope_tables(seq):
    half = HEAD_DIM // 2
    inv_freq = ROPE_THETA ** (-jnp.arange(half, dtype=jnp.float32) * 2.0 / HEAD_DIM)
    ang = jnp.arange(seq, dtype=jnp.float32)[:, None] * inv_freq[None, :]
    cos = jnp.concatenate([jnp.cos(ang)] * 4, axis=-1)
    sin = jnp.concatenate([-jnp.sin(ang), jnp.sin(ang)] * 2, axis=-1)
    return cos, sin


def _att_weights(w_in, d_model):
    n_heads = d_model // HEAD_DIM
    n_kv = n_heads // KV_GROUP
    n_idx = max(4, d_model // 128)
    n_q, n_qi = n_heads * HEAD_DIM, n_idx * IDX_DIM
    cols = n_q + 2 * n_kv * HEAD_DIM + n_qi + IDX_DIM + n_idx
    assert w_in.shape[1] == cols and IDX_DIM + n_idx <= LANES and n_kv % 2 == 0
    padded = -(-cols // LANES) * LANES
    w_cat = jnp.pad(w_in, ((0, 0), (0, padded - cols))).astype(jnp.bfloat16)
    return w_cat, n_q, n_kv, n_qi, n_idx


def kernel(x, norm_g, mlp_w1, mlp_w2, ssm_w_in, ssm_lam_re, ssm_lam_im, ssm_log_dt, ssm_b_re, ssm_b_im,
           ssm_c_re, ssm_c_im, ssm_d, ssm_w_glu, ssm_w_out, att_w_in, att_w_out):
    bsz, seq, d_model = x.shape
    depth = norm_g.shape[0]
    bf = jnp.bfloat16
    h = x.reshape(bsz * seq, d_model)
    cos, sin = _rope_tables(seq)
    k_sel = min(TOPK_MAX, seq // 4)
    for i in range(depth):
        j = i // 2
        g = norm_g[i][:, None, :]
        if i % 2 == 0:
            ar, ai, bmat, cmat, dvec = _ssm_discretise(
                ssm_lam_re[j], ssm_lam_im[j], ssm_log_dt[j], ssm_b_re[j], ssm_b_im[j],
                ssm_c_re[j], ssm_c_im[j], ssm_d[j], bsz)
            u = _ssm_in(h, g[0], ssm_w_in[j].astype(bf), bsz, seq)
            mixed = _ssm_scan(u, ar, ai, bmat, cmat, dvec, bsz)
            w_glu, w_out = ssm_w_glu[j].astype(bf), ssm_w_out[j].astype(bf)
        else:
            w_cat, n_q, n_kv, n_qi, n_idx = _att_weights(att_w_in[j], d_model)
            q, kx, vx, qi, kix, wi = _att_in(h, g[0], w_cat, cos, sin, bsz, seq, n_q, n_kv, n_qi,
                                             HEAD_DIM ** -0.5 * LOG2_E, n_idx ** -0.5 * IDX_DIM ** -0.5)
            mixed = _dsa(q, qi, wi, kx, vx, kix, bsz, seq, k_sel)
            w_glu, w_out = None, att_w_out[j].astype(bf)
        h = _mix_mlp(mixed, h, norm_g[i][1:], w_glu, w_out, mlp_w1[i].astype(bf), mlp_w2[i].astype(bf),
                     seq, time_major=(i % 2 == 0))
    return h.reshape(bsz, seq, d_model)
```

```python
import functools
import math

import jax
import jax.numpy as jnp
from jax import lax
from jax.experimental import pallas as pl
from jax.experimental.pallas import tpu as pltpu

NORM_EPS = 1e-6
SSM_CH = 16
SSM_STATE = 64
DT_MIN, DT_MAX = 1e-3, 1e-1
HEAD_DIM = 64
KV_GROUP = 4
IDX_DIM = 64
TOPK_MAX = 256
ROPE_THETA = 10000.0

LANES = 128
SUBLANES = 8
MXU_DIM = 256
VMEM_LIMIT_BYTES = 56 * 1024 * 1024

TOKEN_TILE = 512
MLP_TOKEN_TILE = 1024
MLP_SUB_TILE = 512
FF_TILE = 1024
SCAN_CHUNK = 64
SCAN_LANES = 1024
SCAN_UNROLL = 2
Q_TILE = 256
K_TILE = 512
COARSE_ITERS = 9
BISECT_ITERS = 8
BF16_ULP = 2.0 ** -7
FOLD_ROWS = 64
LOG2_E = math.log2(math.e)
BOUND_SLACK = 1.01
MIN_DENOMINATOR = 2.0 ** -60


def _params(*sem):
    return pltpu.CompilerParams(dimension_semantics=sem, vmem_limit_bytes=VMEM_LIMIT_BYTES)


def _rms(x, g):
    return x * lax.rsqrt(jnp.mean(x * x, axis=-1, keepdims=True) + NORM_EPS) * g


def _dot(a, b):
    return jnp.dot(a, b, preferred_element_type=jnp.float32)


def _dot_nt(a, b):
    return lax.dot_general(a, b, (((1,), (1,)), ((), ())), preferred_element_type=jnp.float32)


def _mix_mlp_kernel(x_ref, h_ref, g_ref, *rest, gated, sub, tf):
    if gated:
        wg_ref, wo_ref, w1_ref, w2_ref, o_ref = rest
    else:
        wo_ref, w1_ref, w2_ref, o_ref = rest
    g = g_ref[...]
    ff = w1_ref.shape[1]
    for r in range(h_ref.shape[0] // sub):
        rows = slice(r * sub, (r + 1) * sub)
        x = x_ref[rows, :]
        if gated:
            x = x * jax.nn.sigmoid(_dot(x.astype(jnp.bfloat16), wg_ref[...]))
        h = h_ref[rows, :] + _rms(_dot(x.astype(jnp.bfloat16), wo_ref[...]), g[0:1])
        xn = _rms(h, g[1:2]).astype(jnp.bfloat16)
        acc = None
        for c in range(ff // tf):
            cols = slice(c * tf, (c + 1) * tf)
            a = jnp.maximum(_dot(xn, w1_ref[:, cols]), 0.0)
            part = _dot((a * a).astype(jnp.bfloat16), w2_ref[cols, :])
            acc = part if acc is None else acc + part
        o_ref[rows, :] = h + _rms(acc, g[2:3])


def _mix_mlp(x, h, gains, w_glu, w_out, w1, w2, seq, time_major):
    t, d = h.shape
    ff = w1.shape[1]
    n = w_out.shape[0]
    tm = min(MLP_TOKEN_TILE, seq)
    nt = seq // tm
    gated = w_glu is not None
    kern = functools.partial(_mix_mlp_kernel, gated=gated, sub=min(MLP_SUB_TILE, tm), tf=min(FF_TILE, ff))
    resident = dict(pipeline_mode=pl.Buffered(1))
    whole = lambda a: pl.BlockSpec(a.shape, lambda i: (0, 0), **resident)
    x_map = (lambda i: (i % nt, i // nt)) if time_major else (lambda i: (i, 0))
    weights = ([w_glu] if gated else []) + [w_out, w1, w2]
    return pl.pallas_call(
        kern,
        out_shape=jax.ShapeDtypeStruct((t, d), jnp.float32),
        grid=(t // tm,),
        in_specs=[pl.BlockSpec((tm, n), x_map),
                  pl.BlockSpec((tm, d), lambda i: (i, 0)),
                  pl.BlockSpec(gains.shape, lambda i: (0, 0))] + [whole(w) for w in weights],
        out_specs=pl.BlockSpec((tm, d), lambda i: (i, 0)),
        compiler_params=_params("parallel"),
        name="mix_mlp",
    )(x, h, gains, *weights)


def _ssm_in_kernel(h_ref, g_ref, w_ref, u_ref):
    xn = _rms(h_ref[...], g_ref[...]).astype(jnp.bfloat16)
    u_ref[...] = _dot(xn, w_ref[...])


def _ssm_in(h, g, w, bsz, seq):
    d = h.shape[1]
    e = w.shape[1]
    tm = min(TOKEN_TILE, seq)
    nt = seq // tm
    return pl.pallas_call(
        _ssm_in_kernel,
        out_shape=jax.ShapeDtypeStruct((seq, bsz * e), jnp.float32),
        grid=(bsz, nt),
        in_specs=[
            pl.BlockSpec((tm, d), lambda b, i: (b * nt + i, 0)),
            pl.BlockSpec((1, d), lambda b, i: (0, 0)),
            pl.BlockSpec((d, e), lambda b, i: (0, 0)),
        ],
        out_specs=pl.BlockSpec((tm, e), lambda b, i: (i, b)),
        compiler_params=_params("parallel", "parallel"),
        name="ssm_in",
    )(h, g, w)


def _ssm_scan_kernel(u_ref, ar_ref, ai_ref, bmat_ref, cmat_ref, d_ref, z_ref, x_ref, st_ref, io_ref,
                     *, bsz, chunk, half, lanes):
    @pl.when(pl.program_id(0) == 0)
    def _():
        st_ref[...] = jnp.zeros_like(st_ref)

    e = d_ref.shape[1]
    n_blk = e // MXU_DIM
    sl = half // n_blk

    def state_lanes(j):
        return slice(j * sl, (j + 1) * sl), slice(half + j * sl, half + (j + 1) * sl)

    n_slab = e // LANES
    for b in range(bsz):
        for c in range(n_slab):
            io_ref[c, pl.ds(b, chunk, stride=bsz), :] = u_ref[:, b * e + c * LANES:b * e + (c + 1) * LANES]
    u = jnp.concatenate([io_ref[c] for c in range(n_slab)], axis=1)
    ub = u.astype(jnp.bfloat16)
    for j in range(n_blk):
        ch = slice(j * MXU_DIM, (j + 1) * MXU_DIM)
        for lanes_j in state_lanes(j):
            x_ref[:, lanes_j] = _dot(ub[:, ch], bmat_ref[ch, lanes_j])

    for c in range(half // lanes):
        re = slice(c * lanes, (c + 1) * lanes)
        im = slice(half + c * lanes, half + (c + 1) * lanes)
        ar = ar_ref[:, re]
        ai = ai_ref[:, re]

        def step(t, carry, re=re, im=im, ar=ar, ai=ai):
            xr, xi = carry
            rows = pl.ds(pl.multiple_of(t * bsz, bsz), bsz)
            nr = ar * xr - ai * xi + x_ref[rows, re]
            ni = ar * xi + ai * xr + x_ref[rows, im]
            x_ref[rows, re] = nr
            x_ref[rows, im] = ni
            return nr, ni

        xr, xi = lax.fori_loop(0, chunk, step, (st_ref[:, re], st_ref[:, im]), unroll=SCAN_UNROLL)
        st_ref[:, re] = xr
        st_ref[:, im] = xi

    for j in range(n_blk):
        ch = slice(j * MXU_DIM, (j + 1) * MXU_DIM)
        re_j, im_j = state_lanes(j)
        y = (_dot(x_ref[:, re_j].astype(jnp.bfloat16), cmat_ref[re_j, ch])
             + _dot(x_ref[:, im_j].astype(jnp.bfloat16), cmat_ref[im_j, ch])
             + d_ref[:, ch] * u[:, ch])
        z = jax.nn.gelu(y)
        for c in range(MXU_DIM // LANES):
            io_ref[j * (MXU_DIM // LANES) + c] = z[:, c * LANES:(c + 1) * LANES]
    for b in range(bsz):
        for c in range(n_slab):
            z_ref[:, b * e + c * LANES:b * e + (c + 1) * LANES] = io_ref[c, pl.ds(b, chunk, stride=bsz), :]


def _ssm_scan(u, ar, ai, bmat, cmat, dvec, bsz):
    seq = u.shape[0]
    e = u.shape[1] // bsz
    n2 = bmat.shape[1]
    half = n2 // 2
    chunk = min(SCAN_CHUNK, seq)
    lanes = min(SCAN_LANES, half)
    kern = functools.partial(_ssm_scan_kernel, bsz=bsz, chunk=chunk, half=half, lanes=lanes)
    return pl.pallas_call(
        kern,
        out_shape=jax.ShapeDtypeStruct(u.shape, jnp.float32),
        grid=(seq // chunk,),
        in_specs=[
            pl.BlockSpec((chunk, bsz * e), lambda i: (i, 0)),
            pl.BlockSpec((bsz, half), lambda i: (0, 0)),
            pl.BlockSpec((bsz, half), lambda i: (0, 0)),
            pl.BlockSpec((e, n2), lambda i: (0, 0)),
            pl.BlockSpec((n2, e), lambda i: (0, 0)),
            pl.BlockSpec((1, e), lambda i: (0, 0)),
        ],
        out_specs=pl.BlockSpec((chunk, bsz * e), lambda i: (i, 0)),
        scratch_shapes=[pltpu.VMEM((chunk * bsz, n2), jnp.float32),
                        pltpu.VMEM((bsz, n2), jnp.float32),
                        pltpu.VMEM((e // LANES, chunk * bsz, LANES), jnp.float32)],
        compiler_params=_params("arbitrary"),
        name="ssm_scan",
    )(u, ar, ai, bmat, cmat, dvec)


def _ssm_discretise(lam_re, lam_im, log_dt, b_re, b_im, c_re, c_im, d_skip, bsz):
    g, p = lam_re.shape
    c = b_re.shape[2]
    dt = jnp.exp(log_dt)[:, None]
    mag = jnp.exp(lam_re * dt)
    abar_re = mag * jnp.cos(lam_im * dt)
    abar_im = mag * jnp.sin(lam_im * dt)
    den = lam_re * lam_re + lam_im * lam_im
    nr = abar_re - 1.0
    ni = abar_im
    fr = (nr * lam_re + ni * lam_im) / den
    fi = (ni * lam_re - nr * lam_im) / den
    bbar_re = fr[..., None] * b_re - fi[..., None] * b_im
    bbar_im = fr[..., None] * b_im + fi[..., None] * b_re
    bf = jnp.bfloat16
    state_of_col = jnp.arange(g * p) % p
    spread_p = (jnp.arange(p)[:, None] == state_of_col[None, :]).astype(bf)
    same_group = (jnp.arange(g * c)[:, None] // c) == (jnp.arange(g * p)[None, :] // p)

    def bd(m):
        rows = m.transpose(0, 2, 1).reshape(g * c, p).astype(bf)
        return jnp.where(same_group, jnp.dot(rows, spread_p, preferred_element_type=jnp.float32), 0.0).astype(bf)

    def cd(m):
        cols = m.transpose(2, 0, 1).reshape(p, g * c).astype(bf)
        return jnp.where(same_group.T, jnp.dot(spread_p.T, cols, preferred_element_type=jnp.float32), 0.0).astype(bf)

    bmat = jnp.concatenate([bd(bbar_re), bd(bbar_im)], axis=1)
    cmat = jnp.concatenate([cd(c_re), cd(-c_im)], axis=0)
    ar = jnp.broadcast_to(abar_re.reshape(1, g * p), (bsz, g * p))
    ai = jnp.broadcast_to(abar_im.reshape(1, g * p), (bsz, g * p))
    return ar, ai, bmat, cmat, d_skip.reshape(1, g * c)


def _rope128(x, cos, sin_signed, lane):
    swapped = jnp.where((lane % HEAD_DIM) < HEAD_DIM // 2,
                        pltpu.roll(x, LANES - HEAD_DIM // 2, axis=1),
                        pltpu.roll(x, HEAD_DIM // 2, axis=1))
    return x * cos + swapped * sin_signed


def _att_in_kernel(h_ref, g_ref, w_ref, cos_ref, sin_ref, q_ref, k_ref, v_ref, qi_ref, ki_ref, wi_ref,
                   *, n_q, n_kv, n_qi, q_scale, wi_scale):
    xn = _rms(h_ref[...], g_ref[...]).astype(jnp.bfloat16)
    proj = _dot(xn, w_ref[...])
    cos = cos_ref[...]
    sin = sin_ref[...]
    lane = lax.broadcasted_iota(jnp.int32, cos.shape, 1)
    is_head = lane < HEAD_DIM

    def chunk(off, c):
        return proj[:, off + c * LANES: off + (c + 1) * LANES]

    def head_of(x, n):
        return x if n % 2 == 0 else pltpu.roll(x, HEAD_DIM, axis=1)

    for c in range(n_q // LANES):
        q_ref[:, c * LANES:(c + 1) * LANES] = (_rope128(chunk(0, c), cos, sin, lane) * q_scale).astype(q_ref.dtype)
    off = n_q
    for n in range(n_kv):
        k2 = _rope128(chunk(off, n // 2), cos, sin, lane)
        kx = jnp.where(is_head, head_of(k2, n), jnp.where(lane == HEAD_DIM, 1.0, 0.0))
        k_ref[:, n * LANES:(n + 1) * LANES] = kx.astype(k_ref.dtype)
    off += n_kv * HEAD_DIM
    for n in range(n_kv):
        vx = jnp.where(is_head, head_of(chunk(off, n // 2), n), 1.0)
        v_ref[:, n * LANES:(n + 1) * LANES] = vx.astype(v_ref.dtype)
    off += n_kv * HEAD_DIM
    for c in range(n_qi // LANES):
        qi_ref[:, c * LANES:(c + 1) * LANES] = _rope128(chunk(off, c), cos, sin, lane).astype(qi_ref.dtype)
    off += n_qi
    last = chunk(off, 0)
    ki_lo = jnp.where(is_head, _rope128(last, cos, sin, lane), 0.0)
    ki_ref[:, :LANES] = ki_lo.astype(ki_ref.dtype)
    ki_ref[:, LANES:] = pltpu.roll(ki_lo, HEAD_DIM, axis=1).astype(ki_ref.dtype)
    wi_ref[...] = pltpu.roll(last, HEAD_DIM, axis=1) * wi_scale


def _att_in(h, g, w, cos, sin, bsz, seq, n_q, n_kv, n_qi, q_scale, wi_scale):
    d = h.shape[1]
    n_k = n_v = n_kv * LANES
    n_ki = 2 * LANES
    ncols = w.shape[1]
    tm = min(TOKEN_TILE, seq)
    nt = seq // tm
    t = bsz * seq
    row = lambda b, i: (b * nt + i, 0)
    kern = functools.partial(_att_in_kernel, n_q=n_q, n_kv=n_kv, n_qi=n_qi, q_scale=q_scale, wi_scale=wi_scale)
    bf = jnp.bfloat16
    return pl.pallas_call(
        kern,
        out_shape=[jax.ShapeDtypeStruct((t, n_q), bf), jax.ShapeDtypeStruct((t, n_k), bf),
                   jax.ShapeDtypeStruct((t, n_v), bf), jax.ShapeDtypeStruct((t, n_qi), bf),
                   jax.ShapeDtypeStruct((t, n_ki), bf), jax.ShapeDtypeStruct((t, LANES), jnp.float32)],
        grid=(bsz, nt),
        in_specs=[
            pl.BlockSpec((tm, d), row),
            pl.BlockSpec((1, d), lambda b, i: (0, 0)),
            pl.BlockSpec((d, ncols), lambda b, i: (0, 0)),
            pl.BlockSpec((tm, LANES), lambda b, i: (i, 0)),
            pl.BlockSpec((tm, LANES), lambda b, i: (i, 0)),
        ],
        out_specs=[pl.BlockSpec((tm, n_q), row), pl.BlockSpec((tm, n_k), row), pl.BlockSpec((tm, n_v), row),
                   pl.BlockSpec((tm, n_qi), row), pl.BlockSpec((tm, n_ki), row), pl.BlockSpec((tm, LANES), row)],
        compiler_params=_params("parallel", "parallel"),
        name="att_in",
    )(h, g, w, cos, sin)


def _lane_fold(x, op):
    out = x[:, :LANES]
    for j in range(1, x.shape[1] // LANES):
        out = op(out, x[:, j * LANES:(j + 1) * LANES])
    return out


def _row_fold(x, op):
    out = x[:FOLD_ROWS]
    for j in range(1, x.shape[0] // FOLD_ROWS):
        out = op(out, x[j * FOLD_ROWS:(j + 1) * FOLD_ROWS])
    return out


def _dsa_kernel(q_ref, qi_ref, wi_ref, k_ref, v_ref, ki_ref, o_ref,
                sc_ref, sct_ref, sct16_ref, wb_ref, qx_ref, acc_ref, m_ref, kn_ref,
                *, tq, tk, k_sel, n_heads, n_idx):
    i = pl.program_id(1)
    n_kv = n_heads // KV_GROUP
    lane = lax.broadcasted_iota(jnp.int32, (tq, LANES), 1)
    is_head = lane < HEAD_DIM

    @pl.when(i == 0)
    def _():
        lane_k = lax.broadcasted_iota(jnp.int32, (tk, LANES), 1)
        head_lane = lax.broadcasted_iota(jnp.int32, kn_ref.shape, 1)
        kn = jnp.zeros(kn_ref.shape, jnp.float32)
        for n in range(n_kv):
            def body(r, c, n=n):
                x = k_ref[pl.ds(pl.multiple_of(r * tk, tk), tk), n * LANES:(n + 1) * LANES].astype(jnp.float32)
                x = jnp.where(lane_k < HEAD_DIM, x, 0.0)
                ss = jnp.sum(x * x, axis=1, keepdims=True)
                return jnp.maximum(c, jnp.max(ss, axis=0, keepdims=True))
            kmax = lax.fori_loop(0, k_ref.shape[0] // tk, body, jnp.zeros((1, 1), jnp.float32))
            kn = jnp.where(head_lane // KV_GROUP == n, kmax, kn)
        kn_ref[...] = kn

    n_kt = (i * tq + tq + tk - 1) // tk
    neg_inf = jnp.float32(-jnp.inf)
    row = i * tq + lax.broadcasted_iota(jnp.int32, (tq, tk), 0)
    col0 = lax.broadcasted_iota(jnp.int32, (tq, tk), 1)
    wi = wi_ref[...]
    for hh in range(n_idx):
        wb_ref[hh] = jnp.broadcast_to(wi[:, hh:hh + 1], (tq, LANES))

    def score_tile(kt, carry):
        rmax, rmin = carry
        ks = pl.ds(pl.multiple_of(kt * tk, tk), tk)
        acc = jnp.zeros((tq, tk), jnp.float32)
        for hh in range(n_idx):
            qc = qi_ref[:, (hh // 2) * LANES:(hh // 2 + 1) * LANES]
            kc = ki_ref[ks, (hh % 2) * LANES:(hh % 2 + 1) * LANES]
            acc = acc + jnp.maximum(_dot_nt(qc, kc), 0.0) * jnp.concatenate([wb_ref[hh]] * (tk // LANES), axis=1)
        causal = (col0 + kt * tk) <= row
        masked = jnp.where(causal, acc, neg_inf)
        sc_ref[kt] = masked
        masked_t = masked.T
        sct_ref[kt] = masked_t
        sct16_ref[kt] = masked_t.astype(jnp.bfloat16)
        rmax = jnp.maximum(rmax, _row_fold(masked_t, jnp.maximum))
        rmin = jnp.minimum(rmin, _row_fold(jnp.where(masked_t == neg_inf, -neg_inf, masked_t), jnp.minimum))
        return rmax, rmin

    rmax, rmin = lax.fori_loop(0, n_kt, score_tile,
                               (jnp.full((FOLD_ROWS, tq), neg_inf), jnp.full((FOLD_ROWS, tq), -neg_inf)))

    kf = jnp.float32(k_sel)

    def sweep(fn, init, x):
        def body(kt, c):
            for j in range(tk // FOLD_ROWS):
                c = fn(c, sct_ref[kt, j * FOLD_ROWS:(j + 1) * FOLD_ROWS, :], x)
            return c
        return lax.fori_loop(0, n_kt, body, tuple(jnp.full((FOLD_ROWS, tq), v, jnp.float32) for v in init))

    def col_sum(part):
        return jnp.sum(part, axis=0, keepdims=True)

    def col_max(part):
        return jnp.max(part, axis=0, keepdims=True)

    def count_ge(x):
        (c,) = sweep(lambda c, s, x: (c[0] + jnp.where(s >= x, 1.0, 0.0),), (0.0,), x)
        return col_sum(c)

    def bisect(_, carry):
        lo, hi = carry
        mid = 0.5 * lo + 0.5 * hi
        ge = count_ge(mid) >= kf
        return jnp.where(ge, mid, lo), jnp.where(ge, hi, mid)

    def count_ge_coarse(x16):
        one, zero = jnp.bfloat16(1), jnp.bfloat16(0)

        def body(kt, c):
            for j in range(tk // FOLD_ROWS):
                c = c + jnp.where(sct16_ref[kt, j * FOLD_ROWS:(j + 1) * FOLD_ROWS, :] >= x16, one, zero)
            return c
        part = lax.fori_loop(0, n_kt, body, jnp.zeros((FOLD_ROWS, tq), jnp.bfloat16))
        return col_sum(part.astype(jnp.float32))

    def bisect_coarse(_, carry):
        lo, hi = carry
        mid16 = (0.5 * lo + 0.5 * hi).astype(jnp.bfloat16)
        mid = mid16.astype(jnp.float32)
        ge = count_ge_coarse(mid16) >= kf
        return jnp.where(ge, mid, lo), jnp.where(ge, hi, mid)

    rmin = -col_max(-rmin)
    rmax = col_max(rmax)
    lo, hi = lax.fori_loop(0, COARSE_ITERS, bisect_coarse, (rmin, rmax))
    lo = lo - (jnp.abs(lo) * BF16_ULP + jnp.float32(1e-30))
    lo, hi = lax.fori_loop(0, BISECT_ITERS, bisect, (lo, hi))

    few = (i * tq + lax.broadcasted_iota(jnp.int32, (1, tq), 1)) < k_sel

    def max_le(x):
        (c,) = sweep(lambda c, s, x: (jnp.maximum(c[0], jnp.where(s <= x, s, neg_inf)),), (-jnp.inf,), x)
        return col_max(c)

    def probe(x):
        cnt, nxt = sweep(lambda c, s, x: (c[0] + jnp.where(s >= x, 1.0, 0.0),
                                          jnp.maximum(c[1], jnp.where(s < x, s, neg_inf))),
                         (0.0, -jnp.inf), x)
        return col_sum(cnt), col_max(nxt)

    cand0 = jnp.where(few, rmin, max_le(hi))
    cnt0, nxt0 = probe(cand0)

    def unresolved(cnt):
        return jnp.logical_and(jnp.logical_not(few), cnt < kf)

    def finish_cond(c):
        _, cnt, _ = c
        return jnp.max(jnp.where(unresolved(cnt), 1.0, 0.0)) > 0.0

    def finish_body(c):
        cand, cnt, nxt = c
        cand = jnp.where(unresolved(cnt), nxt, cand)
        cnt, nxt = probe(cand)
        return cand, cnt, nxt

    thr_t, cnt_ge, _ = lax.while_loop(finish_cond, finish_body, (cand0, cnt0, nxt0))

    def to_rows(x):
        return jnp.broadcast_to(x, (LANES, tq)).T

    thr = to_rows(thr_t)

    tied_t = jnp.logical_and(jnp.logical_not(few), cnt_ge > kf)

    @pl.when(jnp.max(jnp.where(tied_t, 1.0, 0.0)) > 0.0)
    def _():
        (c,) = sweep(lambda c, s, x: (c[0] + jnp.where(s > x, 1.0, 0.0),), (0.0,), thr_t)
        need = to_rows(kf - col_sum(c))[:, :1]
        thr1 = thr[:, :1]
        tied1 = to_rows(jnp.where(tied_t, 1.0, 0.0))[:, :1] > 0.0
        tri = (lax.broadcasted_iota(jnp.int32, (tk, tk), 0)
               <= lax.broadcasted_iota(jnp.int32, (tk, tk), 1)).astype(jnp.bfloat16)

        def drop_body(kt, run):
            s = sc_ref[kt]
            eq = jnp.logical_and(s == thr1, tied1)
            eqf = jnp.where(eq, 1.0, 0.0)
            rank = run + _dot(eqf.astype(jnp.bfloat16), tri)
            sc_ref[kt] = jnp.where(jnp.logical_and(eq, rank > need), neg_inf, s)
            return run + jnp.sum(eqf, axis=1, keepdims=True)

        lax.fori_loop(0, n_kt, drop_body, jnp.zeros((tq, 1), jnp.float32))

    qf = q_ref[...].astype(jnp.float32)
    n_q = qf.shape[1]
    head_of_col = lax.broadcasted_iota(jnp.int32, (n_q, LANES), 0) // HEAD_DIM
    head_sel = jnp.where(head_of_col == lax.broadcasted_iota(jnp.int32, (n_q, LANES), 1), 1.0, 0.0)
    qss = _dot((qf * qf).astype(jnp.bfloat16), head_sel.astype(jnp.bfloat16))
    bound = jnp.sqrt(qss * kn_ref[0:1, :]) * BOUND_SLACK
    for c in range(n_heads // 2):
        qc = qf[:, c * LANES:(c + 1) * LANES]
        for half in range(2):
            h = 2 * c + half
            n, g = h // KV_GROUP, h % KV_GROUP
            x = qc if half == 0 else pltpu.roll(qc, HEAD_DIM, axis=1)
            b = pltpu.roll(bound, (HEAD_DIM - h) % LANES, axis=1)
            qx = jnp.where(is_head, x, jnp.where(lane == HEAD_DIM, -b, 0.0))
            qx_ref[n, g * tq:(g + 1) * tq, :] = qx.astype(qx_ref.dtype)

    thr_tile = jnp.concatenate([thr] * (tk // LANES), axis=1)

    def attend(online):
        acc_ref[...] = jnp.zeros_like(acc_ref)
        if online:
            m_ref[...] = jnp.full_like(m_ref, neg_inf)

        def att_tile(kt, carry):
            ks = pl.ds(pl.multiple_of(kt * tk, tk), tk)
            keep = sc_ref[kt] >= thr_tile
            keep_b = jnp.where(keep, 1.0, 0.0).astype(jnp.bfloat16)
            for n in range(n_kv):
                s = _dot_nt(qx_ref[n], k_ref[ks, n * LANES:(n + 1) * LANES])
                vc = v_ref[ks, n * LANES:(n + 1) * LANES]
                if online:
                    s = jnp.where(jnp.concatenate([keep] * KV_GROUP, axis=0), s, neg_inf)
                    m_old = m_ref[n]
                    m_new = jnp.maximum(m_old, jnp.max(s, axis=1, keepdims=True))
                    m_safe = jnp.where(m_new == neg_inf, 0.0, m_new)
                    p = jnp.exp2(s - m_safe).astype(jnp.bfloat16)
                    acc_ref[n] = jnp.exp2(m_old - m_safe) * acc_ref[n] + _dot(p, vc)
                    m_ref[n] = m_new
                else:
                    p = jnp.exp2(s).astype(jnp.bfloat16).reshape(KV_GROUP, tq, tk) * keep_b[None]
                    acc_ref[n] += _dot(p.reshape(KV_GROUP * tq, tk), vc)
            return carry

        lax.fori_loop(0, n_kt, att_tile, 0)

    attend(False)
    den = acc_ref[0]
    for n in range(1, n_kv):
        den = jnp.minimum(den, acc_ref[n])
    den_lane = lax.broadcasted_iota(jnp.int32, den.shape, 1) >= HEAD_DIM
    safe = jnp.min(jnp.where(den_lane, den, 1.0)) > MIN_DENOMINATOR

    @pl.when(jnp.logical_not(safe))
    def _():
        attend(True)

    for c in range(n_heads // 2):
        parts = []
        for half in range(2):
            h = 2 * c + half
            n, g = h // KV_GROUP, h % KV_GROUP
            a = acc_ref[n, g * tq:(g + 1) * tq, :]
            parts.append(a / pltpu.roll(a, HEAD_DIM, axis=1))
        out = jnp.where(is_head, parts[0], pltpu.roll(parts[1], HEAD_DIM, axis=1))
        o_ref[:, c * LANES:(c + 1) * LANES] = out.astype(o_ref.dtype)


def _dsa(q, qi, wi, kx, vx, kix, bsz, seq, k_sel):
    t, n_q = q.shape
    n_heads = n_q // HEAD_DIM
    n_kv = n_heads // KV_GROUP
    n_idx = qi.shape[1] // IDX_DIM
    tq = min(Q_TILE, seq)
    tk = min(K_TILE, seq)
    nq = seq // tq
    row = lambda b, i: (b * nq + i, 0)
    per_batch = lambda b, i: (b, 0)
    kern = functools.partial(_dsa_kernel, tq=tq, tk=tk, k_sel=k_sel, n_heads=n_heads, n_idx=n_idx)
    return pl.pallas_call(
        kern,
        out_shape=jax.ShapeDtypeStruct((t, n_q), jnp.bfloat16),
        grid=(bsz, nq),
        in_specs=[
            pl.BlockSpec((tq, n_q), row),
            pl.BlockSpec((tq, qi.shape[1]), row),
            pl.BlockSpec((tq, LANES), row),
            pl.BlockSpec((seq, kx.shape[1]), per_batch),
            pl.BlockSpec((seq, vx.shape[1]), per_batch),
            pl.BlockSpec((seq, kix.shape[1]), per_batch),
        ],
        out_specs=pl.BlockSpec((tq, n_q), row),
        scratch_shapes=[
            pltpu.VMEM((seq // tk, tq, tk), jnp.float32),
            pltpu.VMEM((seq // tk, tk, tq), jnp.float32),
            pltpu.VMEM((seq // tk, tk, tq), jnp.bfloat16),
            pltpu.VMEM((n_idx, tq, LANES), jnp.float32),
            pltpu.VMEM((n_kv, KV_GROUP * tq, LANES), jnp.bfloat16),
            pltpu.VMEM((n_kv, KV_GROUP * tq, LANES), jnp.float32),
            pltpu.VMEM((n_kv, KV_GROUP * tq, 1), jnp.float32),
            pltpu.VMEM((SUBLANES, LANES), jnp.float32),
        ],
        compiler_params=_params("parallel", "arbitrary"),
        name="dsa",
    )(q, qi, wi, kx, vx, kix)


def _rope_tables(seq):
    half = HEAD_DIM // 2
    inv_freq = ROPE_THETA ** (-jnp.arange(half, dtype=jnp.float32) * 2.0 / HEAD_DIM)
    ang = jnp.arange(seq, dtype=jnp.float32)[:, None] * inv_freq[None, :]
    cos = jnp.concatenate([jnp.cos(ang)] * 4, axis=-1)
    sin = jnp.concatenate([-jnp.sin(ang), jnp.sin(ang)] * 2, axis=-1)
    return cos, sin


def _att_weights(w_in, d_model):
    n_heads = d_model // HEAD_DIM
    n_kv = n_heads // KV_GROUP
    n_idx = max(4, d_model // 128)
    n_q, n_qi = n_heads * HEAD_DIM, n_idx * IDX_DIM
    cols = n_q + 2 * n_kv * HEAD_DIM + n_qi + IDX_DIM + n_idx
    assert w_in.shape[1] == cols and IDX_DIM + n_idx <= LANES and n_kv % 2 == 0
    padded = -(-cols // LANES) * LANES
    w_cat = jnp.pad(w_in, ((0, 0), (0, padded - cols))).astype(jnp.bfloat16)
    return w_cat, n_q, n_kv, n_qi, n_idx


def kernel(x, norm_g, mlp_w1, mlp_w2, ssm_w_in, ssm_lam_re, ssm_lam_im, ssm_log_dt, ssm_b_re, ssm_b_im,
           ssm_c_re, ssm_c_im, ssm_d, ssm_w_glu, ssm_w_out, att_w_in, att_w_out):
    bsz, seq, d_model = x.shape
    depth = norm_g.shape[0]
    bf = jnp.bfloat16
    h = x.reshape(bsz * seq, d_model)
    cos, sin = _rope_tables(seq)
    k_sel = min(TOPK_MAX, seq // 4)
    for i in range(depth):
        j = i // 2
        g = norm_g[i][:, None, :]
        if i % 2 == 0:
            ar, ai, bmat, cmat, dvec = _ssm_discretise(
                ssm_lam_re[j], ssm_lam_im[j], ssm_log_dt[j], ssm_b_re[j], ssm_b_im[j],
                ssm_c_re[j], ssm_c_im[j], ssm_d[j], bsz)
            u = _ssm_in(h, g[0], ssm_w_in[j].astype(bf), bsz, seq)
            mixed = _ssm_scan(u, ar, ai, bmat, cmat, dvec, bsz)
            w_glu, w_out = ssm_w_glu[j].astype(bf), ssm_w_out[j].astype(bf)
        else:
            w_cat, n_q, n_kv, n_qi, n_idx = _att_weights(att_w_in[j], d_model)
            q, kx, vx, qi, kix, wi = _att_in(h, g[0], w_cat, cos, sin, bsz, seq, n_q, n_kv, n_qi,
                                             HEAD_DIM ** -0.5 * LOG2_E, n_idx ** -0.5 * IDX_DIM ** -0.5)
            mixed = _dsa(q, qi, wi, kx, vx, kix, bsz, seq, k_sel)
            w_glu, w_out = None, att_w_out[j].astype(bf)
        h = _mix_mlp(mixed, h, norm_g[i][1:], w_glu, w_out, mlp_w1[i].astype(bf), mlp_w2[i].astype(bf),
                     seq, time_major=(i % 2 == 0))
    return h.reshape(bsz, seq, d_model)
```

```python
import functools
import math

import jax
import jax.numpy as jnp
from jax import lax
from jax.experimental import pallas as pl
from jax.experimental.pallas import tpu as pltpu

NORM_EPS = 1e-6
SSM_CH = 16
SSM_STATE = 64
DT_MIN, DT_MAX = 1e-3, 1e-1
HEAD_DIM = 64
KV_GROUP = 4
IDX_DIM = 64
TOPK_MAX = 256
ROPE_THETA = 10000.0

LANES = 128
SUBLANES = 8
MXU_DIM = 256
VMEM_LIMIT_BYTES = 56 * 1024 * 1024

TOKEN_TILE = 512
MLP_TOKEN_TILE = 1024
MLP_SUB_TILE = 512
FF_TILE = 1024
SCAN_CHUNK = 64
SCAN_LANES = 1024
SCAN_UNROLL = 2
Q_TILE = 256
K_TILE = 512
COARSE_ITERS = 9
RESIDUAL_ITERS = 8
BF16_ULP = 2.0 ** -7
F32_SLACK = 2.0 ** -21
FOLD_ROWS = 64
LOG2_E = math.log2(math.e)
BOUND_SLACK = 1.01
MIN_DENOMINATOR = 2.0 ** -60


def _params(*sem):
    return pltpu.CompilerParams(dimension_semantics=sem, vmem_limit_bytes=VMEM_LIMIT_BYTES)


def _rms(x, g):
    return x * lax.rsqrt(jnp.mean(x * x, axis=-1, keepdims=True) + NORM_EPS) * g


def _dot(a, b):
    return jnp.dot(a, b, preferred_element_type=jnp.float32)


def _dot_nt(a, b):
    return lax.dot_general(a, b, (((1,), (1,)), ((), ())), preferred_element_type=jnp.float32)


def _mix_mlp_kernel(x_ref, h_ref, g_ref, *rest, gated, sub, tf):
    if gated:
        wg_ref, wo_ref, w1_ref, w2_ref, o_ref = rest
    else:
        wo_ref, w1_ref, w2_ref, o_ref = rest
    g = g_ref[...]
    ff = w1_ref.shape[1]
    for r in range(h_ref.shape[0] // sub):
        rows = slice(r * sub, (r + 1) * sub)
        x = x_ref[rows, :]
        if gated:
            x = x * jax.nn.sigmoid(_dot(x.astype(jnp.bfloat16), wg_ref[...]))
        h = h_ref[rows, :] + _rms(_dot(x.astype(jnp.bfloat16), wo_ref[...]), g[0:1])
        xn = _rms(h, g[1:2]).astype(jnp.bfloat16)
        acc = None
        for c in range(ff // tf):
            cols = slice(c * tf, (c + 1) * tf)
            a = jnp.maximum(_dot(xn, w1_ref[:, cols]), 0.0)
            part = _dot((a * a).astype(jnp.bfloat16), w2_ref[cols, :])
            acc = part if acc is None else acc + part
        o_ref[rows, :] = h + _rms(acc, g[2:3])


def _mix_mlp(x, h, gains, w_glu, w_out, w1, w2, seq, time_major):
    t, d = h.shape
    ff = w1.shape[1]
    n = w_out.shape[0]
    tm = min(MLP_TOKEN_TILE, seq)
    nt = seq // tm
    gated = w_glu is not None
    kern = functools.partial(_mix_mlp_kernel, gated=gated, sub=min(MLP_SUB_TILE, tm), tf=min(FF_TILE, ff))
    resident = dict(pipeline_mode=pl.Buffered(1))
    whole = lambda a: pl.BlockSpec(a.shape, lambda i: (0, 0), **resident)
    x_map = (lambda i: (i % nt, i // nt)) if time_major else (lambda i: (i, 0))
    weights = ([w_glu] if gated else []) + [w_out, w1, w2]
    return pl.pallas_call(
        kern,
        out_shape=jax.ShapeDtypeStruct((t, d), jnp.float32),
        grid=(t // tm,),
        in_specs=[pl.BlockSpec((tm, n), x_map),
                  pl.BlockSpec((tm, d), lambda i: (i, 0)),
                  pl.BlockSpec(gains.shape, lambda i: (0, 0))] + [whole(w) for w in weights],
        out_specs=pl.BlockSpec((tm, d), lambda i: (i, 0)),
        compiler_params=_params("parallel"),
        name="mix_mlp",
    )(x, h, gains, *weights)


def _ssm_in_kernel(h_ref, g_ref, w_ref, u_ref):
    xn = _rms(h_ref[...], g_ref[...]).astype(jnp.bfloat16)
    u_ref[...] = _dot(xn, w_ref[...])


def _ssm_in(h, g, w, bsz, seq):
    d = h.shape[1]
    e = w.shape[1]
    tm = min(TOKEN_TILE, seq)
    nt = seq // tm
    return pl.pallas_call(
        _ssm_in_kernel,
        out_shape=jax.ShapeDtypeStruct((seq, bsz * e), jnp.float32),
        grid=(bsz, nt),
        in_specs=[
            pl.BlockSpec((tm, d), lambda b, i: (b * nt + i, 0)),
            pl.BlockSpec((1, d), lambda b, i: (0, 0)),
            pl.BlockSpec((d, e), lambda b, i: (0, 0)),
        ],
        out_specs=pl.BlockSpec((tm, e), lambda b, i: (i, b)),
        compiler_params=_params("parallel", "parallel"),
        name="ssm_in",
    )(h, g, w)


def _ssm_scan_kernel(u_ref, ar_ref, ai_ref, bmat_ref, cmat_ref, d_ref, z_ref, x_ref, st_ref, io_ref,
                     *, bsz, chunk, half, lanes):
    @pl.when(pl.program_id(0) == 0)
    def _():
        st_ref[...] = jnp.zeros_like(st_ref)

    e = d_ref.shape[1]
    n_blk = e // MXU_DIM
    sl = half // n_blk

    def state_lanes(j):
        return slice(j * sl, (j + 1) * sl), slice(half + j * sl, half + (j + 1) * sl)

    n_slab = e // LANES
    for b in range(bsz):
        for c in range(n_slab):
            io_ref[c, pl.ds(b, chunk, stride=bsz), :] = u_ref[:, b * e + c * LANES:b * e + (c + 1) * LANES]
    u = jnp.concatenate([io_ref[c] for c in range(n_slab)], axis=1)
    ub = u.astype(jnp.bfloat16)
    for j in range(n_blk):
        ch = slice(j * MXU_DIM, (j + 1) * MXU_DIM)
        for lanes_j in state_lanes(j):
            x_ref[:, lanes_j] = _dot(ub[:, ch], bmat_ref[ch, lanes_j])

    for c in range(half // lanes):
        re = slice(c * lanes, (c + 1) * lanes)
        im = slice(half + c * lanes, half + (c + 1) * lanes)
        ar = ar_ref[:, re]
        ai = ai_ref[:, re]

        def step(t, carry, re=re, im=im, ar=ar, ai=ai):
            xr, xi = carry
            rows = pl.ds(pl.multiple_of(t * bsz, bsz), bsz)
            nr = ar * xr - ai * xi + x_ref[rows, re]
            ni = ar * xi + ai * xr + x_ref[rows, im]
            x_ref[rows, re] = nr
            x_ref[rows, im] = ni
            return nr, ni

        xr, xi = lax.fori_loop(0, chunk, step, (st_ref[:, re], st_ref[:, im]), unroll=SCAN_UNROLL)
        st_ref[:, re] = xr
        st_ref[:, im] = xi

    for j in range(n_blk):
        ch = slice(j * MXU_DIM, (j + 1) * MXU_DIM)
        re_j, im_j = state_lanes(j)
        y = (_dot(x_ref[:, re_j].astype(jnp.bfloat16), cmat_ref[re_j, ch])
             + _dot(x_ref[:, im_j].astype(jnp.bfloat16), cmat_ref[im_j, ch])
             + d_ref[:, ch] * u[:, ch])
        z = jax.nn.gelu(y)
        for c in range(MXU_DIM // LANES):
            io_ref[j * (MXU_DIM // LANES) + c] = z[:, c * LANES:(c + 1) * LANES]
    for b in range(bsz):
        for c in range(n_slab):
            z_ref[:, b * e + c * LANES:b * e + (c + 1) * LANES] = io_ref[c, pl.ds(b, chunk, stride=bsz), :]


def _ssm_scan(u, ar, ai, bmat, cmat, dvec, bsz):
    seq = u.shape[0]
    e = u.shape[1] // bsz
    n2 = bmat.shape[1]
    half = n2 // 2
    chunk = min(SCAN_CHUNK, seq)
    lanes = min(SCAN_LANES, half)
    kern = functools.partial(_ssm_scan_kernel, bsz=bsz, chunk=chunk, half=half, lanes=lanes)
    return pl.pallas_call(
        kern,
        out_shape=jax.ShapeDtypeStruct(u.shape, jnp.float32),
        grid=(seq // chunk,),
        in_specs=[
            pl.BlockSpec((chunk, bsz * e), lambda i: (i, 0)),
            pl.BlockSpec((bsz, half), lambda i: (0, 0)),
            pl.BlockSpec((bsz, half), lambda i: (0, 0)),
            pl.BlockSpec((e, n2), lambda i: (0, 0)),
            pl.BlockSpec((n2, e), lambda i: (0, 0)),
            pl.BlockSpec((1, e), lambda i: (0, 0)),
        ],
        out_specs=pl.BlockSpec((chunk, bsz * e), lambda i: (i, 0)),
        scratch_shapes=[pltpu.VMEM((chunk * bsz, n2), jnp.float32),
                        pltpu.VMEM((bsz, n2), jnp.float32),
                        pltpu.VMEM((e // LANES, chunk * bsz, LANES), jnp.float32)],
        compiler_params=_params("arbitrary"),
        name="ssm_scan",
    )(u, ar, ai, bmat, cmat, dvec)


def _ssm_discretise(lam_re, lam_im, log_dt, b_re, b_im, c_re, c_im, d_skip, bsz):
    g, p = lam_re.shape
    c = b_re.shape[2]
    dt = jnp.exp(log_dt)[:, None]
    mag = jnp.exp(lam_re * dt)
    abar_re = mag * jnp.cos(lam_im * dt)
    abar_im = mag * jnp.sin(lam_im * dt)
    den = lam_re * lam_re + lam_im * lam_im
    nr = abar_re - 1.0
    ni = abar_im
    fr = (nr * lam_re + ni * lam_im) / den
    fi = (ni * lam_re - nr * lam_im) / den
    bbar_re = fr[..., None] * b_re - fi[..., None] * b_im
    bbar_im = fr[..., None] * b_im + fi[..., None] * b_re
    bf = jnp.bfloat16
    state_of_col = jnp.arange(g * p) % p
    spread_p = (jnp.arange(p)[:, None] == state_of_col[None, :]).astype(bf)
    same_group = (jnp.arange(g * c)[:, None] // c) == (jnp.arange(g * p)[None, :] // p)

    def bd(m):
        rows = m.transpose(0, 2, 1).reshape(g * c, p).astype(bf)
        return jnp.where(same_group, jnp.dot(rows, spread_p, preferred_element_type=jnp.float32), 0.0).astype(bf)

    def cd(m):
        cols = m.transpose(2, 0, 1).reshape(p, g * c).astype(bf)
        return jnp.where(same_group.T, jnp.dot(spread_p.T, cols, preferred_element_type=jnp.float32), 0.0).astype(bf)

    bmat = jnp.concatenate([bd(bbar_re), bd(bbar_im)], axis=1)
    cmat = jnp.concatenate([cd(c_re), cd(-c_im)], axis=0)
    ar = jnp.broadcast_to(abar_re.reshape(1, g * p), (bsz, g * p))
    ai = jnp.broadcast_to(abar_im.reshape(1, g * p), (bsz, g * p))
    return ar, ai, bmat, cmat, d_skip.reshape(1, g * c)


def _rope128(x, cos, sin_signed, lane):
    swapped = jnp.where((lane % HEAD_DIM) < HEAD_DIM // 2,
                        pltpu.roll(x, LANES - HEAD_DIM // 2, axis=1),
                        pltpu.roll(x, HEAD_DIM // 2, axis=1))
    return x * cos + swapped * sin_signed


def _att_in_kernel(h_ref, g_ref, w_ref, cos_ref, sin_ref, q_ref, k_ref, v_ref, qi_ref, ki_ref, wi_ref,
                   *, n_q, n_kv, n_qi, q_scale, wi_scale):
    xn = _rms(h_ref[...], g_ref[...]).astype(jnp.bfloat16)
    proj = _dot(xn, w_ref[...])
    cos = cos_ref[...]
    sin = sin_ref[...]
    lane = lax.broadcasted_iota(jnp.int32, cos.shape, 1)
    is_head = lane < HEAD_DIM

    def chunk(off, c):
        return proj[:, off + c * LANES: off + (c + 1) * LANES]

    def head_of(x, n):
        return x if n % 2 == 0 else pltpu.roll(x, HEAD_DIM, axis=1)

    for c in range(n_q // LANES):
        q_ref[:, c * LANES:(c + 1) * LANES] = (_rope128(chunk(0, c), cos, sin, lane) * q_scale).astype(q_ref.dtype)
    off = n_q
    for n in range(n_kv):
        k2 = _rope128(chunk(off, n // 2), cos, sin, lane)
        kx = jnp.where(is_head, head_of(k2, n), jnp.where(lane == HEAD_DIM, 1.0, 0.0))
        k_ref[:, n * LANES:(n + 1) * LANES] = kx.astype(k_ref.dtype)
    off += n_kv * HEAD_DIM
    for n in range(n_kv):
        vx = jnp.where(is_head, head_of(chunk(off, n // 2), n), 1.0)
        v_ref[:, n * LANES:(n + 1) * LANES] = vx.astype(v_ref.dtype)
    off += n_kv * HEAD_DIM
    for c in range(n_qi // LANES):
        qi_ref[:, c * LANES:(c + 1) * LANES] = _rope128(chunk(off, c), cos, sin, lane).astype(qi_ref.dtype)
    off += n_qi
    last = chunk(off, 0)
    ki_lo = jnp.where(is_head, _rope128(last, cos, sin, lane), 0.0)
    ki_ref[:, :LANES] = ki_lo.astype(ki_ref.dtype)
    ki_ref[:, LANES:] = pltpu.roll(ki_lo, HEAD_DIM, axis=1).astype(ki_ref.dtype)
    wi_ref[...] = pltpu.roll(last, HEAD_DIM, axis=1) * wi_scale


def _att_in(h, g, w, cos, sin, bsz, seq, n_q, n_kv, n_qi, q_scale, wi_scale):
    d = h.shape[1]
    n_k = n_v = n_kv * LANES
    n_ki = 2 * LANES
    ncols = w.shape[1]
    tm = min(TOKEN_TILE, seq)
    nt = seq // tm
    t = bsz * seq
    row = lambda b, i: (b * nt + i, 0)
    kern = functools.partial(_att_in_kernel, n_q=n_q, n_kv=n_kv, n_qi=n_qi, q_scale=q_scale, wi_scale=wi_scale)
    bf = jnp.bfloat16
    return pl.pallas_call(
        kern,
        out_shape=[jax.ShapeDtypeStruct((t, n_q), bf), jax.ShapeDtypeStruct((t, n_k), bf),
                   jax.ShapeDtypeStruct((t, n_v), bf), jax.ShapeDtypeStruct((t, n_qi), bf),
                   jax.ShapeDtypeStruct((t, n_ki), bf), jax.ShapeDtypeStruct((t, LANES), jnp.float32)],
        grid=(bsz, nt),
        in_specs=[
            pl.BlockSpec((tm, d), row),
            pl.BlockSpec((1, d), lambda b, i: (0, 0)),
            pl.BlockSpec((d, ncols), lambda b, i: (0, 0)),
            pl.BlockSpec((tm, LANES), lambda b, i: (i, 0)),
            pl.BlockSpec((tm, LANES), lambda b, i: (i, 0)),
        ],
        out_specs=[pl.BlockSpec((tm, n_q), row), pl.BlockSpec((tm, n_k), row), pl.BlockSpec((tm, n_v), row),
                   pl.BlockSpec((tm, n_qi), row), pl.BlockSpec((tm, n_ki), row), pl.BlockSpec((tm, LANES), row)],
        compiler_params=_params("parallel", "parallel"),
        name="att_in",
    )(h, g, w, cos, sin)


def _lane_fold(x, op):
    out = x[:, :LANES]
    for j in range(1, x.shape[1] // LANES):
        out = op(out, x[:, j * LANES:(j + 1) * LANES])
    return out


def _row_fold(x, op):
    out = x[:FOLD_ROWS]
    for j in range(1, x.shape[0] // FOLD_ROWS):
        out = op(out, x[j * FOLD_ROWS:(j + 1) * FOLD_ROWS])
    return out


def _dsa_kernel(q_ref, qi_ref, wi_ref, k_ref, v_ref, ki_ref, o_ref,
                sc_ref, sct_ref, sct16_ref, wb_ref, qx_ref, acc_ref, m_ref, kn_ref,
                *, tq, tk, k_sel, n_heads, n_idx):
    i = pl.program_id(1)
    n_kv = n_heads // KV_GROUP
    lane = lax.broadcasted_iota(jnp.int32, (tq, LANES), 1)
    is_head = lane < HEAD_DIM

    @pl.when(i == 0)
    def _():
        lane_k = lax.broadcasted_iota(jnp.int32, (tk, LANES), 1)
        head_lane = lax.broadcasted_iota(jnp.int32, kn_ref.shape, 1)
        kn = jnp.zeros(kn_ref.shape, jnp.float32)
        for n in range(n_kv):
            def body(r, c, n=n):
                x = k_ref[pl.ds(pl.multiple_of(r * tk, tk), tk), n * LANES:(n + 1) * LANES].astype(jnp.float32)
                x = jnp.where(lane_k < HEAD_DIM, x, 0.0)
                ss = jnp.sum(x * x, axis=1, keepdims=True)
                return jnp.maximum(c, jnp.max(ss, axis=0, keepdims=True))
            kmax = lax.fori_loop(0, k_ref.shape[0] // tk, body, jnp.zeros((1, 1), jnp.float32))
            kn = jnp.where(head_lane // KV_GROUP == n, kmax, kn)
        kn_ref[...] = kn

    n_kt = (i * tq + tq + tk - 1) // tk
    neg_inf = jnp.float32(-jnp.inf)
    row = i * tq + lax.broadcasted_iota(jnp.int32, (tq, tk), 0)
    col0 = lax.broadcasted_iota(jnp.int32, (tq, tk), 1)
    wi = wi_ref[...]
    for hh in range(n_idx):
        wb_ref[hh] = jnp.broadcast_to(wi[:, hh:hh + 1], (tq, LANES))

    def score_tile(kt, carry):
        rmax, rmin = carry
        ks = pl.ds(pl.multiple_of(kt * tk, tk), tk)
        acc = jnp.zeros((tq, tk), jnp.float32)
        for hh in range(n_idx):
            qc = qi_ref[:, (hh // 2) * LANES:(hh // 2 + 1) * LANES]
            kc = ki_ref[ks, (hh % 2) * LANES:(hh % 2 + 1) * LANES]
            acc = acc + jnp.maximum(_dot_nt(qc, kc), 0.0) * jnp.concatenate([wb_ref[hh]] * (tk // LANES), axis=1)
        causal = (col0 + kt * tk) <= row
        masked = jnp.where(causal, acc, neg_inf)
        sc_ref[kt] = masked
        masked_t = masked.T
        sct_ref[kt] = masked_t
        sct16_ref[kt] = masked_t.astype(jnp.bfloat16)
        rmax = jnp.maximum(rmax, _row_fold(masked_t, jnp.maximum))
        rmin = jnp.minimum(rmin, _row_fold(jnp.where(masked_t == neg_inf, -neg_inf, masked_t), jnp.minimum))
        return rmax, rmin

    rmax, rmin = lax.fori_loop(0, n_kt, score_tile,
                               (jnp.full((FOLD_ROWS, tq), neg_inf), jnp.full((FOLD_ROWS, tq), -neg_inf)))

    kf = jnp.float32(k_sel)

    def sweep(fn, init, x):
        def body(kt, c):
            for j in range(tk // FOLD_ROWS):
                c = fn(c, sct_ref[kt, j * FOLD_ROWS:(j + 1) * FOLD_ROWS, :], x)
            return c
        return lax.fori_loop(0, n_kt, body, tuple(jnp.full((FOLD_ROWS, tq), v, jnp.float32) for v in init))

    def col_sum(part):
        return jnp.sum(part, axis=0, keepdims=True)

    def col_max(part):
        return jnp.max(part, axis=0, keepdims=True)

    def count_ge_coarse(x16):
        one, zero = jnp.bfloat16(1), jnp.bfloat16(0)

        def body(kt, c):
            for j in range(tk // FOLD_ROWS):
                c = c + jnp.where(sct16_ref[kt, j * FOLD_ROWS:(j + 1) * FOLD_ROWS, :] >= x16, one, zero)
            return c
        part = lax.fori_loop(0, n_kt, body, jnp.zeros((FOLD_ROWS, tq), jnp.bfloat16))
        return col_sum(part.astype(jnp.float32))

    def bisect_coarse(_, carry):
        lo, hi = carry
        mid16 = (0.5 * lo + 0.5 * hi).astype(jnp.bfloat16)
        mid = mid16.astype(jnp.float32)
        ge = count_ge_coarse(mid16) >= kf
        return jnp.where(ge, mid, lo), jnp.where(ge, hi, mid)

    rmin = -col_max(-rmin)
    rmax = col_max(rmax)
    lo, hi = lax.fori_loop(0, COARSE_ITERS, bisect_coarse, (rmin, rmax))
    lo = lo - (jnp.abs(lo) * BF16_ULP + jnp.float32(1e-30))

    def rebase(kt, c):
        for j in range(tk // FOLD_ROWS):
            rows = slice(j * FOLD_ROWS, (j + 1) * FOLD_ROWS)
            sct16_ref[kt, rows, :] = (sct_ref[kt, rows, :] - lo).astype(jnp.bfloat16)
        return c

    lax.fori_loop(0, n_kt, rebase, 0)
    width = (hi - lo) * (1.0 + BF16_ULP)
    dlo, dhi = lax.fori_loop(0, RESIDUAL_ITERS, bisect_coarse, (jnp.zeros_like(width), width))
    slack = (jnp.abs(lo) + dhi) * F32_SLACK
    hi = jnp.where(dhi < width, jnp.minimum(hi, lo + dhi + slack), hi)

    few = (i * tq + lax.broadcasted_iota(jnp.int32, (1, tq), 1)) < k_sel

    def max_le(x):
        (c,) = sweep(lambda c, s, x: (jnp.maximum(c[0], jnp.where(s <= x, s, neg_inf)),), (-jnp.inf,), x)
        return col_max(c)

    def probe(x):
        cnt, nxt = sweep(lambda c, s, x: (c[0] + jnp.where(s >= x, 1.0, 0.0),
                                          jnp.maximum(c[1], jnp.where(s < x, s, neg_inf))),
                         (0.0, -jnp.inf), x)
        return col_sum(cnt), col_max(nxt)

    cand0 = jnp.where(few, rmin, max_le(hi))
    cnt0, nxt0 = probe(cand0)

    def unresolved(cnt):
        return jnp.logical_and(jnp.logical_not(few), cnt < kf)

    def finish_cond(c):
        _, cnt, _ = c
        return jnp.max(jnp.where(unresolved(cnt), 1.0, 0.0)) > 0.0

    def finish_body(c):
        cand, cnt, nxt = c
        cand = jnp.where(unresolved(cnt), nxt, cand)
        cnt, nxt = probe(cand)
        return cand, cnt, nxt

    thr_t, cnt_ge, _ = lax.while_loop(finish_cond, finish_body, (cand0, cnt0, nxt0))

    def to_rows(x):
        return jnp.broadcast_to(x, (LANES, tq)).T

    thr = to_rows(thr_t)

    tied_t = jnp.logical_and(jnp.logical_not(few), cnt_ge > kf)

    @pl.when(jnp.max(jnp.where(tied_t, 1.0, 0.0)) > 0.0)
    def _():
        (c,) = sweep(lambda c, s, x: (c[0] + jnp.where(s > x, 1.0, 0.0),), (0.0,), thr_t)
        need = to_rows(kf - col_sum(c))[:, :1]
        thr1 = thr[:, :1]
        tied1 = to_rows(jnp.where(tied_t, 1.0, 0.0))[:, :1] > 0.0
        tri = (lax.broadcasted_iota(jnp.int32, (tk, tk), 0)
               <= lax.broadcasted_iota(jnp.int32, (tk, tk), 1)).astype(jnp.bfloat16)

        def drop_body(kt, run):
            s = sc_ref[kt]
            eq = jnp.logical_and(s == thr1, tied1)
            eqf = jnp.where(eq, 1.0, 0.0)
            rank = run + _dot(eqf.astype(jnp.bfloat16), tri)
            sc_ref[kt] = jnp.where(jnp.logical_and(eq, rank > need), neg_inf, s)
            return run + jnp.sum(eqf, axis=1, keepdims=True)

        lax.fori_loop(0, n_kt, drop_body, jnp.zeros((tq, 1), jnp.float32))

    qf = q_ref[...].astype(jnp.float32)
    n_q = qf.shape[1]
    head_of_col = lax.broadcasted_iota(jnp.int32, (n_q, LANES), 0) // HEAD_DIM
    head_sel = jnp.where(head_of_col == lax.broadcasted_iota(jnp.int32, (n_q, LANES), 1), 1.0, 0.0)
    qss = _dot((qf * qf).astype(jnp.bfloat16), head_sel.astype(jnp.bfloat16))
    bound = jnp.sqrt(jnp.max(qss * kn_ref[0:1, :], axis=1, keepdims=True)) * BOUND_SLACK
    shift = jnp.where(lane == HEAD_DIM, -bound, 0.0)
    for c in range(n_heads // 2):
        qc = qf[:, c * LANES:(c + 1) * LANES]
        for half in range(2):
            h = 2 * c + half
            n, g = h // KV_GROUP, h % KV_GROUP
            x = qc if half == 0 else pltpu.roll(qc, HEAD_DIM, axis=1)
            qx_ref[n, g * tq:(g + 1) * tq, :] = jnp.where(is_head, x, shift).astype(qx_ref.dtype)

    thr_tile = jnp.concatenate([thr] * (tk // LANES), axis=1)

    def attend(online):
        acc_ref[...] = jnp.zeros_like(acc_ref)
        if online:
            m_ref[...] = jnp.full_like(m_ref, neg_inf)

        def att_tile(kt, carry):
            ks = pl.ds(pl.multiple_of(kt * tk, tk), tk)
            keep = sc_ref[kt] >= thr_tile
            keep_b = jnp.where(keep, 1.0, 0.0).astype(jnp.bfloat16)
            for n in range(n_kv):
                s = _dot_nt(qx_ref[n], k_ref[ks, n * LANES:(n + 1) * LANES])
                vc = v_ref[ks, n * LANES:(n + 1) * LANES]
                if online:
                    s = jnp.where(jnp.concatenate([keep] * KV_GROUP, axis=0), s, neg_inf)
                    m_old = m_ref[n]
                    m_new = jnp.maximum(m_old, jnp.max(s, axis=1, keepdims=True))
                    m_safe = jnp.where(m_new == neg_inf, 0.0, m_new)
                    p = jnp.exp2(s - m_safe).astype(jnp.bfloat16)
                    acc_ref[n] = jnp.exp2(m_old - m_safe) * acc_ref[n] + _dot(p, vc)
                    m_ref[n] = m_new
                else:
                    p = jnp.exp2(s).astype(jnp.bfloat16).reshape(KV_GROUP, tq, tk) * keep_b[None]
                    acc_ref[n] += _dot(p.reshape(KV_GROUP * tq, tk), vc)
            return carry

        lax.fori_loop(0, n_kt, att_tile, 0)

    attend(False)
    den = acc_ref[0]
    for n in range(1, n_kv):
        den = jnp.minimum(den, acc_ref[n])
    den_lane = lax.broadcasted_iota(jnp.int32, den.shape, 1) >= HEAD_DIM
    safe = jnp.min(jnp.where(den_lane, den, 1.0)) > MIN_DENOMINATOR

    @pl.when(jnp.logical_not(safe))
    def _():
        attend(True)

    for c in range(n_heads // 2):
        parts = []
        for half in range(2):
            h = 2 * c + half
            n, g = h // KV_GROUP, h % KV_GROUP
            a = acc_ref[n, g * tq:(g + 1) * tq, :]
            parts.append(a / pltpu.roll(a, HEAD_DIM, axis=1))
        out = jnp.where(is_head, parts[0], pltpu.roll(parts[1], HEAD_DIM, axis=1))
        o_ref[:, c * LANES:(c + 1) * LANES] = out.astype(o_ref.dtype)


def _dsa(q, qi, wi, kx, vx, kix, bsz, seq, k_sel):
    t, n_q = q.shape
    n_heads = n_q // HEAD_DIM
    n_kv = n_heads // KV_GROUP
    n_idx = qi.shape[1] // IDX_DIM
    tq = min(Q_TILE, seq)
    tk = min(K_TILE, seq)
    nq = seq // tq
    row = lambda b, i: (b * nq + i, 0)
    per_batch = lambda b, i: (b, 0)
    kern = functools.partial(_dsa_kernel, tq=tq, tk=tk, k_sel=k_sel, n_heads=n_heads, n_idx=n_idx)
    return pl.pallas_call(
        kern,
        out_shape=jax.ShapeDtypeStruct((t, n_q), jnp.bfloat16),
        grid=(bsz, nq),
        in_specs=[
            pl.BlockSpec((tq, n_q), row),
            pl.BlockSpec((tq, qi.shape[1]), row),
            pl.BlockSpec((tq, LANES), row),
            pl.BlockSpec((seq, kx.shape[1]), per_batch),
            pl.BlockSpec((seq, vx.shape[1]), per_batch),
            pl.BlockSpec((seq, kix.shape[1]), per_batch),
        ],
        out_specs=pl.BlockSpec((tq, n_q), row),
        scratch_shapes=[
            pltpu.VMEM((seq // tk, tq, tk), jnp.float32),
            pltpu.VMEM((seq // tk, tk, tq), jnp.float32),
            pltpu.VMEM((seq // tk, tk, tq), jnp.bfloat16),
            pltpu.VMEM((n_idx, tq, LANES), jnp.float32),
            pltpu.VMEM((n_kv, KV_GROUP * tq, LANES), jnp.bfloat16),
            pltpu.VMEM((n_kv, KV_GROUP * tq, LANES), jnp.float32),
            pltpu.VMEM((n_kv, KV_GROUP * tq, 1), jnp.float32),
            pltpu.VMEM((SUBLANES, LANES), jnp.float32),
        ],
        compiler_params=_params("parallel", "arbitrary"),
        name="dsa",
    )(q, qi, wi, kx, vx, kix)


def _rope_tables(seq):
    half = HEAD_DIM // 2
    inv_freq = ROPE_THETA ** (-jnp.arange(half, dtype=jnp.float32) * 2.0 / HEAD_DIM)
    ang = jnp.arange(seq, dtype=jnp.float32)[:, None] * inv_freq[None, :]
    cos = jnp.concatenate([jnp.cos(ang)] * 4, axis=-1)
    sin = jnp.concatenate([-jnp.sin(ang), jnp.sin(ang)] * 2, axis=-1)
    return cos, sin


def _att_weights(w_in, d_model):
    n_heads = d_model // HEAD_DIM
    n_kv = n_heads // KV_GROUP
    n_idx = max(4, d_model // 128)
    n_q, n_qi = n_heads * HEAD_DIM, n_idx * IDX_DIM
    cols = n_q + 2 * n_kv * HEAD_DIM + n_qi + IDX_DIM + n_idx
    assert w_in.shape[1] == cols and IDX_DIM + n_idx <= LANES and n_kv % 2 == 0
    padded = -(-cols // LANES) * LANES
    w_cat = jnp.pad(w_in, ((0, 0), (0, padded - cols))).astype(jnp.bfloat16)
    return w_cat, n_q, n_kv, n_qi, n_idx


def kernel(x, norm_g, mlp_w1, mlp_w2, ssm_w_in, ssm_lam_re, ssm_lam_im, ssm_log_dt, ssm_b_re, ssm_b_im,
           ssm_c_re, ssm_c_im, ssm_d, ssm_w_glu, ssm_w_out, att_w_in, att_w_out):
    bsz, seq, d_model = x.shape
    depth = norm_g.shape[0]
    bf = jnp.bfloat16
    h = x.reshape(bsz * seq, d_model)
    cos, sin = _rope_tables(seq)
    k_sel = min(TOPK_MAX, seq // 4)
    for i in range(depth):
        j = i // 2
        g = norm_g[i][:, None, :]
        if i % 2 == 0:
            ar, ai, bmat, cmat, dvec = _ssm_discretise(
                ssm_lam_re[j], ssm_lam_im[j], ssm_log_dt[j], ssm_b_re[j], ssm_b_im[j],
                ssm_c_re[j], ssm_c_im[j], ssm_d[j], bsz)
            u = _ssm_in(h, g[0], ssm_w_in[j].astype(bf), bsz, seq)
            mixed = _ssm_scan(u, ar, ai, bmat, cmat, dvec, bsz)
            w_glu, w_out = ssm_w_glu[j].astype(bf), ssm_w_out[j].astype(bf)
        else:
            w_cat, n_q, n_kv, n_qi, n_idx = _att_weights(att_w_in[j], d_model)
            q, kx, vx, qi, kix, wi = _att_in(h, g[0], w_cat, cos, sin, bsz, seq, n_q, n_kv, n_qi,
                                             HEAD_DIM ** -0.5 * LOG2_E, n_idx ** -0.5 * IDX_DIM ** -0.5)
            mixed = _dsa(q, qi, wi, kx, vx, kix, bsz, seq, k_sel)
            w_glu, w_out = None, att_w_out[j].astype(bf)
        h = _mix_mlp(mixed, h, norm_g[i][1:], w_glu, w_out, mlp_w1[i].astype(bf), mlp_w2[i].astype(bf),
                     seq, time_major=(i % 2 == 0))
    return h.reshape(bsz, seq, d_model)
```

```python
import functools
import math

import jax
import jax.numpy as jnp
from jax import lax
from jax.experimental import pallas as pl
from jax.experimental.pallas import tpu as pltpu

NORM_EPS = 1e-6
SSM_CH = 16
SSM_STATE = 64
DT_MIN, DT_MAX = 1e-3, 1e-1
HEAD_DIM = 64
KV_GROUP = 4
IDX_DIM = 64
TOPK_MAX = 256
ROPE_THETA = 10000.0

LANES = 128
SUBLANES = 8
MXU_DIM = 256
VMEM_LIMIT_BYTES = 56 * 1024 * 1024

TOKEN_TILE = 512
MLP_TOKEN_TILE = 1024
MLP_SUB_TILE = 512
FF_TILE = 1024
SCAN_CHUNK = 64
SCAN_LANES = 1024
SCAN_UNROLL = 2
Q_TILE = 256
K_TILE = 512
COARSE_ITERS = 9
RESIDUAL_ITERS = 8
BF16_ULP = 2.0 ** -7
F32_SLACK = 2.0 ** -21
FOLD_ROWS = 64
LOG2_E = math.log2(math.e)
BOUND_SLACK = 1.01
MIN_DENOMINATOR = 2.0 ** -60


def _params(*sem):
    return pltpu.CompilerParams(dimension_semantics=sem, vmem_limit_bytes=VMEM_LIMIT_BYTES)


def _rms(x, g):
    return x * lax.rsqrt(jnp.mean(x * x, axis=-1, keepdims=True) + NORM_EPS) * g


def _dot(a, b):
    return jnp.dot(a, b, preferred_element_type=jnp.float32)


def _dot_nt(a, b):
    return lax.dot_general(a, b, (((1,), (1,)), ((), ())), preferred_element_type=jnp.float32)


def _mix_mlp_kernel(x_ref, h_ref, g_ref, *rest, gated, sub, tf):
    if gated:
        wg_ref, wo_ref, w1_ref, w2_ref, o_ref = rest
    else:
        wo_ref, w1_ref, w2_ref, o_ref = rest
    g = g_ref[...]
    ff = w1_ref.shape[1]
    for r in range(h_ref.shape[0] // sub):
        rows = slice(r * sub, (r + 1) * sub)
        x = x_ref[rows, :]
        if gated:
            x = x * jax.nn.sigmoid(_dot(x.astype(jnp.bfloat16), wg_ref[...]))
        h = h_ref[rows, :] + _rms(_dot(x.astype(jnp.bfloat16), wo_ref[...]), g[0:1])
        xn = _rms(h, g[1:2]).astype(jnp.bfloat16)
        acc = None
        for c in range(ff // tf):
            cols = slice(c * tf, (c + 1) * tf)
            a = jnp.maximum(_dot(xn, w1_ref[:, cols]), 0.0)
            part = _dot((a * a).astype(jnp.bfloat16), w2_ref[cols, :])
            acc = part if acc is None else acc + part
        o_ref[rows, :] = h + _rms(acc, g[2:3])


def _mix_mlp(x, h, gains, w_glu, w_out, w1, w2, seq, time_major):
    t, d = h.shape
    ff = w1.shape[1]
    n = w_out.shape[0]
    tm = min(MLP_TOKEN_TILE, seq)
    nt = seq // tm
    gated = w_glu is not None
    kern = functools.partial(_mix_mlp_kernel, gated=gated, sub=min(MLP_SUB_TILE, tm), tf=min(FF_TILE, ff))
    resident = dict(pipeline_mode=pl.Buffered(1))
    whole = lambda a: pl.BlockSpec(a.shape, lambda i: (0, 0), **resident)
    x_map = (lambda i: (i % nt, i // nt)) if time_major else (lambda i: (i, 0))
    weights = ([w_glu] if gated else []) + [w_out, w1, w2]
    return pl.pallas_call(
        kern,
        out_shape=jax.ShapeDtypeStruct((t, d), jnp.float32),
        grid=(t // tm,),
        in_specs=[pl.BlockSpec((tm, n), x_map),
                  pl.BlockSpec((tm, d), lambda i: (i, 0)),
                  pl.BlockSpec(gains.shape, lambda i: (0, 0))] + [whole(w) for w in weights],
        out_specs=pl.BlockSpec((tm, d), lambda i: (i, 0)),
        compiler_params=_params("parallel"),
        name="mix_mlp",
    )(x, h, gains, *weights)


def _ssm_in_kernel(h_ref, g_ref, w_ref, u_ref):
    xn = _rms(h_ref[...], g_ref[...]).astype(jnp.bfloat16)
    u_ref[...] = _dot(xn, w_ref[...])


def _ssm_in(h, g, w, bsz, seq):
    d = h.shape[1]
    e = w.shape[1]
    tm = min(TOKEN_TILE, seq)
    nt = seq // tm
    return pl.pallas_call(
        _ssm_in_kernel,
        out_shape=jax.ShapeDtypeStruct((seq, bsz * e), jnp.float32),
        grid=(bsz, nt),
        in_specs=[
            pl.BlockSpec((tm, d), lambda b, i: (b * nt + i, 0)),
            pl.BlockSpec((1, d), lambda b, i: (0, 0)),
            pl.BlockSpec((d, e), lambda b, i: (0, 0)),
        ],
        out_specs=pl.BlockSpec((tm, e), lambda b, i: (i, b)),
        compiler_params=_params("parallel", "parallel"),
        name="ssm_in",
    )(h, g, w)


def _ssm_scan_kernel(u_ref, ar_ref, ai_ref, bmat_ref, cmat_ref, d_ref, z_ref, x_ref, st_ref, io_ref,
                     *, bsz, chunk, half, lanes):
    @pl.when(pl.program_id(0) == 0)
    def _():
        st_ref[...] = jnp.zeros_like(st_ref)

    e = d_ref.shape[1]
    n_blk = e // MXU_DIM
    sl = half // n_blk

    def state_lanes(j):
        return slice(j * sl, (j + 1) * sl), slice(half + j * sl, half + (j + 1) * sl)

    n_slab = e // LANES
    for b in range(bsz):
        for c in range(n_slab):
            io_ref[c, pl.ds(b, chunk, stride=bsz), :] = u_ref[:, b * e + c * LANES:b * e + (c + 1) * LANES]
    u = jnp.concatenate([io_ref[c] for c in range(n_slab)], axis=1)
    ub = u.astype(jnp.bfloat16)
    for j in range(n_blk):
        ch = slice(j * MXU_DIM, (j + 1) * MXU_DIM)
        for lanes_j in state_lanes(j):
            x_ref[:, lanes_j] = _dot(ub[:, ch], bmat_ref[ch, lanes_j])

    for c in range(half // lanes):
        re = slice(c * lanes, (c + 1) * lanes)
        im = slice(half + c * lanes, half + (c + 1) * lanes)
        ar = ar_ref[:, re]
        ai = ai_ref[:, re]

        def step(t, carry, re=re, im=im, ar=ar, ai=ai):
            xr, xi = carry
            rows = pl.ds(pl.multiple_of(t * bsz, bsz), bsz)
            nr = ar * xr - ai * xi + x_ref[rows, re]
            ni = ar * xi + ai * xr + x_ref[rows, im]
            x_ref[rows, re] = nr
            x_ref[rows, im] = ni
            return nr, ni

        xr, xi = lax.fori_loop(0, chunk, step, (st_ref[:, re], st_ref[:, im]), unroll=SCAN_UNROLL)
        st_ref[:, re] = xr
        st_ref[:, im] = xi

    for j in range(n_blk):
        ch = slice(j * MXU_DIM, (j + 1) * MXU_DIM)
        re_j, im_j = state_lanes(j)
        y = (_dot(x_ref[:, re_j].astype(jnp.bfloat16), cmat_ref[re_j, ch])
             + _dot(x_ref[:, im_j].astype(jnp.bfloat16), cmat_ref[im_j, ch])
             + d_ref[:, ch] * u[:, ch])
        z = jax.nn.gelu(y)
        for c in range(MXU_DIM // LANES):
            io_ref[j * (MXU_DIM // LANES) + c] = z[:, c * LANES:(c + 1) * LANES]
    for b in range(bsz):
        for c in range(n_slab):
            z_ref[:, b * e + c * LANES:b * e + (c + 1) * LANES] = io_ref[c, pl.ds(b, chunk, stride=bsz), :]


def _ssm_scan(u, ar, ai, bmat, cmat, dvec, bsz):
    seq = u.shape[0]
    e = u.shape[1] // bsz
    n2 = bmat.shape[1]
    half = n2 // 2
    chunk = min(SCAN_CHUNK, seq)
    lanes = min(SCAN_LANES, half)
    kern = functools.partial(_ssm_scan_kernel, bsz=bsz, chunk=chunk, half=half, lanes=lanes)
    return pl.pallas_call(
        kern,
        out_shape=jax.ShapeDtypeStruct(u.shape, jnp.float32),
        grid=(seq // chunk,),
        in_specs=[
            pl.BlockSpec((chunk, bsz * e), lambda i: (i, 0)),
            pl.BlockSpec((bsz, half), lambda i: (0, 0)),
            pl.BlockSpec((bsz, half), lambda i: (0, 0)),
            pl.BlockSpec((e, n2), lambda i: (0, 0)),
            pl.BlockSpec((n2, e), lambda i: (0, 0)),
            pl.BlockSpec((1, e), lambda i: (0, 0)),
        ],
        out_specs=pl.BlockSpec((chunk, bsz * e), lambda i: (i, 0)),
        scratch_shapes=[pltpu.VMEM((chunk * bsz, n2), jnp.float32),
                        pltpu.VMEM((bsz, n2), jnp.float32),
                        pltpu.VMEM((e // LANES, chunk * bsz, LANES), jnp.float32)],
        compiler_params=_params("arbitrary"),
        name="ssm_scan",
    )(u, ar, ai, bmat, cmat, dvec)


def _ssm_discretise(lam_re, lam_im, log_dt, b_re, b_im, c_re, c_im, d_skip, bsz):
    g, p = lam_re.shape
    c = b_re.shape[2]
    dt = jnp.exp(log_dt)[:, None]
    mag = jnp.exp(lam_re * dt)
    abar_re = mag * jnp.cos(lam_im * dt)
    abar_im = mag * jnp.sin(lam_im * dt)
    den = lam_re * lam_re + lam_im * lam_im
    nr = abar_re - 1.0
    ni = abar_im
    fr = (nr * lam_re + ni * lam_im) / den
    fi = (ni * lam_re - nr * lam_im) / den
    bbar_re = fr[..., None] * b_re - fi[..., None] * b_im
    bbar_im = fr[..., None] * b_im + fi[..., None] * b_re
    bf = jnp.bfloat16
    state_of_col = jnp.arange(g * p) % p
    spread_p = (jnp.arange(p)[:, None] == state_of_col[None, :]).astype(bf)
    same_group = (jnp.arange(g * c)[:, None] // c) == (jnp.arange(g * p)[None, :] // p)

    def bd(m):
        rows = m.transpose(0, 2, 1).reshape(g * c, p).astype(bf)
        return jnp.where(same_group, jnp.dot(rows, spread_p, preferred_element_type=jnp.float32), 0.0).astype(bf)

    def cd(m):
        cols = m.transpose(2, 0, 1).reshape(p, g * c).astype(bf)
        return jnp.where(same_group.T, jnp.dot(spread_p.T, cols, preferred_element_type=jnp.float32), 0.0).astype(bf)

    bmat = jnp.concatenate([bd(bbar_re), bd(bbar_im)], axis=1)
    cmat = jnp.concatenate([cd(c_re), cd(-c_im)], axis=0)
    ar = jnp.broadcast_to(abar_re.reshape(1, g * p), (bsz, g * p))
    ai = jnp.broadcast_to(abar_im.reshape(1, g * p), (bsz, g * p))
    return ar, ai, bmat, cmat, d_skip.reshape(1, g * c)


def _rope128(x, cos, sin_signed, lane):
    swapped = jnp.where((lane % HEAD_DIM) < HEAD_DIM // 2,
                        pltpu.roll(x, LANES - HEAD_DIM // 2, axis=1),
                        pltpu.roll(x, HEAD_DIM // 2, axis=1))
    return x * cos + swapped * sin_signed


def _att_in_kernel(h_ref, g_ref, w_ref, cos_ref, sin_ref, q_ref, k_ref, v_ref, qi_ref, ki_ref, wi_ref,
                   *, n_q, n_kv, n_qi, q_scale, wi_scale):
    xn = _rms(h_ref[...], g_ref[...]).astype(jnp.bfloat16)
    proj = _dot(xn, w_ref[...])
    cos = cos_ref[...]
    sin = sin_ref[...]
    lane = lax.broadcasted_iota(jnp.int32, cos.shape, 1)
    is_head = lane < HEAD_DIM

    def chunk(off, c):
        return proj[:, off + c * LANES: off + (c + 1) * LANES]

    def head_of(x, n):
        return x if n % 2 == 0 else pltpu.roll(x, HEAD_DIM, axis=1)

    for c in range(n_q // LANES):
        q_ref[:, c * LANES:(c + 1) * LANES] = (_rope128(chunk(0, c), cos, sin, lane) * q_scale).astype(q_ref.dtype)
    off = n_q
    for n in range(n_kv):
        k2 = _rope128(chunk(off, n // 2), cos, sin, lane)
        kx = jnp.where(is_head, head_of(k2, n), jnp.where(lane == HEAD_DIM, 1.0, 0.0))
        k_ref[:, n * LANES:(n + 1) * LANES] = kx.astype(k_ref.dtype)
    off += n_kv * HEAD_DIM
    for n in range(n_kv):
        vx = jnp.where(is_head, head_of(chunk(off, n // 2), n), 1.0)
        v_ref[:, n * LANES:(n + 1) * LANES] = vx.astype(v_ref.dtype)
    off += n_kv * HEAD_DIM
    for c in range(n_qi // LANES):
        qi_ref[:, c * LANES:(c + 1) * LANES] = _rope128(chunk(off, c), cos, sin, lane).astype(qi_ref.dtype)
    off += n_qi
    last = chunk(off, 0)
    ki_lo = jnp.where(is_head, _rope128(last, cos, sin, lane), 0.0)
    ki_ref[:, :LANES] = ki_lo.astype(ki_ref.dtype)
    ki_ref[:, LANES:] = pltpu.roll(ki_lo, HEAD_DIM, axis=1).astype(ki_ref.dtype)
    wi_ref[...] = pltpu.roll(last, HEAD_DIM, axis=1) * wi_scale


def _att_in(h, g, w, cos, sin, bsz, seq, n_q, n_kv, n_qi, q_scale, wi_scale):
    d = h.shape[1]
    n_k = n_v = n_kv * LANES
    n_ki = 2 * LANES
    ncols = w.shape[1]
    tm = min(TOKEN_TILE, seq)
    nt = seq // tm
    t = bsz * seq
    row = lambda b, i: (b * nt + i, 0)
    kern = functools.partial(_att_in_kernel, n_q=n_q, n_kv=n_kv, n_qi=n_qi, q_scale=q_scale, wi_scale=wi_scale)
    bf = jnp.bfloat16
    return pl.pallas_call(
        kern,
        out_shape=[jax.ShapeDtypeStruct((t, n_q), bf), jax.ShapeDtypeStruct((t, n_k), bf),
                   jax.ShapeDtypeStruct((t, n_v), bf), jax.ShapeDtypeStruct((t, n_qi), bf),
                   jax.ShapeDtypeStruct((t, n_ki), bf), jax.ShapeDtypeStruct((t, LANES), jnp.float32)],
        grid=(bsz, nt),
        in_specs=[
            pl.BlockSpec((tm, d), row),
            pl.BlockSpec((1, d), lambda b, i: (0, 0)),
            pl.BlockSpec((d, ncols), lambda b, i: (0, 0)),
            pl.BlockSpec((tm, LANES), lambda b, i: (i, 0)),
            pl.BlockSpec((tm, LANES), lambda b, i: (i, 0)),
        ],
        out_specs=[pl.BlockSpec((tm, n_q), row), pl.BlockSpec((tm, n_k), row), pl.BlockSpec((tm, n_v), row),
                   pl.BlockSpec((tm, n_qi), row), pl.BlockSpec((tm, n_ki), row), pl.BlockSpec((tm, LANES), row)],
        compiler_params=_params("parallel", "parallel"),
        name="att_in",
    )(h, g, w, cos, sin)


def _lane_fold(x, op):
    out = x[:, :LANES]
    for j in range(1, x.shape[1] // LANES):
        out = op(out, x[:, j * LANES:(j + 1) * LANES])
    return out


def _row_fold(x, op):
    out = x[:FOLD_ROWS]
    for j in range(1, x.shape[0] // FOLD_ROWS):
        out = op(out, x[j * FOLD_ROWS:(j + 1) * FOLD_ROWS])
    return out


def _dsa_kernel(q_ref, qi_ref, wi_ref, k_ref, v_ref, ki_ref, o_ref,
                sc_ref, sct_ref, sct16_ref, stage_ref, wb_ref, qx_ref, acc_ref, m_ref, kn_ref,
                *, tq, tk, k_sel, n_heads, n_idx):
    i = pl.program_id(1)
    n_kv = n_heads // KV_GROUP
    lane = lax.broadcasted_iota(jnp.int32, (tq, LANES), 1)
    is_head = lane < HEAD_DIM

    @pl.when(i == 0)
    def _():
        lane_k = lax.broadcasted_iota(jnp.int32, (tk, LANES), 1)
        head_lane = lax.broadcasted_iota(jnp.int32, kn_ref.shape, 1)
        kn = jnp.zeros(kn_ref.shape, jnp.float32)
        for n in range(n_kv):
            def body(r, c, n=n):
                x = k_ref[pl.ds(pl.multiple_of(r * tk, tk), tk), n * LANES:(n + 1) * LANES].astype(jnp.float32)
                x = jnp.where(lane_k < HEAD_DIM, x, 0.0)
                ss = jnp.sum(x * x, axis=1, keepdims=True)
                return jnp.maximum(c, jnp.max(ss, axis=0, keepdims=True))
            kmax = lax.fori_loop(0, k_ref.shape[0] // tk, body, jnp.zeros((1, 1), jnp.float32))
            kn = jnp.where(head_lane // KV_GROUP == n, kmax, kn)
        kn_ref[...] = kn

    n_kt = (i * tq + tq + tk - 1) // tk
    neg_inf = jnp.float32(-jnp.inf)
    row = i * tq + lax.broadcasted_iota(jnp.int32, (tq, tk), 0)
    col0 = lax.broadcasted_iota(jnp.int32, (tq, tk), 1)
    wi = wi_ref[...]
    for hh in range(n_idx):
        wb_ref[hh] = jnp.broadcast_to(wi[:, hh:hh + 1], (tq, LANES))

    def raw_scores(kt):
        ks = pl.ds(pl.multiple_of(kt * tk, tk), tk)
        acc = jnp.zeros((tq, tk), jnp.float32)
        for hh in range(n_idx):
            qc = qi_ref[:, (hh // 2) * LANES:(hh // 2 + 1) * LANES]
            kc = ki_ref[ks, (hh % 2) * LANES:(hh % 2 + 1) * LANES]
            acc = acc + jnp.maximum(_dot_nt(qc, kc), 0.0) * jnp.concatenate([wb_ref[hh]] * (tk // LANES), axis=1)
        return acc

    def mask_tile(kt, carry):
        rmax, rmin = carry
        causal = (col0 + kt * tk) <= row
        masked = jnp.where(causal, stage_ref[...], neg_inf)
        sc_ref[kt] = masked
        masked_t = masked.T
        sct_ref[kt] = masked_t
        sct16_ref[kt] = masked_t.astype(jnp.bfloat16)
        rmax = jnp.maximum(rmax, _row_fold(masked_t, jnp.maximum))
        rmin = jnp.minimum(rmin, _row_fold(jnp.where(masked_t == neg_inf, -neg_inf, masked_t), jnp.minimum))
        return rmax, rmin

    def score_step(kt, carry):
        carry = mask_tile(kt - 1, carry)
        stage_ref[...] = raw_scores(kt)
        return carry

    stage_ref[...] = raw_scores(0)
    extremes = lax.fori_loop(1, n_kt, score_step,
                             (jnp.full((FOLD_ROWS, tq), neg_inf), jnp.full((FOLD_ROWS, tq), -neg_inf)))
    rmax, rmin = mask_tile(n_kt - 1, extremes)

    kf = jnp.float32(k_sel)

    def sweep(fn, init, x):
        def body(kt, c):
            for j in range(tk // FOLD_ROWS):
                c = fn(c, sct_ref[kt, j * FOLD_ROWS:(j + 1) * FOLD_ROWS, :], x)
            return c
        return lax.fori_loop(0, n_kt, body, tuple(jnp.full((FOLD_ROWS, tq), v, jnp.float32) for v in init))

    def col_sum(part):
        return jnp.sum(part, axis=0, keepdims=True)

    def col_max(part):
        return jnp.max(part, axis=0, keepdims=True)

    def count_ge_coarse(x16):
        one, zero = jnp.bfloat16(1), jnp.bfloat16(0)

        def body(kt, c):
            for j in range(tk // FOLD_ROWS):
                c = c + jnp.where(sct16_ref[kt, j * FOLD_ROWS:(j + 1) * FOLD_ROWS, :] >= x16, one, zero)
            return c
        part = lax.fori_loop(0, n_kt, body, jnp.zeros((FOLD_ROWS, tq), jnp.bfloat16))
        return col_sum(part.astype(jnp.float32))

    def bisect_coarse(_, carry):
        lo, hi = carry
        mid16 = (0.5 * lo + 0.5 * hi).astype(jnp.bfloat16)
        mid = mid16.astype(jnp.float32)
        ge = count_ge_coarse(mid16) >= kf
        return jnp.where(ge, mid, lo), jnp.where(ge, hi, mid)

    rmin = -col_max(-rmin)
    rmax = col_max(rmax)
    lo, hi = lax.fori_loop(0, COARSE_ITERS, bisect_coarse, (rmin, rmax))
    lo = lo - (jnp.abs(lo) * BF16_ULP + jnp.float32(1e-30))

    def rebase(kt, c):
        for j in range(tk // FOLD_ROWS):
            rows = slice(j * FOLD_ROWS, (j + 1) * FOLD_ROWS)
            sct16_ref[kt, rows, :] = (sct_ref[kt, rows, :] - lo).astype(jnp.bfloat16)
        return c

    lax.fori_loop(0, n_kt, rebase, 0)
    width = (hi - lo) * (1.0 + BF16_ULP)
    dlo, dhi = lax.fori_loop(0, RESIDUAL_ITERS, bisect_coarse, (jnp.zeros_like(width), width))
    slack = (jnp.abs(lo) + dhi) * F32_SLACK
    hi = jnp.where(dhi < width, jnp.minimum(hi, lo + dhi + slack), hi)

    few = (i * tq + lax.broadcasted_iota(jnp.int32, (1, tq), 1)) < k_sel

    def max_le(x):
        (c,) = sweep(lambda c, s, x: (jnp.maximum(c[0], jnp.where(s <= x, s, neg_inf)),), (-jnp.inf,), x)
        return col_max(c)

    def probe(x):
        cnt, nxt = sweep(lambda c, s, x: (c[0] + jnp.where(s >= x, 1.0, 0.0),
                                          jnp.maximum(c[1], jnp.where(s < x, s, neg_inf))),
                         (0.0, -jnp.inf), x)
        return col_sum(cnt), col_max(nxt)

    cand0 = jnp.where(few, rmin, max_le(hi))
    cnt0, nxt0 = probe(cand0)

    def unresolved(cnt):
        return jnp.logical_and(jnp.logical_not(few), cnt < kf)

    def finish_cond(c):
        _, cnt, _ = c
        return jnp.max(jnp.where(unresolved(cnt), 1.0, 0.0)) > 0.0

    def finish_body(c):
        cand, cnt, nxt = c
        cand = jnp.where(unresolved(cnt), nxt, cand)
        cnt, nxt = probe(cand)
        return cand, cnt, nxt

    thr_t, cnt_ge, _ = lax.while_loop(finish_cond, finish_body, (cand0, cnt0, nxt0))

    def to_rows(x):
        return jnp.broadcast_to(x, (LANES, tq)).T

    thr = to_rows(thr_t)

    tied_t = jnp.logical_and(jnp.logical_not(few), cnt_ge > kf)

    @pl.when(jnp.max(jnp.where(tied_t, 1.0, 0.0)) > 0.0)
    def _():
        (c,) = sweep(lambda c, s, x: (c[0] + jnp.where(s > x, 1.0, 0.0),), (0.0,), thr_t)
        need = to_rows(kf - col_sum(c))[:, :1]
        thr1 = thr[:, :1]
        tied1 = to_rows(jnp.where(tied_t, 1.0, 0.0))[:, :1] > 0.0
        tri = (lax.broadcasted_iota(jnp.int32, (tk, tk), 0)
               <= lax.broadcasted_iota(jnp.int32, (tk, tk), 1)).astype(jnp.bfloat16)

        def drop_body(kt, run):
            s = sc_ref[kt]
            eq = jnp.logical_and(s == thr1, tied1)
            eqf = jnp.where(eq, 1.0, 0.0)
            rank = run + _dot(eqf.astype(jnp.bfloat16), tri)
            sc_ref[kt] = jnp.where(jnp.logical_and(eq, rank > need), neg_inf, s)
            return run + jnp.sum(eqf, axis=1, keepdims=True)

        lax.fori_loop(0, n_kt, drop_body, jnp.zeros((tq, 1), jnp.float32))

    qf = q_ref[...].astype(jnp.float32)
    n_q = qf.shape[1]
    head_of_col = lax.broadcasted_iota(jnp.int32, (n_q, LANES), 0) // HEAD_DIM
    head_sel = jnp.where(head_of_col == lax.broadcasted_iota(jnp.int32, (n_q, LANES), 1), 1.0, 0.0)
    qss = _dot((qf * qf).astype(jnp.bfloat16), head_sel.astype(jnp.bfloat16))
    bound = jnp.sqrt(jnp.max(qss * kn_ref[0:1, :], axis=1, keepdims=True)) * BOUND_SLACK
    shift = jnp.where(lane == HEAD_DIM, -bound, 0.0)
    for c in range(n_heads // 2):
        qc = qf[:, c * LANES:(c + 1) * LANES]
        for half in range(2):
            h = 2 * c + half
            n, g = h // KV_GROUP, h % KV_GROUP
            x = qc if half == 0 else pltpu.roll(qc, HEAD_DIM, axis=1)
            qx_ref[n, g * tq:(g + 1) * tq, :] = jnp.where(is_head, x, shift).astype(qx_ref.dtype)

    thr_tile = jnp.concatenate([thr] * (tk // LANES), axis=1)

    def attend(online):
        acc_ref[...] = jnp.zeros_like(acc_ref)
        if online:
            m_ref[...] = jnp.full_like(m_ref, neg_inf)

        def att_block(kt, width):
            ks = pl.ds(pl.multiple_of(kt * tk, tk), width)
            keep = sc_ref[kt, :, :width] >= thr_tile[:, :width]
            keep_b = jnp.where(keep, 1.0, 0.0).astype(jnp.bfloat16)
            for n in range(n_kv):
                s = _dot_nt(qx_ref[n], k_ref[ks, n * LANES:(n + 1) * LANES])
                vc = v_ref[ks, n * LANES:(n + 1) * LANES]
                if online:
                    s = jnp.where(jnp.concatenate([keep] * KV_GROUP, axis=0), s, neg_inf)
                    m_old = m_ref[n]
                    m_new = jnp.maximum(m_old, jnp.max(s, axis=1, keepdims=True))
                    m_safe = jnp.where(m_new == neg_inf, 0.0, m_new)
                    p = jnp.exp2(s - m_safe).astype(jnp.bfloat16)
                    acc_ref[n] = jnp.exp2(m_old - m_safe) * acc_ref[n] + _dot(p, vc)
                    m_ref[n] = m_new
                else:
                    p = jnp.exp2(s).astype(jnp.bfloat16).reshape(KV_GROUP, tq, width) * keep_b[None]
                    acc_ref[n] += _dot(p.reshape(KV_GROUP * tq, width), vc)

        def att_tile(kt, carry):
            att_block(kt, tk)
            return carry

        n_full = (i * tq + tq) // tk
        lax.fori_loop(0, n_full, att_tile, 0)
        rest = (i * tq + tq) % tk
        for part in range(1, tk // tq):
            @pl.when(rest == part * tq)
            def _(part=part):
                att_block(n_full, part * tq)

    attend(False)
    den = acc_ref[0]
    for n in range(1, n_kv):
        den = jnp.minimum(den, acc_ref[n])
    den_lane = lax.broadcasted_iota(jnp.int32, den.shape, 1) >= HEAD_DIM
    safe = jnp.min(jnp.where(den_lane, den, 1.0)) > MIN_DENOMINATOR

    @pl.when(jnp.logical_not(safe))
    def _():
        attend(True)

    for c in range(n_heads // 2):
        parts = []
        for half in range(2):
            h = 2 * c + half
            n, g = h // KV_GROUP, h % KV_GROUP
            a = acc_ref[n, g * tq:(g + 1) * tq, :]
            parts.append(a / pltpu.roll(a, HEAD_DIM, axis=1))
        out = jnp.where(is_head, parts[0], pltpu.roll(parts[1], HEAD_DIM, axis=1))
        o_ref[:, c * LANES:(c + 1) * LANES] = out.astype(o_ref.dtype)


def _dsa(q, qi, wi, kx, vx, kix, bsz, seq, k_sel):
    t, n_q = q.shape
    n_heads = n_q // HEAD_DIM
    n_kv = n_heads // KV_GROUP
    n_idx = qi.shape[1] // IDX_DIM
    tq = min(Q_TILE, seq)
    tk = min(K_TILE, seq)
    nq = seq // tq
    row = lambda b, i: (b * nq + i, 0)
    per_batch = lambda b, i: (b, 0)
    kern = functools.partial(_dsa_kernel, tq=tq, tk=tk, k_sel=k_sel, n_heads=n_heads, n_idx=n_idx)
    return pl.pallas_call(
        kern,
        out_shape=jax.ShapeDtypeStruct((t, n_q), jnp.bfloat16),
        grid=(bsz, nq),
        in_specs=[
            pl.BlockSpec((tq, n_q), row),
            pl.BlockSpec((tq, qi.shape[1]), row),
            pl.BlockSpec((tq, LANES), row),
            pl.BlockSpec((seq, kx.shape[1]), per_batch),
            pl.BlockSpec((seq, vx.shape[1]), per_batch),
            pl.BlockSpec((seq, kix.shape[1]), per_batch),
        ],
        out_specs=pl.BlockSpec((tq, n_q), row),
        scratch_shapes=[
            pltpu.VMEM((seq // tk, tq, tk), jnp.float32),
            pltpu.VMEM((seq // tk, tk, tq), jnp.float32),
            pltpu.VMEM((seq // tk, tk, tq), jnp.bfloat16),
            pltpu.VMEM((tq, tk), jnp.float32),
            pltpu.VMEM((n_idx, tq, LANES), jnp.float32),
            pltpu.VMEM((n_kv, KV_GROUP * tq, LANES), jnp.bfloat16),
            pltpu.VMEM((n_kv, KV_GROUP * tq, LANES), jnp.float32),
            pltpu.VMEM((n_kv, KV_GROUP * tq, 1), jnp.float32),
            pltpu.VMEM((SUBLANES, LANES), jnp.float32),
        ],
        compiler_params=_params("parallel", "arbitrary"),
        name="dsa",
    )(q, qi, wi, kx, vx, kix)


def _rope_tables(seq):
    half = HEAD_DIM // 2
    inv_freq = ROPE_THETA ** (-jnp.arange(half, dtype=jnp.float32) * 2.0 / HEAD_DIM)
    ang = jnp.arange(seq, dtype=jnp.float32)[:, None] * inv_freq[None, :]
    cos = jnp.concatenate([jnp.cos(ang)] * 4, axis=-1)
    sin = jnp.concatenate([-jnp.sin(ang), jnp.sin(ang)] * 2, axis=-1)
    return cos, sin


def _att_weights(w_in, d_model):
    n_heads = d_model // HEAD_DIM
    n_kv = n_heads // KV_GROUP
    n_idx = max(4, d_model // 128)
    n_q, n_qi = n_heads * HEAD_DIM, n_idx * IDX_DIM
    cols = n_q + 2 * n_kv * HEAD_DIM + n_qi + IDX_DIM + n_idx
    assert w_in.shape[1] == cols and IDX_DIM + n_idx <= LANES and n_kv % 2 == 0
    padded = -(-cols // LANES) * LANES
    w_cat = jnp.pad(w_in, ((0, 0), (0, padded - cols))).astype(jnp.bfloat16)
    return w_cat, n_q, n_kv, n_qi, n_idx


def kernel(x, norm_g, mlp_w1, mlp_w2, ssm_w_in, ssm_lam_re, ssm_lam_im, ssm_log_dt, ssm_b_re, ssm_b_im,
           ssm_c_re, ssm_c_im, ssm_d, ssm_w_glu, ssm_w_out, att_w_in, att_w_out):
    bsz, seq, d_model = x.shape
    depth = norm_g.shape[0]
    bf = jnp.bfloat16
    h = x.reshape(bsz * seq, d_model)
    cos, sin = _rope_tables(seq)
    k_sel = min(TOPK_MAX, seq // 4)
    for i in range(depth):
        j = i // 2
        g = norm_g[i][:, None, :]
        if i % 2 == 0:
            ar, ai, bmat, cmat, dvec = _ssm_discretise(
                ssm_lam_re[j], ssm_lam_im[j], ssm_log_dt[j], ssm_b_re[j], ssm_b_im[j],
                ssm_c_re[j], ssm_c_im[j], ssm_d[j], bsz)
            u = _ssm_in(h, g[0], ssm_w_in[j].astype(bf), bsz, seq)
            mixed = _ssm_scan(u, ar, ai, bmat, cmat, dvec, bsz)
            w_glu, w_out = ssm_w_glu[j].astype(bf), ssm_w_out[j].astype(bf)
        else:
            w_cat, n_q, n_kv, n_qi, n_idx = _att_weights(att_w_in[j], d_model)
            q, kx, vx, qi, kix, wi = _att_in(h, g[0], w_cat, cos, sin, bsz, seq, n_q, n_kv, n_qi,
                                             HEAD_DIM ** -0.5 * LOG2_E, n_idx ** -0.5 * IDX_DIM ** -0.5)
            mixed = _dsa(q, qi, wi, kx, vx, kix, bsz, seq, k_sel)
            w_glu, w_out = None, att_w_out[j].astype(bf)
        h = _mix_mlp(mixed, h, norm_g[i][1:], w_glu, w_out, mlp_w1[i].astype(bf), mlp_w2[i].astype(bf),
                     seq, time_major=(i % 2 == 0))
    return h.reshape(bsz, seq, d_model)
```

```python
import functools
import math

import jax
import jax.numpy as jnp
from jax import lax
from jax.experimental import pallas as pl
from jax.experimental.pallas import tpu as pltpu

NORM_EPS = 1e-6
SSM_CH = 16
SSM_STATE = 64
DT_MIN, DT_MAX = 1e-3, 1e-1
HEAD_DIM = 64
KV_GROUP = 4
IDX_DIM = 64
TOPK_MAX = 256
ROPE_THETA = 10000.0

LANES = 128
SUBLANES = 8
MXU_DIM = 256
VMEM_LIMIT_BYTES = 56 * 1024 * 1024

TOKEN_TILE = 512
MLP_TOKEN_TILE = 1024
MLP_SUB_TILE = 512
FF_TILE = 1024
SCAN_CHUNK = 64
SCAN_LANES = 1024
SCAN_UNROLL = 2
Q_TILE = 256
K_TILE = 512
COARSE_ITERS = 9
RESIDUAL_ITERS = 8
BF16_ULP = 2.0 ** -7
F32_SLACK = 2.0 ** -21
FOLD_ROWS = 64
LOG2_E = math.log2(math.e)
BOUND_SLACK = 1.01
MIN_DENOMINATOR = 2.0 ** -60


def _params(*sem):
    return pltpu.CompilerParams(dimension_semantics=sem, vmem_limit_bytes=VMEM_LIMIT_BYTES)


def _rms(x, g):
    return x * lax.rsqrt(jnp.mean(x * x, axis=-1, keepdims=True) + NORM_EPS) * g


def _dot(a, b):
    return jnp.dot(a, b, preferred_element_type=jnp.float32)


def _dot_nt(a, b):
    return lax.dot_general(a, b, (((1,), (1,)), ((), ())), preferred_element_type=jnp.float32)


def _mix_mlp_kernel(x_ref, h_ref, g_ref, *rest, gated, sub, tf):
    if gated:
        wg_ref, wo_ref, w1_ref, w2_ref, o_ref = rest
    else:
        wo_ref, w1_ref, w2_ref, o_ref = rest
    g = g_ref[...]
    ff = w1_ref.shape[1]
    for r in range(h_ref.shape[0] // sub):
        rows = slice(r * sub, (r + 1) * sub)
        x = x_ref[rows, :]
        if gated:
            x = x * jax.nn.sigmoid(_dot(x.astype(jnp.bfloat16), wg_ref[...]))
        h = h_ref[rows, :] + _rms(_dot(x.astype(jnp.bfloat16), wo_ref[...]), g[0:1])
        xn = _rms(h, g[1:2]).astype(jnp.bfloat16)
        acc = None
        for c in range(ff // tf):
            cols = slice(c * tf, (c + 1) * tf)
            a = jnp.maximum(_dot(xn, w1_ref[:, cols]), 0.0)
            part = _dot((a * a).astype(jnp.bfloat16), w2_ref[cols, :])
            acc = part if acc is None else acc + part
        o_ref[rows, :] = h + _rms(acc, g[2:3])


def _mix_mlp(x, h, gains, w_glu, w_out, w1, w2, seq, time_major):
    t, d = h.shape
    ff = w1.shape[1]
    n = w_out.shape[0]
    tm = min(MLP_TOKEN_TILE, seq)
    nt = seq // tm
    gated = w_glu is not None
    kern = functools.partial(_mix_mlp_kernel, gated=gated, sub=min(MLP_SUB_TILE, tm), tf=min(FF_TILE, ff))
    resident = dict(pipeline_mode=pl.Buffered(1))
    whole = lambda a: pl.BlockSpec(a.shape, lambda i: (0, 0), **resident)
    x_map = (lambda i: (i % nt, i // nt)) if time_major else (lambda i: (i, 0))
    weights = ([w_glu] if gated else []) + [w_out, w1, w2]
    return pl.pallas_call(
        kern,
        out_shape=jax.ShapeDtypeStruct((t, d), jnp.float32),
        grid=(t // tm,),
        in_specs=[pl.BlockSpec((tm, n), x_map),
                  pl.BlockSpec((tm, d), lambda i: (i, 0)),
                  pl.BlockSpec(gains.shape, lambda i: (0, 0))] + [whole(w) for w in weights],
        out_specs=pl.BlockSpec((tm, d), lambda i: (i, 0)),
        compiler_params=_params("parallel"),
        name="mix_mlp",
    )(x, h, gains, *weights)


def _ssm_scan_kernel(h_ref, g_ref, w_ref, ar_ref, ai_ref, bmat_ref, cmat_ref, d_ref, z_ref, x_ref, st_ref, io_ref,
                     *, bsz, chunk, half, lanes):
    @pl.when(pl.program_id(0) == 0)
    def _():
        st_ref[...] = jnp.zeros_like(st_ref)

    e = d_ref.shape[1]
    n_blk = e // MXU_DIM
    sl = half // n_blk

    def state_lanes(j):
        return slice(j * sl, (j + 1) * sl), slice(half + j * sl, half + (j + 1) * sl)

    n_slab = e // LANES
    hb = h_ref[...]
    xn = _rms(hb.reshape(bsz * chunk, hb.shape[2]), g_ref[...]).astype(jnp.bfloat16)
    u_bt = _dot(xn, w_ref[...])
    for b in range(bsz):
        for c in range(n_slab):
            io_ref[c, pl.ds(b, chunk, stride=bsz), :] = u_bt[b * chunk:(b + 1) * chunk, c * LANES:(c + 1) * LANES]
    u = jnp.concatenate([io_ref[c] for c in range(n_slab)], axis=1)
    ub = u.astype(jnp.bfloat16)
    for j in range(n_blk):
        ch = slice(j * MXU_DIM, (j + 1) * MXU_DIM)
        for lanes_j in state_lanes(j):
            x_ref[:, lanes_j] = _dot(ub[:, ch], bmat_ref[ch, lanes_j])

    for c in range(half // lanes):
        re = slice(c * lanes, (c + 1) * lanes)
        im = slice(half + c * lanes, half + (c + 1) * lanes)
        ar = ar_ref[:, re]
        ai = ai_ref[:, re]

        def step(t, carry, re=re, im=im, ar=ar, ai=ai):
            xr, xi = carry
            rows = pl.ds(pl.multiple_of(t * bsz, bsz), bsz)
            nr = ar * xr - ai * xi + x_ref[rows, re]
            ni = ar * xi + ai * xr + x_ref[rows, im]
            x_ref[rows, re] = nr
            x_ref[rows, im] = ni
            return nr, ni

        xr, xi = lax.fori_loop(0, chunk, step, (st_ref[:, re], st_ref[:, im]), unroll=SCAN_UNROLL)
        st_ref[:, re] = xr
        st_ref[:, im] = xi

    for j in range(n_blk):
        ch = slice(j * MXU_DIM, (j + 1) * MXU_DIM)
        re_j, im_j = state_lanes(j)
        y = (_dot(x_ref[:, re_j].astype(jnp.bfloat16), cmat_ref[re_j, ch])
             + _dot(x_ref[:, im_j].astype(jnp.bfloat16), cmat_ref[im_j, ch])
             + d_ref[:, ch] * u[:, ch])
        z = jax.nn.gelu(y)
        for c in range(MXU_DIM // LANES):
            io_ref[j * (MXU_DIM // LANES) + c] = z[:, c * LANES:(c + 1) * LANES]
    for b in range(bsz):
        for c in range(n_slab):
            z_ref[:, b * e + c * LANES:b * e + (c + 1) * LANES] = io_ref[c, pl.ds(b, chunk, stride=bsz), :]


def _ssm_scan(h, g, w_in, ar, ai, bmat, cmat, dvec):
    bsz, seq, d = h.shape
    e = w_in.shape[1]
    n2 = bmat.shape[1]
    half = n2 // 2
    chunk = min(SCAN_CHUNK, seq)
    lanes = min(SCAN_LANES, half)
    kern = functools.partial(_ssm_scan_kernel, bsz=bsz, chunk=chunk, half=half, lanes=lanes)
    return pl.pallas_call(
        kern,
        out_shape=jax.ShapeDtypeStruct((seq, bsz * e), jnp.float32),
        grid=(seq // chunk,),
        in_specs=[
            pl.BlockSpec((bsz, chunk, d), lambda i: (0, i, 0)),
            pl.BlockSpec((1, d), lambda i: (0, 0)),
            pl.BlockSpec((d, e), lambda i: (0, 0)),
            pl.BlockSpec((bsz, half), lambda i: (0, 0)),
            pl.BlockSpec((bsz, half), lambda i: (0, 0)),
            pl.BlockSpec((e, n2), lambda i: (0, 0)),
            pl.BlockSpec((n2, e), lambda i: (0, 0)),
            pl.BlockSpec((1, e), lambda i: (0, 0)),
        ],
        out_specs=pl.BlockSpec((chunk, bsz * e), lambda i: (i, 0)),
        scratch_shapes=[pltpu.VMEM((chunk * bsz, n2), jnp.float32),
                        pltpu.VMEM((bsz, n2), jnp.float32),
                        pltpu.VMEM((e // LANES, chunk * bsz, LANES), jnp.float32)],
        compiler_params=_params("arbitrary"),
        name="ssm_scan",
    )(h, g, w_in, ar, ai, bmat, cmat, dvec)


def _ssm_discretise(lam_re, lam_im, log_dt, b_re, b_im, c_re, c_im, d_skip, bsz):
    g, p = lam_re.shape
    c = b_re.shape[2]
    dt = jnp.exp(log_dt)[:, None]
    mag = jnp.exp(lam_re * dt)
    abar_re = mag * jnp.cos(lam_im * dt)
    abar_im = mag * jnp.sin(lam_im * dt)
    den = lam_re * lam_re + lam_im * lam_im
    nr = abar_re - 1.0
    ni = abar_im
    fr = (nr * lam_re + ni * lam_im) / den
    fi = (ni * lam_re - nr * lam_im) / den
    bbar_re = fr[..., None] * b_re - fi[..., None] * b_im
    bbar_im = fr[..., None] * b_im + fi[..., None] * b_re
    bf = jnp.bfloat16
    state_of_col = jnp.arange(g * p) % p
    spread_p = (jnp.arange(p)[:, None] == state_of_col[None, :]).astype(bf)
    same_group = (jnp.arange(g * c)[:, None] // c) == (jnp.arange(g * p)[None, :] // p)

    def bd(m):
        rows = m.transpose(0, 2, 1).reshape(g * c, p).astype(bf)
        return jnp.where(same_group, jnp.dot(rows, spread_p, preferred_element_type=jnp.float32), 0.0).astype(bf)

    def cd(m):
        cols = m.transpose(2, 0, 1).reshape(p, g * c).astype(bf)
        return jnp.where(same_group.T, jnp.dot(spread_p.T, cols, preferred_element_type=jnp.float32), 0.0).astype(bf)

    bmat = jnp.concatenate([bd(bbar_re), bd(bbar_im)], axis=1)
    cmat = jnp.concatenate([cd(c_re), cd(-c_im)], axis=0)
    ar = jnp.broadcast_to(abar_re.reshape(1, g * p), (bsz, g * p))
    ai = jnp.broadcast_to(abar_im.reshape(1, g * p), (bsz, g * p))
    return ar, ai, bmat, cmat, d_skip.reshape(1, g * c)


def _rope128(x, cos, sin_signed, lane):
    swapped = jnp.where((lane % HEAD_DIM) < HEAD_DIM // 2,
                        pltpu.roll(x, LANES - HEAD_DIM // 2, axis=1),
                        pltpu.roll(x, HEAD_DIM // 2, axis=1))
    return x * cos + swapped * sin_signed


def _att_in_kernel(h_ref, g_ref, w_ref, cos_ref, sin_ref, q_ref, k_ref, v_ref, qi_ref, ki_ref, wi_ref,
                   *, n_q, n_kv, n_qi, q_scale, wi_scale):
    xn = _rms(h_ref[...], g_ref[...]).astype(jnp.bfloat16)
    proj = _dot(xn, w_ref[...])
    cos = cos_ref[...]
    sin = sin_ref[...]
    lane = lax.broadcasted_iota(jnp.int32, cos.shape, 1)
    is_head = lane < HEAD_DIM

    def chunk(off, c):
        return proj[:, off + c * LANES: off + (c + 1) * LANES]

    def head_of(x, n):
        return x if n % 2 == 0 else pltpu.roll(x, HEAD_DIM, axis=1)

    for c in range(n_q // LANES):
        q_ref[:, c * LANES:(c + 1) * LANES] = (_rope128(chunk(0, c), cos, sin, lane) * q_scale).astype(q_ref.dtype)
    off = n_q
    for n in range(n_kv):
        k2 = _rope128(chunk(off, n // 2), cos, sin, lane)
        kx = jnp.where(is_head, head_of(k2, n), jnp.where(lane == HEAD_DIM, 1.0, 0.0))
        k_ref[:, n * LANES:(n + 1) * LANES] = kx.astype(k_ref.dtype)
    off += n_kv * HEAD_DIM
    for n in range(n_kv):
        vx = jnp.where(is_head, head_of(chunk(off, n // 2), n), 1.0)
        v_ref[:, n * LANES:(n + 1) * LANES] = vx.astype(v_ref.dtype)
    off += n_kv * HEAD_DIM
    for c in range(n_qi // LANES):
        qi_ref[:, c * LANES:(c + 1) * LANES] = _rope128(chunk(off, c), cos, sin, lane).astype(qi_ref.dtype)
    off += n_qi
    last = chunk(off, 0)
    ki_lo = jnp.where(is_head, _rope128(last, cos, sin, lane), 0.0)
    ki_ref[:, :LANES] = ki_lo.astype(ki_ref.dtype)
    ki_ref[:, LANES:] = pltpu.roll(ki_lo, HEAD_DIM, axis=1).astype(ki_ref.dtype)
    wi_ref[...] = pltpu.roll(last, HEAD_DIM, axis=1) * wi_scale


def _att_in(h, g, w, cos, sin, bsz, seq, n_q, n_kv, n_qi, q_scale, wi_scale):
    d = h.shape[1]
    n_k = n_v = n_kv * LANES
    n_ki = 2 * LANES
    ncols = w.shape[1]
    tm = min(TOKEN_TILE, seq)
    nt = seq // tm
    t = bsz * seq
    row = lambda b, i: (b * nt + i, 0)
    kern = functools.partial(_att_in_kernel, n_q=n_q, n_kv=n_kv, n_qi=n_qi, q_scale=q_scale, wi_scale=wi_scale)
    bf = jnp.bfloat16
    return pl.pallas_call(
        kern,
        out_shape=[jax.ShapeDtypeStruct((t, n_q), bf), jax.ShapeDtypeStruct((t, n_k), bf),
                   jax.ShapeDtypeStruct((t, n_v), bf), jax.ShapeDtypeStruct((t, n_qi), bf),
                   jax.ShapeDtypeStruct((t, n_ki), bf), jax.ShapeDtypeStruct((t, LANES), jnp.float32)],
        grid=(bsz, nt),
        in_specs=[
            pl.BlockSpec((tm, d), row),
            pl.BlockSpec((1, d), lambda b, i: (0, 0)),
            pl.BlockSpec((d, ncols), lambda b, i: (0, 0)),
            pl.BlockSpec((tm, LANES), lambda b, i: (i, 0)),
            pl.BlockSpec((tm, LANES), lambda b, i: (i, 0)),
        ],
        out_specs=[pl.BlockSpec((tm, n_q), row), pl.BlockSpec((tm, n_k), row), pl.BlockSpec((tm, n_v), row),
                   pl.BlockSpec((tm, n_qi), row), pl.BlockSpec((tm, n_ki), row), pl.BlockSpec((tm, LANES), row)],
        compiler_params=_params("parallel", "parallel"),
        name="att_in",
    )(h, g, w, cos, sin)


def _lane_fold(x, op):
    out = x[:, :LANES]
    for j in range(1, x.shape[1] // LANES):
        out = op(out, x[:, j * LANES:(j + 1) * LANES])
    return out


def _row_fold(x, op):
    out = x[:FOLD_ROWS]
    for j in range(1, x.shape[0] // FOLD_ROWS):
        out = op(out, x[j * FOLD_ROWS:(j + 1) * FOLD_ROWS])
    return out


def _dsa_kernel(q_ref, qi_ref, wi_ref, k_ref, v_ref, ki_ref, o_ref,
                sc_ref, sct_ref, sct16_ref, stage_ref, wb_ref, qx_ref, acc_ref, m_ref, kn_ref,
                *, tq, tk, k_sel, n_heads, n_idx):
    i = pl.program_id(1)
    n_kv = n_heads // KV_GROUP
    lane = lax.broadcasted_iota(jnp.int32, (tq, LANES), 1)
    is_head = lane < HEAD_DIM

    @pl.when(i == 0)
    def _():
        def body(r, c):
            x = k_ref[pl.ds(pl.multiple_of(r * tk, tk), tk), :].astype(jnp.float32)
            return jnp.maximum(c, jnp.max(x * x, axis=0, keepdims=True))
        sq_max = lax.fori_loop(0, k_ref.shape[0] // tk, body, jnp.zeros((1, k_ref.shape[1]), jnp.float32))
        lane_k = lax.broadcasted_iota(jnp.int32, (1, LANES), 1)
        head_lane = lax.broadcasted_iota(jnp.int32, kn_ref.shape, 1)
        kn = jnp.zeros(kn_ref.shape, jnp.float32)
        for n in range(n_kv):
            chunk = jnp.where(lane_k < HEAD_DIM, sq_max[:, n * LANES:(n + 1) * LANES], 0.0)
            kn = jnp.where(head_lane // KV_GROUP == n, jnp.sum(chunk, axis=1, keepdims=True), kn)
        kn_ref[...] = kn

    n_kt = (i * tq + tq + tk - 1) // tk
    neg_inf = jnp.float32(-jnp.inf)
    row = i * tq + lax.broadcasted_iota(jnp.int32, (tq, tk), 0)
    col0 = lax.broadcasted_iota(jnp.int32, (tq, tk), 1)
    wi = wi_ref[...]
    for hh in range(n_idx):
        wb_ref[hh] = jnp.broadcast_to(wi[:, hh:hh + 1], (tq, LANES))

    def raw_scores(kt):
        ks = pl.ds(pl.multiple_of(kt * tk, tk), tk)
        acc = jnp.zeros((tq, tk), jnp.float32)
        for hh in range(n_idx):
            qc = qi_ref[:, (hh // 2) * LANES:(hh // 2 + 1) * LANES]
            kc = ki_ref[ks, (hh % 2) * LANES:(hh % 2 + 1) * LANES]
            acc = acc + jnp.maximum(_dot_nt(qc, kc), 0.0) * jnp.concatenate([wb_ref[hh]] * (tk // LANES), axis=1)
        return acc

    def mask_tile(kt, carry):
        rmax, rmin = carry
        causal = (col0 + kt * tk) <= row
        masked = jnp.where(causal, stage_ref[...], neg_inf)
        sc_ref[kt] = masked
        masked_t = masked.T
        sct_ref[kt] = masked_t
        sct16_ref[kt] = masked_t.astype(jnp.bfloat16)
        rmax = jnp.maximum(rmax, _row_fold(masked_t, jnp.maximum))
        rmin = jnp.minimum(rmin, _row_fold(jnp.where(masked_t == neg_inf, -neg_inf, masked_t), jnp.minimum))
        return rmax, rmin

    def score_step(kt, carry):
        carry = mask_tile(kt - 1, carry)
        stage_ref[...] = raw_scores(kt)
        return carry

    stage_ref[...] = raw_scores(0)
    extremes = lax.fori_loop(1, n_kt, score_step,
                             (jnp.full((FOLD_ROWS, tq), neg_inf), jnp.full((FOLD_ROWS, tq), -neg_inf)))
    rmax, rmin = mask_tile(n_kt - 1, extremes)

    kf = jnp.float32(k_sel)

    def sweep(fn, init, x):
        def body(kt, c):
            for j in range(tk // FOLD_ROWS):
                c = fn(c, sct_ref[kt, j * FOLD_ROWS:(j + 1) * FOLD_ROWS, :], x)
            return c
        return lax.fori_loop(0, n_kt, body, tuple(jnp.full((FOLD_ROWS, tq), v, jnp.float32) for v in init))

    def col_sum(part):
        return jnp.sum(part, axis=0, keepdims=True)

    def col_max(part):
        return jnp.max(part, axis=0, keepdims=True)

    def count_ge_coarse(x16):
        one, zero = jnp.bfloat16(1), jnp.bfloat16(0)

        def body(kt, c):
            for j in range(tk // FOLD_ROWS):
                c = c + jnp.where(sct16_ref[kt, j * FOLD_ROWS:(j + 1) * FOLD_ROWS, :] >= x16, one, zero)
            return c
        part = lax.fori_loop(0, n_kt, body, jnp.zeros((FOLD_ROWS, tq), jnp.bfloat16))
        return col_sum(part.astype(jnp.float32))

    def bisect_coarse(_, carry):
        lo, hi = carry
        mid16 = (0.5 * lo + 0.5 * hi).astype(jnp.bfloat16)
        mid = mid16.astype(jnp.float32)
        ge = count_ge_coarse(mid16) >= kf
        return jnp.where(ge, mid, lo), jnp.where(ge, hi, mid)

    rmin = -col_max(-rmin)
    rmax = col_max(rmax)
    lo, hi = lax.fori_loop(0, COARSE_ITERS, bisect_coarse, (rmin, rmax))
    lo = lo - (jnp.abs(lo) * BF16_ULP + jnp.float32(1e-30))

    def rebase(kt, c):
        for j in range(tk // FOLD_ROWS):
            rows = slice(j * FOLD_ROWS, (j + 1) * FOLD_ROWS)
            sct16_ref[kt, rows, :] = (sct_ref[kt, rows, :] - lo).astype(jnp.bfloat16)
        return c

    lax.fori_loop(0, n_kt, rebase, 0)
    width = (hi - lo) * (1.0 + BF16_ULP)
    dlo, dhi = lax.fori_loop(0, RESIDUAL_ITERS, bisect_coarse, (jnp.zeros_like(width), width))
    slack = (jnp.abs(lo) + dhi) * F32_SLACK
    hi = jnp.where(dhi < width, jnp.minimum(hi, lo + dhi + slack), hi)

    few = (i * tq + lax.broadcasted_iota(jnp.int32, (1, tq), 1)) < k_sel

    def max_le(x):
        (c,) = sweep(lambda c, s, x: (jnp.maximum(c[0], jnp.where(s <= x, s, neg_inf)),), (-jnp.inf,), x)
        return col_max(c)

    def probe(x):
        cnt, nxt = sweep(lambda c, s, x: (c[0] + jnp.where(s >= x, 1.0, 0.0),
                                          jnp.maximum(c[1], jnp.where(s < x, s, neg_inf))),
                         (0.0, -jnp.inf), x)
        return col_sum(cnt), col_max(nxt)

    cand0 = jnp.where(few, rmin, max_le(hi))
    cnt0, nxt0 = probe(cand0)

    def unresolved(cnt):
        return jnp.logical_and(jnp.logical_not(few), cnt < kf)

    def finish_cond(c):
        _, cnt, _ = c
        return jnp.max(jnp.where(unresolved(cnt), 1.0, 0.0)) > 0.0

    def finish_body(c):
        cand, cnt, nxt = c
        cand = jnp.where(unresolved(cnt), nxt, cand)
        cnt, nxt = probe(cand)
        return cand, cnt, nxt

    thr_t, cnt_ge, _ = lax.while_loop(finish_cond, finish_body, (cand0, cnt0, nxt0))

    def to_rows(x):
        return jnp.broadcast_to(x, (LANES, tq)).T

    thr = to_rows(thr_t)

    tied_t = jnp.logical_and(jnp.logical_not(few), cnt_ge > kf)

    @pl.when(jnp.max(jnp.where(tied_t, 1.0, 0.0)) > 0.0)
    def _():
        (c,) = sweep(lambda c, s, x: (c[0] + jnp.where(s > x, 1.0, 0.0),), (0.0,), thr_t)
        need = to_rows(kf - col_sum(c))[:, :1]
        thr1 = thr[:, :1]
        tied1 = to_rows(jnp.where(tied_t, 1.0, 0.0))[:, :1] > 0.0
        tri = (lax.broadcasted_iota(jnp.int32, (tk, tk), 0)
               <= lax.broadcasted_iota(jnp.int32, (tk, tk), 1)).astype(jnp.bfloat16)

        def drop_body(kt, run):
            s = sc_ref[kt]
            eq = jnp.logical_and(s == thr1, tied1)
            eqf = jnp.where(eq, 1.0, 0.0)
            rank = run + _dot(eqf.astype(jnp.bfloat16), tri)
            sc_ref[kt] = jnp.where(jnp.logical_and(eq, rank > need), neg_inf, s)
            return run + jnp.sum(eqf, axis=1, keepdims=True)

        lax.fori_loop(0, n_kt, drop_body, jnp.zeros((tq, 1), jnp.float32))

    qf = q_ref[...].astype(jnp.float32)
    n_q = qf.shape[1]
    head_of_col = lax.broadcasted_iota(jnp.int32, (n_q, LANES), 0) // HEAD_DIM
    head_sel = jnp.where(head_of_col == lax.broadcasted_iota(jnp.int32, (n_q, LANES), 1), 1.0, 0.0)
    qss = _dot((qf * qf).astype(jnp.bfloat16), head_sel.astype(jnp.bfloat16))
    bound = jnp.sqrt(jnp.max(qss * kn_ref[0:1, :], axis=1, keepdims=True)) * BOUND_SLACK
    shift = jnp.where(lane == HEAD_DIM, -bound, 0.0)
    for c in range(n_heads // 2):
        qc = qf[:, c * LANES:(c + 1) * LANES]
        for half in range(2):
            h = 2 * c + half
            n, g = h // KV_GROUP, h % KV_GROUP
            x = qc if half == 0 else pltpu.roll(qc, HEAD_DIM, axis=1)
            qx_ref[n, g * tq:(g + 1) * tq, :] = jnp.where(is_head, x, shift).astype(qx_ref.dtype)

    thr_tile = jnp.concatenate([thr] * (tk // LANES), axis=1)

    def attend(online):
        acc_ref[...] = jnp.zeros_like(acc_ref)
        if online:
            m_ref[...] = jnp.full_like(m_ref, neg_inf)

        def att_block(kt, width):
            ks = pl.ds(pl.multiple_of(kt * tk, tk), width)
            keep = sc_ref[kt, :, :width] >= thr_tile[:, :width]
            keep_b = jnp.where(keep, 1.0, 0.0).astype(jnp.bfloat16)
            for n in range(n_kv):
                s = _dot_nt(qx_ref[n], k_ref[ks, n * LANES:(n + 1) * LANES])
                vc = v_ref[ks, n * LANES:(n + 1) * LANES]
                if online:
                    s = jnp.where(jnp.concatenate([keep] * KV_GROUP, axis=0), s, neg_inf)
                    m_old = m_ref[n]
                    m_new = jnp.maximum(m_old, jnp.max(s, axis=1, keepdims=True))
                    m_safe = jnp.where(m_new == neg_inf, 0.0, m_new)
                    p = jnp.exp2(s - m_safe).astype(jnp.bfloat16)
                    acc_ref[n] = jnp.exp2(m_old - m_safe) * acc_ref[n] + _dot(p, vc)
                    m_ref[n] = m_new
                else:
                    p = jnp.exp2(s).astype(jnp.bfloat16).reshape(KV_GROUP, tq, width) * keep_b[None]
                    acc_ref[n] += _dot(p.reshape(KV_GROUP * tq, width), vc)

        def att_tile(kt, carry):
            att_block(kt, tk)
            return carry

        n_full = (i * tq + tq) // tk
        lax.fori_loop(0, n_full, att_tile, 0)
        rest = (i * tq + tq) % tk
        for part in range(1, tk // tq):
            @pl.when(rest == part * tq)
            def _(part=part):
                att_block(n_full, part * tq)

    attend(False)
    den = acc_ref[0]
    for n in range(1, n_kv):
        den = jnp.minimum(den, acc_ref[n])
    den_lane = lax.broadcasted_iota(jnp.int32, den.shape, 1) >= HEAD_DIM
    safe = jnp.min(jnp.where(den_lane, den, 1.0)) > MIN_DENOMINATOR

    @pl.when(jnp.logical_not(safe))
    def _():
        attend(True)

    for c in range(n_heads // 2):
        parts = []
        for half in range(2):
            h = 2 * c + half
            n, g = h // KV_GROUP, h % KV_GROUP
            a = acc_ref[n, g * tq:(g + 1) * tq, :]
            parts.append(a / pltpu.roll(a, HEAD_DIM, axis=1))
        out = jnp.where(is_head, parts[0], pltpu.roll(parts[1], HEAD_DIM, axis=1))
        o_ref[:, c * LANES:(c + 1) * LANES] = out.astype(o_ref.dtype)


def _dsa(q, qi, wi, kx, vx, kix, bsz, seq, k_sel):
    t, n_q = q.shape
    n_heads = n_q // HEAD_DIM
    n_kv = n_heads // KV_GROUP
    n_idx = qi.shape[1] // IDX_DIM
    tq = min(Q_TILE, seq)
    tk = min(K_TILE, seq)
    nq = seq // tq
    row = lambda b, i: (b * nq + i, 0)
    per_batch = lambda b, i: (b, 0)
    kern = functools.partial(_dsa_kernel, tq=tq, tk=tk, k_sel=k_sel, n_heads=n_heads, n_idx=n_idx)
    return pl.pallas_call(
        kern,
        out_shape=jax.ShapeDtypeStruct((t, n_q), jnp.bfloat16),
        grid=(bsz, nq),
        in_specs=[
            pl.BlockSpec((tq, n_q), row),
            pl.BlockSpec((tq, qi.shape[1]), row),
            pl.BlockSpec((tq, LANES), row),
            pl.BlockSpec((seq, kx.shape[1]), per_batch),
            pl.BlockSpec((seq, vx.shape[1]), per_batch),
            pl.BlockSpec((seq, kix.shape[1]), per_batch),
        ],
        out_specs=pl.BlockSpec((tq, n_q), row),
        scratch_shapes=[
            pltpu.VMEM((seq // tk, tq, tk), jnp.float32),
            pltpu.VMEM((seq // tk, tk, tq), jnp.float32),
            pltpu.VMEM((seq // tk, tk, tq), jnp.bfloat16),
            pltpu.VMEM((tq, tk), jnp.float32),
            pltpu.VMEM((n_idx, tq, LANES), jnp.float32),
            pltpu.VMEM((n_kv, KV_GROUP * tq, LANES), jnp.bfloat16),
            pltpu.VMEM((n_kv, KV_GROUP * tq, LANES), jnp.float32),
            pltpu.VMEM((n_kv, KV_GROUP * tq, 1), jnp.float32),
            pltpu.VMEM((SUBLANES, LANES), jnp.float32),
        ],
        compiler_params=_params("parallel", "arbitrary"),
        name="dsa",
    )(q, qi, wi, kx, vx, kix)


def _rope_tables(seq):
    half = HEAD_DIM // 2
    inv_freq = ROPE_THETA ** (-jnp.arange(half, dtype=jnp.float32) * 2.0 / HEAD_DIM)
    ang = jnp.arange(seq, dtype=jnp.float32)[:, None] * inv_freq[None, :]
    cos = jnp.concatenate([jnp.cos(ang)] * 4, axis=-1)
    sin = jnp.concatenate([-jnp.sin(ang), jnp.sin(ang)] * 2, axis=-1)
    return cos, sin


def _att_weights(w_in, d_model):
    n_heads = d_model // HEAD_DIM
    n_kv = n_heads // KV_GROUP
    n_idx = max(4, d_model // 128)
    n_q, n_qi = n_heads * HEAD_DIM, n_idx * IDX_DIM
    cols = n_q + 2 * n_kv * HEAD_DIM + n_qi + IDX_DIM + n_idx
    assert w_in.shape[1] == cols and IDX_DIM + n_idx <= LANES and n_kv % 2 == 0
    padded = -(-cols // LANES) * LANES
    w_cat = jnp.pad(w_in, ((0, 0), (0, padded - cols))).astype(jnp.bfloat16)
    return w_cat, n_q, n_kv, n_qi, n_idx


def kernel(x, norm_g, mlp_w1, mlp_w2, ssm_w_in, ssm_lam_re, ssm_lam_im, ssm_log_dt, ssm_b_re, ssm_b_im,
           ssm_c_re, ssm_c_im, ssm_d, ssm_w_glu, ssm_w_out, att_w_in, att_w_out):
    bsz, seq, d_model = x.shape
    depth = norm_g.shape[0]
    bf = jnp.bfloat16
    h = x.reshape(bsz * seq, d_model)
    cos, sin = _rope_tables(seq)
    k_sel = min(TOPK_MAX, seq // 4)
    for i in range(depth):
        j = i // 2
        g = norm_g[i][:, None, :]
        if i % 2 == 0:
            ar, ai, bmat, cmat, dvec = _ssm_discretise(
                ssm_lam_re[j], ssm_lam_im[j], ssm_log_dt[j], ssm_b_re[j], ssm_b_im[j],
                ssm_c_re[j], ssm_c_im[j], ssm_d[j], bsz)
            mixed = _ssm_scan(h.reshape(bsz, seq, d_model), g[0], ssm_w_in[j].astype(bf), ar, ai, bmat, cmat, dvec)
            w_glu, w_out = ssm_w_glu[j].astype(bf), ssm_w_out[j].astype(bf)
        else:
            w_cat, n_q, n_kv, n_qi, n_idx = _att_weights(att_w_in[j], d_model)
            q, kx, vx, qi, kix, wi = _att_in(h, g[0], w_cat, cos, sin, bsz, seq, n_q, n_kv, n_qi,
                                             HEAD_DIM ** -0.5 * LOG2_E, n_idx ** -0.5 * IDX_DIM ** -0.5)
            mixed = _dsa(q, qi, wi, kx, vx, kix, bsz, seq, k_sel)
            w_glu, w_out = None, att_w_out[j].astype(bf)
        h = _mix_mlp(mixed, h, norm_g[i][1:], w_glu, w_out, mlp_w1[i].astype(bf), mlp_w2[i].astype(bf),
                     seq, time_major=(i % 2 == 0))
    return h.reshape(bsz, seq, d_model)
```

```python
import functools
import math

import jax
import jax.numpy as jnp
from jax import lax
from jax.experimental import pallas as pl
from jax.experimental.pallas import tpu as pltpu

NORM_EPS = 1e-6
HEAD_DIM = 64
KV_GROUP = 4
IDX_DIM = 64
TOPK_MAX = 256
ROPE_THETA = 10000.0

LANES = 128
SUBLANES = 8
MXU_DIM = 256
VMEM_LIMIT_BYTES = 56 * 1024 * 1024

TOKEN_TILE = 512
MLP_TOKEN_TILE = 1024
MLP_SUB_TILE = 512
FF_TILE = 1024
SCAN_CHUNK = 64
SCAN_LANES = 1024
SCAN_UNROLL = 2
Q_TILE = 256
K_TILE = 512
COARSE_ITERS = 9
RESIDUAL_ITERS = 8
BF16_ULP = 2.0 ** -7
F32_SLACK = 2.0 ** -21
FOLD_ROWS = 64
LOG2_E = math.log2(math.e)
BOUND_SLACK = 1.01
MIN_DENOMINATOR = 2.0 ** -60


def _params(*sem):
    return pltpu.CompilerParams(dimension_semantics=sem, vmem_limit_bytes=VMEM_LIMIT_BYTES)


def _rms(x, g):
    return x * lax.rsqrt(jnp.mean(x * x, axis=-1, keepdims=True) + NORM_EPS) * g


def _dot(a, b):
    return jnp.dot(a, b, preferred_element_type=jnp.float32)


def _dot_nt(a, b):
    return lax.dot_general(a, b, (((1,), (1,)), ((), ())), preferred_element_type=jnp.float32)


def _mix_mlp_kernel(x_ref, h_ref, g_ref, *rest, gated, sub, tf):
    if gated:
        wg_ref, wo_ref, w1_ref, w2_ref, o_ref = rest
    else:
        wo_ref, w1_ref, w2_ref, o_ref = rest
    g = g_ref[...]
    ff = w1_ref.shape[1]
    for r in range(h_ref.shape[0] // sub):
        rows = slice(r * sub, (r + 1) * sub)
        x = x_ref[rows, :]
        if gated:
            x = x * jax.nn.sigmoid(_dot(x.astype(jnp.bfloat16), wg_ref[...]))
        h = h_ref[rows, :] + _rms(_dot(x.astype(jnp.bfloat16), wo_ref[...]), g[0:1])
        xn = _rms(h, g[1:2]).astype(jnp.bfloat16)
        acc = None
        for c in range(ff // tf):
            cols = slice(c * tf, (c + 1) * tf)
            a = jnp.maximum(_dot(xn, w1_ref[:, cols]), 0.0)
            part = _dot((a * a).astype(jnp.bfloat16), w2_ref[cols, :])
            acc = part if acc is None else acc + part
        o_ref[rows, :] = h + _rms(acc, g[2:3])


def _mix_mlp(x, h, gains, w_glu, w_out, w1, w2, seq, time_major):
    t, d = h.shape
    ff = w1.shape[1]
    n = w_out.shape[0]
    tm = min(MLP_TOKEN_TILE, seq)
    nt = seq // tm
    gated = w_glu is not None
    kern = functools.partial(_mix_mlp_kernel, gated=gated, sub=min(MLP_SUB_TILE, tm), tf=min(FF_TILE, ff))
    resident = dict(pipeline_mode=pl.Buffered(1))
    whole = lambda a: pl.BlockSpec(a.shape, lambda i: (0, 0), **resident)
    x_map = (lambda i: (i % nt, i // nt)) if time_major else (lambda i: (i, 0))
    weights = ([w_glu] if gated else []) + [w_out, w1, w2]
    return pl.pallas_call(
        kern,
        out_shape=jax.ShapeDtypeStruct((t, d), jnp.float32),
        grid=(t // tm,),
        in_specs=[pl.BlockSpec((tm, n), x_map),
                  pl.BlockSpec((tm, d), lambda i: (i, 0)),
                  pl.BlockSpec(gains.shape, lambda i: (0, 0))] + [whole(w) for w in weights],
        out_specs=pl.BlockSpec((tm, d), lambda i: (i, 0)),
        compiler_params=_params("parallel"),
        name="mix_mlp",
    )(x, h, gains, *weights)


def _ssm_scan_kernel(h_ref, g_ref, w_ref, ar_ref, ai_ref, bmat_ref, cmat_ref, d_ref, z_ref, x_ref, st_ref, io_ref,
                     *, bsz, chunk, half, lanes):
    @pl.when(pl.program_id(0) == 0)
    def _():
        st_ref[...] = jnp.zeros_like(st_ref)

    e = d_ref.shape[1]
    n_blk = e // MXU_DIM
    sl = half // n_blk

    def state_lanes(j):
        return slice(j * sl, (j + 1) * sl), slice(half + j * sl, half + (j + 1) * sl)

    n_slab = e // LANES
    hb = h_ref[...]
    xn = _rms(hb.reshape(bsz * chunk, hb.shape[2]), g_ref[...]).astype(jnp.bfloat16)
    u_bt = _dot(xn, w_ref[...])
    for b in range(bsz):
        for c in range(n_slab):
            io_ref[c, pl.ds(b, chunk, stride=bsz), :] = u_bt[b * chunk:(b + 1) * chunk, c * LANES:(c + 1) * LANES]
    u = jnp.concatenate([io_ref[c] for c in range(n_slab)], axis=1)
    ub = u.astype(jnp.bfloat16)
    for j in range(n_blk):
        ch = slice(j * MXU_DIM, (j + 1) * MXU_DIM)
        for lanes_j in state_lanes(j):
            x_ref[:, lanes_j] = _dot(ub[:, ch], bmat_ref[ch, lanes_j])

    for c in range(half // lanes):
        re = slice(c * lanes, (c + 1) * lanes)
        im = slice(half + c * lanes, half + (c + 1) * lanes)
        ar = ar_ref[:, re]
        ai = ai_ref[:, re]

        def step(t, carry, re=re, im=im, ar=ar, ai=ai):
            xr, xi = carry
            rows = pl.ds(pl.multiple_of(t * bsz, bsz), bsz)
            nr = ar * xr - ai * xi + x_ref[rows, re]
            ni = ar * xi + ai * xr + x_ref[rows, im]
            x_ref[rows, re] = nr
            x_ref[rows, im] = ni
            return nr, ni

        xr, xi = lax.fori_loop(0, chunk, step, (st_ref[:, re], st_ref[:, im]), unroll=SCAN_UNROLL)
        st_ref[:, re] = xr
        st_ref[:, im] = xi

    for j in range(n_blk):
        ch = slice(j * MXU_DIM, (j + 1) * MXU_DIM)
        re_j, im_j = state_lanes(j)
        y = (_dot(x_ref[:, re_j].astype(jnp.bfloat16), cmat_ref[re_j, ch])
             + _dot(x_ref[:, im_j].astype(jnp.bfloat16), cmat_ref[im_j, ch])
             + d_ref[:, ch] * u[:, ch])
        z = jax.nn.gelu(y)
        for c in range(MXU_DIM // LANES):
            io_ref[j * (MXU_DIM // LANES) + c] = z[:, c * LANES:(c + 1) * LANES]
    for b in range(bsz):
        for c in range(n_slab):
            z_ref[:, b * e + c * LANES:b * e + (c + 1) * LANES] = io_ref[c, pl.ds(b, chunk, stride=bsz), :]


def _ssm_scan(h, g, w_in, ar, ai, bmat, cmat, dvec):
    bsz, seq, d = h.shape
    e = w_in.shape[1]
    n2 = bmat.shape[1]
    half = n2 // 2
    chunk = min(SCAN_CHUNK, seq)
    lanes = min(SCAN_LANES, half)
    kern = functools.partial(_ssm_scan_kernel, bsz=bsz, chunk=chunk, half=half, lanes=lanes)
    return pl.pallas_call(
        kern,
        out_shape=jax.ShapeDtypeStruct((seq, bsz * e), jnp.float32),
        grid=(seq // chunk,),
        in_specs=[
            pl.BlockSpec((bsz, chunk, d), lambda i: (0, i, 0)),
            pl.BlockSpec((1, d), lambda i: (0, 0)),
            pl.BlockSpec((d, e), lambda i: (0, 0)),
            pl.BlockSpec((bsz, half), lambda i: (0, 0)),
            pl.BlockSpec((bsz, half), lambda i: (0, 0)),
            pl.BlockSpec((e, n2), lambda i: (0, 0)),
            pl.BlockSpec((n2, e), lambda i: (0, 0)),
            pl.BlockSpec((1, e), lambda i: (0, 0)),
        ],
        out_specs=pl.BlockSpec((chunk, bsz * e), lambda i: (i, 0)),
        scratch_shapes=[pltpu.VMEM((chunk * bsz, n2), jnp.float32),
                        pltpu.VMEM((bsz, n2), jnp.float32),
                        pltpu.VMEM((e // LANES, chunk * bsz, LANES), jnp.float32)],
        compiler_params=_params("arbitrary"),
        name="ssm_scan",
    )(h, g, w_in, ar, ai, bmat, cmat, dvec)


def _ssm_discretise(lam_re, lam_im, log_dt, b_re, b_im, c_re, c_im, d_skip, bsz):
    g, p = lam_re.shape
    c = b_re.shape[2]
    dt = jnp.exp(log_dt)[:, None]
    mag = jnp.exp(lam_re * dt)
    abar_re = mag * jnp.cos(lam_im * dt)
    abar_im = mag * jnp.sin(lam_im * dt)
    den = lam_re * lam_re + lam_im * lam_im
    nr = abar_re - 1.0
    ni = abar_im
    fr = (nr * lam_re + ni * lam_im) / den
    fi = (ni * lam_re - nr * lam_im) / den
    bbar_re = fr[..., None] * b_re - fi[..., None] * b_im
    bbar_im = fr[..., None] * b_im + fi[..., None] * b_re
    bf = jnp.bfloat16
    state_of_col = jnp.arange(g * p) % p
    spread_p = (jnp.arange(p)[:, None] == state_of_col[None, :]).astype(bf)
    same_group = (jnp.arange(g * c)[:, None] // c) == (jnp.arange(g * p)[None, :] // p)

    def bd(m):
        rows = m.transpose(0, 2, 1).reshape(g * c, p).astype(bf)
        return jnp.where(same_group, jnp.dot(rows, spread_p, preferred_element_type=jnp.float32), 0.0).astype(bf)

    def cd(m):
        cols = m.transpose(2, 0, 1).reshape(p, g * c).astype(bf)
        return jnp.where(same_group.T, jnp.dot(spread_p.T, cols, preferred_element_type=jnp.float32), 0.0).astype(bf)

    bmat = jnp.concatenate([bd(bbar_re), bd(bbar_im)], axis=1)
    cmat = jnp.concatenate([cd(c_re), cd(-c_im)], axis=0)
    ar = jnp.broadcast_to(abar_re.reshape(1, g * p), (bsz, g * p))
    ai = jnp.broadcast_to(abar_im.reshape(1, g * p), (bsz, g * p))
    return ar, ai, bmat, cmat, d_skip.reshape(1, g * c)


def _rope128(x, cos, sin_signed, lane):
    swapped = jnp.where((lane % HEAD_DIM) < HEAD_DIM // 2,
                        pltpu.roll(x, LANES - HEAD_DIM // 2, axis=1),
                        pltpu.roll(x, HEAD_DIM // 2, axis=1))
    return x * cos + swapped * sin_signed


def _att_in_kernel(h_ref, g_ref, w_ref, cos_ref, sin_ref, q_ref, k_ref, v_ref, qi_ref, ki_ref, wi_ref,
                   *, n_q, n_kv, n_qi, q_scale, wi_scale):
    xn = _rms(h_ref[...], g_ref[...]).astype(jnp.bfloat16)
    proj = _dot(xn, w_ref[...])
    cos = cos_ref[...]
    sin = sin_ref[...]
    lane = lax.broadcasted_iota(jnp.int32, cos.shape, 1)
    is_head = lane < HEAD_DIM

    def chunk(off, c):
        return proj[:, off + c * LANES: off + (c + 1) * LANES]

    def head_of(x, n):
        return x if n % 2 == 0 else pltpu.roll(x, HEAD_DIM, axis=1)

    for c in range(n_q // LANES):
        q_ref[:, c * LANES:(c + 1) * LANES] = (_rope128(chunk(0, c), cos, sin, lane) * q_scale).astype(q_ref.dtype)
    off = n_q
    for n in range(n_kv):
        k2 = _rope128(chunk(off, n // 2), cos, sin, lane)
        kx = jnp.where(is_head, head_of(k2, n), jnp.where(lane == HEAD_DIM, 1.0, 0.0))
        k_ref[:, n * LANES:(n + 1) * LANES] = kx.astype(k_ref.dtype)
    off += n_kv * HEAD_DIM
    for n in range(n_kv):
        vx = jnp.where(is_head, head_of(chunk(off, n // 2), n), 1.0)
        v_ref[:, n * LANES:(n + 1) * LANES] = vx.astype(v_ref.dtype)
    off += n_kv * HEAD_DIM
    for c in range(n_qi // LANES):
        qi_ref[:, c * LANES:(c + 1) * LANES] = _rope128(chunk(off, c), cos, sin, lane).astype(qi_ref.dtype)
    off += n_qi
    last = chunk(off, 0)
    ki_lo = jnp.where(is_head, _rope128(last, cos, sin, lane), 0.0)
    ki_ref[:, :LANES] = ki_lo.astype(ki_ref.dtype)
    ki_ref[:, LANES:] = pltpu.roll(ki_lo, HEAD_DIM, axis=1).astype(ki_ref.dtype)
    wi_ref[...] = pltpu.roll(last, HEAD_DIM, axis=1) * wi_scale


def _att_in(h, g, w, cos, sin, bsz, seq, n_q, n_kv, n_qi, q_scale, wi_scale):
    d = h.shape[1]
    n_k = n_v = n_kv * LANES
    n_ki = 2 * LANES
    ncols = w.shape[1]
    tm = min(TOKEN_TILE, seq)
    nt = seq // tm
    t = bsz * seq
    row = lambda b, i: (b * nt + i, 0)
    kern = functools.partial(_att_in_kernel, n_q=n_q, n_kv=n_kv, n_qi=n_qi, q_scale=q_scale, wi_scale=wi_scale)
    bf = jnp.bfloat16
    return pl.pallas_call(
        kern,
        out_shape=[jax.ShapeDtypeStruct((t, n_q), bf), jax.ShapeDtypeStruct((t, n_k), bf),
                   jax.ShapeDtypeStruct((t, n_v), bf), jax.ShapeDtypeStruct((t, n_qi), bf),
                   jax.ShapeDtypeStruct((t, n_ki), bf), jax.ShapeDtypeStruct((t, LANES), jnp.float32)],
        grid=(bsz, nt),
        in_specs=[
            pl.BlockSpec((tm, d), row),
            pl.BlockSpec((1, d), lambda b, i: (0, 0)),
            pl.BlockSpec((d, ncols), lambda b, i: (0, 0)),
            pl.BlockSpec((tm, LANES), lambda b, i: (i, 0)),
            pl.BlockSpec((tm, LANES), lambda b, i: (i, 0)),
        ],
        out_specs=[pl.BlockSpec((tm, n_q), row), pl.BlockSpec((tm, n_k), row), pl.BlockSpec((tm, n_v), row),
                   pl.BlockSpec((tm, n_qi), row), pl.BlockSpec((tm, n_ki), row), pl.BlockSpec((tm, LANES), row)],
        compiler_params=_params("parallel", "parallel"),
        name="att_in",
    )(h, g, w, cos, sin)


def _row_fold(x, op):
    out = x[:FOLD_ROWS]
    for j in range(1, x.shape[0] // FOLD_ROWS):
        out = op(out, x[j * FOLD_ROWS:(j + 1) * FOLD_ROWS])
    return out


def _dsa_kernel(q_ref, qi_ref, wi_ref, k_ref, v_ref, ki_ref, o_ref,
                sc_ref, sct_ref, sct16_ref, stage_ref, wb_ref, qx_ref, acc_ref, m_ref, kn_ref,
                *, tq, tk, k_sel, n_heads, n_idx):
    i = pl.program_id(1)
    n_kv = n_heads // KV_GROUP
    lane = lax.broadcasted_iota(jnp.int32, (tq, LANES), 1)
    is_head = lane < HEAD_DIM

    @pl.when(i == 0)
    def _():
        def body(r, c):
            x = k_ref[pl.ds(pl.multiple_of(r * tk, tk), tk), :].astype(jnp.float32)
            return jnp.maximum(c, jnp.max(x * x, axis=0, keepdims=True))
        sq_max = lax.fori_loop(0, k_ref.shape[0] // tk, body, jnp.zeros((1, k_ref.shape[1]), jnp.float32))
        lane_k = lax.broadcasted_iota(jnp.int32, (1, LANES), 1)
        head_lane = lax.broadcasted_iota(jnp.int32, kn_ref.shape, 1)
        kn = jnp.zeros(kn_ref.shape, jnp.float32)
        for n in range(n_kv):
            chunk = jnp.where(lane_k < HEAD_DIM, sq_max[:, n * LANES:(n + 1) * LANES], 0.0)
            kn = jnp.where(head_lane // KV_GROUP == n, jnp.sum(chunk, axis=1, keepdims=True), kn)
        kn_ref[...] = kn

    n_kt = (i * tq + tq + tk - 1) // tk
    neg_inf = jnp.float32(-jnp.inf)
    row = i * tq + lax.broadcasted_iota(jnp.int32, (tq, tk), 0)
    col0 = lax.broadcasted_iota(jnp.int32, (tq, tk), 1)
    wi = wi_ref[...]
    for hh in range(n_idx):
        wb_ref[hh] = jnp.broadcast_to(wi[:, hh:hh + 1], (tq, LANES))

    def raw_scores(kt):
        ks = pl.ds(pl.multiple_of(kt * tk, tk), tk)
        acc = jnp.zeros((tq, tk), jnp.float32)
        for hh in range(n_idx):
            qc = qi_ref[:, (hh // 2) * LANES:(hh // 2 + 1) * LANES]
            kc = ki_ref[ks, (hh % 2) * LANES:(hh % 2 + 1) * LANES]
            acc = acc + jnp.maximum(_dot_nt(qc, kc), 0.0) * jnp.concatenate([wb_ref[hh]] * (tk // LANES), axis=1)
        return acc

    def mask_tile(kt, carry):
        rmax, rmin = carry
        causal = (col0 + kt * tk) <= row
        masked = jnp.where(causal, stage_ref[...], neg_inf)
        sc_ref[kt] = masked
        masked_t = masked.T
        sct_ref[kt] = masked_t
        sct16_ref[kt] = masked_t.astype(jnp.bfloat16)
        rmax = jnp.maximum(rmax, _row_fold(masked_t, jnp.maximum))
        rmin = jnp.minimum(rmin, _row_fold(jnp.where(masked_t == neg_inf, -neg_inf, masked_t), jnp.minimum))
        return rmax, rmin

    def score_step(kt, carry):
        carry = mask_tile(kt - 1, carry)
        stage_ref[...] = raw_scores(kt)
        return carry

    stage_ref[...] = raw_scores(0)
    extremes = lax.fori_loop(1, n_kt, score_step,
                             (jnp.full((FOLD_ROWS, tq), neg_inf), jnp.full((FOLD_ROWS, tq), -neg_inf)))
    rmax, rmin = mask_tile(n_kt - 1, extremes)

    kf = jnp.float32(k_sel)

    def sweep(fn, init, x):
        def body(kt, c):
            for j in range(tk // FOLD_ROWS):
                c = fn(c, sct_ref[kt, j * FOLD_ROWS:(j + 1) * FOLD_ROWS, :], x)
            return c
        return lax.fori_loop(0, n_kt, body, tuple(jnp.full((FOLD_ROWS, tq), v, jnp.float32) for v in init))

    def col_sum(part):
        return jnp.sum(part, axis=0, keepdims=True)

    def col_max(part):
        return jnp.max(part, axis=0, keepdims=True)

    def count_ge_coarse(x16):
        one, zero = jnp.bfloat16(1), jnp.bfloat16(0)

        def body(kt, c):
            for j in range(tk // FOLD_ROWS):
                c = c + jnp.where(sct16_ref[kt, j * FOLD_ROWS:(j + 1) * FOLD_ROWS, :] >= x16, one, zero)
            return c
        part = lax.fori_loop(0, n_kt, body, jnp.zeros((FOLD_ROWS, tq), jnp.bfloat16))
        return col_sum(part.astype(jnp.float32))

    def bisect_coarse(_, carry):
        lo, hi = carry
        mid16 = (0.5 * lo + 0.5 * hi).astype(jnp.bfloat16)
        mid = mid16.astype(jnp.float32)
        ge = count_ge_coarse(mid16) >= kf
        return jnp.where(ge, mid, lo), jnp.where(ge, hi, mid)

    rmin = -col_max(-rmin)
    rmax = col_max(rmax)
    lo, hi = lax.fori_loop(0, COARSE_ITERS, bisect_coarse, (rmin, rmax))
    lo = lo - (jnp.abs(lo) * BF16_ULP + jnp.float32(1e-30))

    def rebase(kt, c):
        for j in range(tk // FOLD_ROWS):
            rows = slice(j * FOLD_ROWS, (j + 1) * FOLD_ROWS)
            sct16_ref[kt, rows, :] = (sct_ref[kt, rows, :] - lo).astype(jnp.bfloat16)
        return c

    lax.fori_loop(0, n_kt, rebase, 0)
    width = (hi - lo) * (1.0 + BF16_ULP)
    dlo, dhi = lax.fori_loop(0, RESIDUAL_ITERS, bisect_coarse, (jnp.zeros_like(width), width))
    slack = (jnp.abs(lo) + dhi) * F32_SLACK
    hi = jnp.where(dhi < width, jnp.minimum(hi, lo + dhi + slack), hi)

    few = (i * tq + lax.broadcasted_iota(jnp.int32, (1, tq), 1)) < k_sel

    def max_le(x):
        (c,) = sweep(lambda c, s, x: (jnp.maximum(c[0], jnp.where(s <= x, s, neg_inf)),), (-jnp.inf,), x)
        return col_max(c)

    def probe(x):
        def fn(c, s, x):
            ge = s >= x
            return c[0] + jnp.where(ge, 1.0, 0.0), jnp.maximum(c[1], jnp.where(ge, neg_inf, s))
        cnt, nxt = sweep(fn, (0.0, -jnp.inf), x)
        return col_sum(cnt), col_max(nxt)

    cand0 = jnp.where(few, rmin, max_le(hi))
    cnt0, nxt0 = probe(cand0)

    def unresolved(cnt):
        return jnp.logical_and(jnp.logical_not(few), cnt < kf)

    def finish_cond(c):
        _, cnt, _ = c
        return jnp.max(jnp.where(unresolved(cnt), 1.0, 0.0)) > 0.0

    def finish_body(c):
        cand, cnt, nxt = c
        cand = jnp.where(unresolved(cnt), nxt, cand)
        cnt, nxt = probe(cand)
        return cand, cnt, nxt

    thr_t, cnt_ge, _ = lax.while_loop(finish_cond, finish_body, (cand0, cnt0, nxt0))

    def to_rows(x):
        return jnp.broadcast_to(x, (LANES, tq)).T

    thr = to_rows(thr_t)

    tied_t = jnp.logical_and(jnp.logical_not(few), cnt_ge > kf)

    @pl.when(jnp.max(jnp.where(tied_t, 1.0, 0.0)) > 0.0)
    def _():
        (c,) = sweep(lambda c, s, x: (c[0] + jnp.where(s > x, 1.0, 0.0),), (0.0,), thr_t)
        need = to_rows(kf - col_sum(c))[:, :1]
        thr1 = thr[:, :1]
        tied1 = to_rows(jnp.where(tied_t, 1.0, 0.0))[:, :1] > 0.0
        tri = (lax.broadcasted_iota(jnp.int32, (tk, tk), 0)
               <= lax.broadcasted_iota(jnp.int32, (tk, tk), 1)).astype(jnp.bfloat16)

        def drop_body(kt, run):
            s = sc_ref[kt]
            eq = jnp.logical_and(s == thr1, tied1)
            eqf = jnp.where(eq, 1.0, 0.0)
            rank = run + _dot(eqf.astype(jnp.bfloat16), tri)
            sc_ref[kt] = jnp.where(jnp.logical_and(eq, rank > need), neg_inf, s)
            return run + jnp.sum(eqf, axis=1, keepdims=True)

        lax.fori_loop(0, n_kt, drop_body, jnp.zeros((tq, 1), jnp.float32))

    qf = q_ref[...].astype(jnp.float32)
    n_q = qf.shape[1]
    head_of_col = lax.broadcasted_iota(jnp.int32, (n_q, LANES), 0) // HEAD_DIM
    head_sel = jnp.where(head_of_col == lax.broadcasted_iota(jnp.int32, (n_q, LANES), 1), 1.0, 0.0)
    qss = _dot((qf * qf).astype(jnp.bfloat16), head_sel.astype(jnp.bfloat16))
    bound = jnp.sqrt(jnp.max(qss * kn_ref[0:1, :], axis=1, keepdims=True)) * BOUND_SLACK
    shift = jnp.where(lane == HEAD_DIM, -bound, 0.0)
    for c in range(n_heads // 2):
        qc = qf[:, c * LANES:(c + 1) * LANES]
        for half in range(2):
            h = 2 * c + half
            n, g = h // KV_GROUP, h % KV_GROUP
            x = qc if half == 0 else pltpu.roll(qc, HEAD_DIM, axis=1)
            qx_ref[n, g * tq:(g + 1) * tq, :] = jnp.where(is_head, x, shift).astype(qx_ref.dtype)

    thr_tile = jnp.concatenate([thr] * (tk // LANES), axis=1)

    def attend(online):
        acc_ref[...] = jnp.zeros_like(acc_ref)
        if online:
            m_ref[...] = jnp.full_like(m_ref, neg_inf)

        def att_block(kt, width):
            ks = pl.ds(pl.multiple_of(kt * tk, tk), width)
            keep = sc_ref[kt, :, :width] >= thr_tile[:, :width]
            keep_b = jnp.where(keep, 1.0, 0.0).astype(jnp.bfloat16)
            for n in range(n_kv):
                s = _dot_nt(qx_ref[n], k_ref[ks, n * LANES:(n + 1) * LANES])
                vc = v_ref[ks, n * LANES:(n + 1) * LANES]
                if online:
                    s = jnp.where(jnp.concatenate([keep] * KV_GROUP, axis=0), s, neg_inf)
                    m_old = m_ref[n]
                    m_new = jnp.maximum(m_old, jnp.max(s, axis=1, keepdims=True))
                    m_safe = jnp.where(m_new == neg_inf, 0.0, m_new)
                    p = jnp.exp2(s - m_safe).astype(jnp.bfloat16)
                    acc_ref[n] = jnp.exp2(m_old - m_safe) * acc_ref[n] + _dot(p, vc)
                    m_ref[n] = m_new
                else:
                    p = jnp.exp2(s).astype(jnp.bfloat16).reshape(KV_GROUP, tq, width) * keep_b[None]
                    acc_ref[n] += _dot(p.reshape(KV_GROUP * tq, width), vc)

        def att_tile(kt, carry):
            att_block(kt, tk)
            return carry

        n_full = (i * tq + tq) // tk
        lax.fori_loop(0, n_full, att_tile, 0)
        rest = (i * tq + tq) % tk
        for part in range(1, tk // tq):
            @pl.when(rest == part * tq)
            def _(part=part):
                att_block(n_full, part * tq)

    attend(False)
    den = acc_ref[0]
    for n in range(1, n_kv):
        den = jnp.minimum(den, acc_ref[n])
    den_lane = lax.broadcasted_iota(jnp.int32, den.shape, 1) >= HEAD_DIM
    safe = jnp.min(jnp.where(den_lane, den, 1.0)) > MIN_DENOMINATOR

    @pl.when(jnp.logical_not(safe))
    def _():
        attend(True)

    for c in range(n_heads // 2):
        parts = []
        for half in range(2):
            h = 2 * c + half
            n, g = h // KV_GROUP, h % KV_GROUP
            a = acc_ref[n, g * tq:(g + 1) * tq, :]
            parts.append(a / pltpu.roll(a, HEAD_DIM, axis=1))
        out = jnp.where(is_head, parts[0], pltpu.roll(parts[1], HEAD_DIM, axis=1))
        o_ref[:, c * LANES:(c + 1) * LANES] = out.astype(o_ref.dtype)


def _dsa(q, qi, wi, kx, vx, kix, bsz, seq, k_sel):
    t, n_q = q.shape
    n_heads = n_q // HEAD_DIM
    n_kv = n_heads // KV_GROUP
    n_idx = qi.shape[1] // IDX_DIM
    tq = min(Q_TILE, seq)
    tk = min(K_TILE, seq)
    nq = seq // tq
    assert seq % tk == 0 and tk % tq == 0 and tk % FOLD_ROWS == 0 and tq % LANES == 0
    assert seq // FOLD_ROWS <= 256, "bf16 partial counts must stay exactly representable"
    row = lambda b, i: (b * nq + i, 0)
    per_batch = lambda b, i: (b, 0)
    kern = functools.partial(_dsa_kernel, tq=tq, tk=tk, k_sel=k_sel, n_heads=n_heads, n_idx=n_idx)
    return pl.pallas_call(
        kern,
        out_shape=jax.ShapeDtypeStruct((t, n_q), jnp.bfloat16),
        grid=(bsz, nq),
        in_specs=[
            pl.BlockSpec((tq, n_q), row),
            pl.BlockSpec((tq, qi.shape[1]), row),
            pl.BlockSpec((tq, LANES), row),
            pl.BlockSpec((seq, kx.shape[1]), per_batch),
            pl.BlockSpec((seq, vx.shape[1]), per_batch),
            pl.BlockSpec((seq, kix.shape[1]), per_batch),
        ],
        out_specs=pl.BlockSpec((tq, n_q), row),
        scratch_shapes=[
            pltpu.VMEM((seq // tk, tq, tk), jnp.float32),
            pltpu.VMEM((seq // tk, tk, tq), jnp.float32),
            pltpu.VMEM((seq // tk, tk, tq), jnp.bfloat16),
            pltpu.VMEM((tq, tk), jnp.float32),
            pltpu.VMEM((n_idx, tq, LANES), jnp.float32),
            pltpu.VMEM((n_kv, KV_GROUP * tq, LANES), jnp.bfloat16),
            pltpu.VMEM((n_kv, KV_GROUP * tq, LANES), jnp.float32),
            pltpu.VMEM((n_kv, KV_GROUP * tq, 1), jnp.float32),
            pltpu.VMEM((SUBLANES, LANES), jnp.float32),
        ],
        compiler_params=_params("parallel", "arbitrary"),
        name="dsa",
    )(q, qi, wi, kx, vx, kix)


def _rope_tables(seq):
    half = HEAD_DIM // 2
    inv_freq = ROPE_THETA ** (-jnp.arange(half, dtype=jnp.float32) * 2.0 / HEAD_DIM)
    ang = jnp.arange(seq, dtype=jnp.float32)[:, None] * inv_freq[None, :]
    cos = jnp.concatenate([jnp.cos(ang)] * 4, axis=-1)
    sin = jnp.concatenate([-jnp.sin(ang), jnp.sin(ang)] * 2, axis=-1)
    return cos, sin


def _att_weights(w_in, d_model):
    n_heads = d_model // HEAD_DIM
    n_kv = n_heads // KV_GROUP
    n_idx = max(4, d_model // 128)
    n_q, n_qi = n_heads * HEAD_DIM, n_idx * IDX_DIM
    cols = n_q + 2 * n_kv * HEAD_DIM + n_qi + IDX_DIM + n_idx
    assert w_in.shape[1] == cols and IDX_DIM + n_idx <= LANES and n_kv % 2 == 0
    padded = -(-cols // LANES) * LANES
    w_cat = jnp.pad(w_in, ((0, 0), (0, padded - cols))).astype(jnp.bfloat16)
    return w_cat, n_q, n_kv, n_qi, n_idx


def kernel(x, norm_g, mlp_w1, mlp_w2, ssm_w_in, ssm_lam_re, ssm_lam_im, ssm_log_dt, ssm_b_re, ssm_b_im,
           ssm_c_re, ssm_c_im, ssm_d, ssm_w_glu, ssm_w_out, att_w_in, att_w_out):
    bsz, seq, d_model = x.shape
    depth = norm_g.shape[0]
    bf = jnp.bfloat16
    h = x.reshape(bsz * seq, d_model)
    cos, sin = _rope_tables(seq)
    k_sel = min(TOPK_MAX, seq // 4)
    for i in range(depth):
        j = i // 2
        g = norm_g[i][:, None, :]
        if i % 2 == 0:
            ar, ai, bmat, cmat, dvec = _ssm_discretise(
                ssm_lam_re[j], ssm_lam_im[j], ssm_log_dt[j], ssm_b_re[j], ssm_b_im[j],
                ssm_c_re[j], ssm_c_im[j], ssm_d[j], bsz)
            mixed = _ssm_scan(h.reshape(bsz, seq, d_model), g[0], ssm_w_in[j].astype(bf), ar, ai, bmat, cmat, dvec)
            w_glu, w_out = ssm_w_glu[j].astype(bf), ssm_w_out[j].astype(bf)
        else:
            w_cat, n_q, n_kv, n_qi, n_idx = _att_weights(att_w_in[j], d_model)
            q, kx, vx, qi, kix, wi = _att_in(h, g[0], w_cat, cos, sin, bsz, seq, n_q, n_kv, n_qi,
                                             HEAD_DIM ** -0.5 * LOG2_E, n_idx ** -0.5 * IDX_DIM ** -0.5)
            mixed = _dsa(q, qi, wi, kx, vx, kix, bsz, seq, k_sel)
            w_glu, w_out = None, att_w_out[j].astype(bf)
        h = _mix_mlp(mixed, h, norm_g[i][1:], w_glu, w_out, mlp_w1[i].astype(bf), mlp_w2[i].astype(bf),
                     seq, time_major=(i % 2 == 0))
    return h.reshape(bsz, seq, d_model)
```

```python
import functools
import math

import jax
import jax.numpy as jnp
from jax import lax
from jax.experimental import pallas as pl
from jax.experimental.pallas import tpu as pltpu

NORM_EPS = 1e-6
HEAD_DIM = 64
KV_GROUP = 4
IDX_DIM = 64
TOPK_MAX = 256
ROPE_THETA = 10000.0

LANES = 128
SUBLANES = 8
MXU_DIM = 256
VMEM_LIMIT_BYTES = 56 * 1024 * 1024

TOKEN_TILE = 512
MLP_TOKEN_TILE = 1024
MLP_SUB_TILE = 512
FF_TILE = 1024
SCAN_CHUNK = 64
SCAN_LANES = 1024
SCAN_UNROLL = 2
Q_TILE = 256
K_TILE = 512
COARSE_ITERS = 9
RESIDUAL_ITERS = 8
BF16_ULP = 2.0 ** -7
F32_SLACK = 2.0 ** -21
FOLD_ROWS = 64
LOG2_E = math.log2(math.e)
BOUND_SLACK = 1.01
MIN_DENOMINATOR = 2.0 ** -60


def _params(*sem):
    return pltpu.CompilerParams(dimension_semantics=sem, vmem_limit_bytes=VMEM_LIMIT_BYTES)


def _rms(x, g):
    return x * lax.rsqrt(jnp.mean(x * x, axis=-1, keepdims=True) + NORM_EPS) * g


def _dot(a, b):
    return jnp.dot(a, b, preferred_element_type=jnp.float32)


def _dot_nt(a, b):
    return lax.dot_general(a, b, (((1,), (1,)), ((), ())), preferred_element_type=jnp.float32)


def _mix_mlp_kernel(x_ref, h_ref, g_ref, *rest, gated, sub, tf):
    if gated:
        wg_ref, wo_ref, w1_ref, w2_ref, o_ref = rest
    else:
        wo_ref, w1_ref, w2_ref, o_ref = rest
    g = g_ref[...]
    ff = w1_ref.shape[1]
    for r in range(h_ref.shape[0] // sub):
        rows = slice(r * sub, (r + 1) * sub)
        x = x_ref[rows, :]
        if gated:
            x = x * jax.nn.sigmoid(_dot(x.astype(jnp.bfloat16), wg_ref[...]))
        h = h_ref[rows, :] + _rms(_dot(x.astype(jnp.bfloat16), wo_ref[...]), g[0:1])
        xn = _rms(h, g[1:2]).astype(jnp.bfloat16)
        acc = None
        for c in range(ff // tf):
            cols = slice(c * tf, (c + 1) * tf)
            a = jnp.maximum(_dot(xn, w1_ref[:, cols]), 0.0)
            part = _dot((a * a).astype(jnp.bfloat16), w2_ref[cols, :])
            acc = part if acc is None else acc + part
        o_ref[rows, :] = h + _rms(acc, g[2:3])


def _mix_mlp(x, h, gains, w_glu, w_out, w1, w2, seq, time_major):
    t, d = h.shape
    ff = w1.shape[1]
    n = w_out.shape[0]
    tm = min(MLP_TOKEN_TILE, seq)
    nt = seq // tm
    gated = w_glu is not None
    kern = functools.partial(_mix_mlp_kernel, gated=gated, sub=min(MLP_SUB_TILE, tm), tf=min(FF_TILE, ff))
    resident = dict(pipeline_mode=pl.Buffered(1))
    whole = lambda a: pl.BlockSpec(a.shape, lambda i: (0, 0), **resident)
    x_map = (lambda i: (i % nt, i // nt)) if time_major else (lambda i: (i, 0))
    weights = ([w_glu] if gated else []) + [w_out, w1, w2]
    return pl.pallas_call(
        kern,
        out_shape=jax.ShapeDtypeStruct((t, d), jnp.float32),
        grid=(t // tm,),
        in_specs=[pl.BlockSpec((tm, n), x_map),
                  pl.BlockSpec((tm, d), lambda i: (i, 0)),
                  pl.BlockSpec(gains.shape, lambda i: (0, 0))] + [whole(w) for w in weights],
        out_specs=pl.BlockSpec((tm, d), lambda i: (i, 0)),
        compiler_params=_params("parallel"),
        name="mix_mlp",
    )(x, h, gains, *weights)


def _ssm_scan_kernel(h_ref, g_ref, w_ref, ar_ref, ai_ref, bmat_ref, cmat_ref, d_ref, z_ref, x_ref, st_ref, io_ref,
                     *, bsz, chunk, half, lanes):
    @pl.when(pl.program_id(0) == 0)
    def _():
        st_ref[...] = jnp.zeros_like(st_ref)

    e = d_ref.shape[1]
    n_blk = e // MXU_DIM
    sl = half // n_blk

    def state_lanes(j):
        return slice(j * sl, (j + 1) * sl), slice(half + j * sl, half + (j + 1) * sl)

    n_slab = e // LANES
    hb = h_ref[...]
    xn = _rms(hb.reshape(bsz * chunk, hb.shape[2]), g_ref[...]).astype(jnp.bfloat16)
    u_bt = _dot(xn, w_ref[...])
    for b in range(bsz):
        for c in range(n_slab):
            io_ref[c, pl.ds(b, chunk, stride=bsz), :] = u_bt[b * chunk:(b + 1) * chunk, c * LANES:(c + 1) * LANES]
    u = jnp.concatenate([io_ref[c] for c in range(n_slab)], axis=1)
    ub = u.astype(jnp.bfloat16)
    for j in range(n_blk):
        ch = slice(j * MXU_DIM, (j + 1) * MXU_DIM)
        for lanes_j in state_lanes(j):
            x_ref[:, lanes_j] = _dot(ub[:, ch], bmat_ref[ch, lanes_j])

    for c in range(half // lanes):
        re = slice(c * lanes, (c + 1) * lanes)
        im = slice(half + c * lanes, half + (c + 1) * lanes)
        ar = ar_ref[:, re]
        ai = ai_ref[:, re]

        def step(t, carry, re=re, im=im, ar=ar, ai=ai):
            xr, xi = carry
            rows = pl.ds(pl.multiple_of(t * bsz, bsz), bsz)
            nr = ar * xr - ai * xi + x_ref[rows, re]
            ni = ar * xi + ai * xr + x_ref[rows, im]
            x_ref[rows, re] = nr
            x_ref[rows, im] = ni
            return nr, ni

        xr, xi = lax.fori_loop(0, chunk, step, (st_ref[:, re], st_ref[:, im]), unroll=SCAN_UNROLL)
        st_ref[:, re] = xr
        st_ref[:, im] = xi

    for j in range(n_blk):
        ch = slice(j * MXU_DIM, (j + 1) * MXU_DIM)
        re_j, im_j = state_lanes(j)
        y = (_dot(x_ref[:, re_j].astype(jnp.bfloat16), cmat_ref[re_j, ch])
             + _dot(x_ref[:, im_j].astype(jnp.bfloat16), cmat_ref[im_j, ch])
             + d_ref[:, ch] * u[:, ch])
        z = jax.nn.gelu(y)
        for c in range(MXU_DIM // LANES):
            io_ref[j * (MXU_DIM // LANES) + c] = z[:, c * LANES:(c + 1) * LANES]
    for b in range(bsz):
        for c in range(n_slab):
            z_ref[:, b * e + c * LANES:b * e + (c + 1) * LANES] = io_ref[c, pl.ds(b, chunk, stride=bsz), :]


def _ssm_scan(h, g, w_in, ar, ai, bmat, cmat, dvec):
    bsz, seq, d = h.shape
    e = w_in.shape[1]
    n2 = bmat.shape[1]
    half = n2 // 2
    chunk = min(SCAN_CHUNK, seq)
    lanes = min(SCAN_LANES, half)
    kern = functools.partial(_ssm_scan_kernel, bsz=bsz, chunk=chunk, half=half, lanes=lanes)
    return pl.pallas_call(
        kern,
        out_shape=jax.ShapeDtypeStruct((seq, bsz * e), jnp.float32),
        grid=(seq // chunk,),
        in_specs=[
            pl.BlockSpec((bsz, chunk, d), lambda i: (0, i, 0)),
            pl.BlockSpec((1, d), lambda i: (0, 0)),
            pl.BlockSpec((d, e), lambda i: (0, 0)),
            pl.BlockSpec((bsz, half), lambda i: (0, 0)),
            pl.BlockSpec((bsz, half), lambda i: (0, 0)),
            pl.BlockSpec((e, n2), lambda i: (0, 0)),
            pl.BlockSpec((n2, e), lambda i: (0, 0)),
            pl.BlockSpec((1, e), lambda i: (0, 0)),
        ],
        out_specs=pl.BlockSpec((chunk, bsz * e), lambda i: (i, 0)),
        scratch_shapes=[pltpu.VMEM((chunk * bsz, n2), jnp.float32),
                        pltpu.VMEM((bsz, n2), jnp.float32),
                        pltpu.VMEM((e // LANES, chunk * bsz, LANES), jnp.float32)],
        compiler_params=_params("arbitrary"),
        name="ssm_scan",
    )(h, g, w_in, ar, ai, bmat, cmat, dvec)


def _ssm_discretise(lam_re, lam_im, log_dt, b_re, b_im, c_re, c_im, d_skip, bsz):
    g, p = lam_re.shape
    c = b_re.shape[2]
    dt = jnp.exp(log_dt)[:, None]
    mag = jnp.exp(lam_re * dt)
    abar_re = mag * jnp.cos(lam_im * dt)
    abar_im = mag * jnp.sin(lam_im * dt)
    den = lam_re * lam_re + lam_im * lam_im
    nr = abar_re - 1.0
    ni = abar_im
    fr = (nr * lam_re + ni * lam_im) / den
    fi = (ni * lam_re - nr * lam_im) / den
    bbar_re = fr[..., None] * b_re - fi[..., None] * b_im
    bbar_im = fr[..., None] * b_im + fi[..., None] * b_re
    bf = jnp.bfloat16
    state_of_col = jnp.arange(g * p) % p
    spread_p = (jnp.arange(p)[:, None] == state_of_col[None, :]).astype(bf)
    same_group = (jnp.arange(g * c)[:, None] // c) == (jnp.arange(g * p)[None, :] // p)

    def bd(m):
        rows = m.transpose(0, 2, 1).reshape(g * c, p).astype(bf)
        return jnp.where(same_group, jnp.dot(rows, spread_p, preferred_element_type=jnp.float32), 0.0).astype(bf)

    def cd(m):
        cols = m.transpose(2, 0, 1).reshape(p, g * c).astype(bf)
        return jnp.where(same_group.T, jnp.dot(spread_p.T, cols, preferred_element_type=jnp.float32), 0.0).astype(bf)

    bmat = jnp.concatenate([bd(bbar_re), bd(bbar_im)], axis=1)
    cmat = jnp.concatenate([cd(c_re), cd(-c_im)], axis=0)
    ar = jnp.broadcast_to(abar_re.reshape(1, g * p), (bsz, g * p))
    ai = jnp.broadcast_to(abar_im.reshape(1, g * p), (bsz, g * p))
    return ar, ai, bmat, cmat, d_skip.reshape(1, g * c)


def _rope128(x, cos, sin_signed, lane):
    swapped = jnp.where((lane % HEAD_DIM) < HEAD_DIM // 2,
                        pltpu.roll(x, LANES - HEAD_DIM // 2, axis=1),
                        pltpu.roll(x, HEAD_DIM // 2, axis=1))
    return x * cos + swapped * sin_signed


def _att_in_kernel(h_ref, g_ref, w_ref, cos_ref, sin_ref, q_ref, k_ref, v_ref, qi_ref, ki_ref, wi_ref,
                   *, n_q, n_kv, n_qi, n_idx, q_scale, wi_scale):
    xn = _rms(h_ref[...], g_ref[...]).astype(jnp.bfloat16)
    proj = _dot(xn, w_ref[...])
    cos = cos_ref[...]
    sin = sin_ref[...]
    lane = lax.broadcasted_iota(jnp.int32, cos.shape, 1)
    is_head = lane < HEAD_DIM

    def chunk(off, c):
        return proj[:, off + c * LANES: off + (c + 1) * LANES]

    def head_of(x, n):
        return x if n % 2 == 0 else pltpu.roll(x, HEAD_DIM, axis=1)

    q_norm = jnp.zeros((proj.shape[0], 1), jnp.float32)
    for c in range(n_q // LANES):
        qc = (_rope128(chunk(0, c), cos, sin, lane) * q_scale).astype(q_ref.dtype)
        q_ref[:, c * LANES:(c + 1) * LANES] = qc
        sq = qc.astype(jnp.float32) ** 2
        for half_sq in (jnp.where(is_head, sq, 0.0), jnp.where(is_head, 0.0, sq)):
            q_norm = jnp.maximum(q_norm, jnp.sum(half_sq, axis=1, keepdims=True))
    off = n_q
    for n in range(n_kv):
        k2 = _rope128(chunk(off, n // 2), cos, sin, lane)
        kx = jnp.where(is_head, head_of(k2, n), jnp.where(lane == HEAD_DIM, 1.0, 0.0))
        k_ref[:, n * LANES:(n + 1) * LANES] = kx.astype(k_ref.dtype)
    off += n_kv * HEAD_DIM
    for n in range(n_kv):
        vx = jnp.where(is_head, head_of(chunk(off, n // 2), n), 1.0)
        v_ref[:, n * LANES:(n + 1) * LANES] = vx.astype(v_ref.dtype)
    off += n_kv * HEAD_DIM
    for c in range(n_qi // LANES):
        qi_ref[:, c * LANES:(c + 1) * LANES] = _rope128(chunk(off, c), cos, sin, lane).astype(qi_ref.dtype)
    off += n_qi
    last = chunk(off, 0)
    ki_lo = jnp.where(is_head, _rope128(last, cos, sin, lane), 0.0)
    ki_ref[:, :LANES] = ki_lo.astype(ki_ref.dtype)
    ki_ref[:, LANES:] = pltpu.roll(ki_lo, HEAD_DIM, axis=1).astype(ki_ref.dtype)
    wi_ref[...] = jnp.where(lane == n_idx, q_norm, pltpu.roll(last, HEAD_DIM, axis=1) * wi_scale)


def _att_in(h, g, w, cos, sin, bsz, seq, n_q, n_kv, n_qi, q_scale, wi_scale):
    d = h.shape[1]
    n_k = n_v = n_kv * LANES
    n_ki = 2 * LANES
    ncols = w.shape[1]
    tm = min(TOKEN_TILE, seq)
    nt = seq // tm
    t = bsz * seq
    row = lambda b, i: (b * nt + i, 0)
    kern = functools.partial(_att_in_kernel, n_q=n_q, n_kv=n_kv, n_qi=n_qi, n_idx=n_qi // IDX_DIM,
                             q_scale=q_scale, wi_scale=wi_scale)
    bf = jnp.bfloat16
    return pl.pallas_call(
        kern,
        out_shape=[jax.ShapeDtypeStruct((t, n_q), bf), jax.ShapeDtypeStruct((t, n_k), bf),
                   jax.ShapeDtypeStruct((t, n_v), bf), jax.ShapeDtypeStruct((t, n_qi), bf),
                   jax.ShapeDtypeStruct((t, n_ki), bf), jax.ShapeDtypeStruct((t, LANES), jnp.float32)],
        grid=(bsz, nt),
        in_specs=[
            pl.BlockSpec((tm, d), row),
            pl.BlockSpec((1, d), lambda b, i: (0, 0)),
            pl.BlockSpec((d, ncols), lambda b, i: (0, 0)),
            pl.BlockSpec((tm, LANES), lambda b, i: (i, 0)),
            pl.BlockSpec((tm, LANES), lambda b, i: (i, 0)),
        ],
        out_specs=[pl.BlockSpec((tm, n_q), row), pl.BlockSpec((tm, n_k), row), pl.BlockSpec((tm, n_v), row),
                   pl.BlockSpec((tm, n_qi), row), pl.BlockSpec((tm, n_ki), row), pl.BlockSpec((tm, LANES), row)],
        compiler_params=_params("parallel", "parallel"),
        name="att_in",
    )(h, g, w, cos, sin)


def _row_fold(x, op):
    out = x[:FOLD_ROWS]
    for j in range(1, x.shape[0] // FOLD_ROWS):
        out = op(out, x[j * FOLD_ROWS:(j + 1) * FOLD_ROWS])
    return out


def _dsa_kernel(q_ref, qi_ref, wi_ref, k_ref, v_ref, ki_ref, o_ref,
                sc_ref, sct_ref, sct16_ref, stage_ref, wb_ref, qx_ref, acc_ref, m_ref, kn_ref,
                *, tq, tk, k_sel, n_heads, n_idx):
    i = pl.program_id(1)
    n_kv = n_heads // KV_GROUP
    lane = lax.broadcasted_iota(jnp.int32, (tq, LANES), 1)
    is_head = lane < HEAD_DIM

    @pl.when(i == 0)
    def _():
        def body(r, c):
            x = k_ref[pl.ds(pl.multiple_of(r * tk, tk), tk), :].astype(jnp.float32)
            return jnp.maximum(c, jnp.max(x * x, axis=0, keepdims=True))
        sq_max = lax.fori_loop(0, k_ref.shape[0] // tk, body, jnp.zeros((1, k_ref.shape[1]), jnp.float32))
        lane_k = lax.broadcasted_iota(jnp.int32, (1, LANES), 1)
        head_lane = lax.broadcasted_iota(jnp.int32, kn_ref.shape, 1)
        kn = jnp.zeros(kn_ref.shape, jnp.float32)
        for n in range(n_kv):
            chunk = jnp.where(lane_k < HEAD_DIM, sq_max[:, n * LANES:(n + 1) * LANES], 0.0)
            kn = jnp.where(head_lane // KV_GROUP == n, jnp.sum(chunk, axis=1, keepdims=True), kn)
        kn_ref[...] = kn

    n_kt = (i * tq + tq + tk - 1) // tk
    neg_inf = jnp.float32(-jnp.inf)
    row = i * tq + lax.broadcasted_iota(jnp.int32, (tq, tk), 0)
    col0 = lax.broadcasted_iota(jnp.int32, (tq, tk), 1)
    wi = wi_ref[...]
    for hh in range(n_idx):
        wb_ref[hh] = jnp.broadcast_to(wi[:, hh:hh + 1], (tq, LANES))

    def raw_scores(kt):
        ks = pl.ds(pl.multiple_of(kt * tk, tk), tk)
        acc = jnp.zeros((tq, tk), jnp.float32)
        for hh in range(n_idx):
            qc = qi_ref[:, (hh // 2) * LANES:(hh // 2 + 1) * LANES]
            kc = ki_ref[ks, (hh % 2) * LANES:(hh % 2 + 1) * LANES]
            acc = acc + jnp.maximum(_dot_nt(qc, kc), 0.0) * jnp.concatenate([wb_ref[hh]] * (tk // LANES), axis=1)
        return acc

    def mask_tile(kt, carry):
        rmax, rmin = carry
        causal = (col0 + kt * tk) <= row
        masked = jnp.where(causal, stage_ref[...], neg_inf)
        sc_ref[kt] = masked
        masked_t = masked.T
        sct_ref[kt] = masked_t
        sct16_ref[kt] = masked_t.astype(jnp.bfloat16)
        rmax = jnp.maximum(rmax, _row_fold(masked_t, jnp.maximum))
        rmin = jnp.minimum(rmin, _row_fold(jnp.where(masked_t == neg_inf, -neg_inf, masked_t), jnp.minimum))
        return rmax, rmin

    def score_step(kt, carry):
        carry = mask_tile(kt - 1, carry)
        stage_ref[...] = raw_scores(kt)
        return carry

    stage_ref[...] = raw_scores(0)
    extremes = lax.fori_loop(1, n_kt, score_step,
                             (jnp.full((FOLD_ROWS, tq), neg_inf), jnp.full((FOLD_ROWS, tq), -neg_inf)))
    rmax, rmin = mask_tile(n_kt - 1, extremes)

    kf = jnp.float32(k_sel)

    def sweep(fn, init, x):
        def body(kt, c):
            for j in range(tk // FOLD_ROWS):
                c = fn(c, sct_ref[kt, j * FOLD_ROWS:(j + 1) * FOLD_ROWS, :], x)
            return c
        return lax.fori_loop(0, n_kt, body, tuple(jnp.full((FOLD_ROWS, tq), v, jnp.float32) for v in init))

    def col_sum(part):
        return jnp.sum(part, axis=0, keepdims=True)

    def col_max(part):
        return jnp.max(part, axis=0, keepdims=True)

    def count_ge_coarse(x16):
        one, zero = jnp.bfloat16(1), jnp.bfloat16(0)

        def body(kt, c):
            for j in range(tk // FOLD_ROWS):
                c = c + jnp.where(sct16_ref[kt, j * FOLD_ROWS:(j + 1) * FOLD_ROWS, :] >= x16, one, zero)
            return c
        part = lax.fori_loop(0, n_kt, body, jnp.zeros((FOLD_ROWS, tq), jnp.bfloat16))
        return col_sum(part.astype(jnp.float32))

    def bisect_coarse(_, carry):
        lo, hi = carry
        mid16 = (0.5 * lo + 0.5 * hi).astype(jnp.bfloat16)
        mid = mid16.astype(jnp.float32)
        ge = count_ge_coarse(mid16) >= kf
        return jnp.where(ge, mid, lo), jnp.where(ge, hi, mid)

    rmin = -col_max(-rmin)
    rmax = col_max(rmax)
    lo, hi = lax.fori_loop(0, COARSE_ITERS, bisect_coarse, (rmin, rmax))
    lo = lo - (jnp.abs(lo) * BF16_ULP + jnp.float32(1e-30))

    def rebase(kt, c):
        for j in range(tk // FOLD_ROWS):
            rows = slice(j * FOLD_ROWS, (j + 1) * FOLD_ROWS)
            sct16_ref[kt, rows, :] = (sct_ref[kt, rows, :] - lo).astype(jnp.bfloat16)
        return c

    lax.fori_loop(0, n_kt, rebase, 0)
    width = (hi - lo) * (1.0 + BF16_ULP)
    dlo, dhi = lax.fori_loop(0, RESIDUAL_ITERS, bisect_coarse, (jnp.zeros_like(width), width))
    slack = (jnp.abs(lo) + dhi) * F32_SLACK
    hi = jnp.where(dhi < width, jnp.minimum(hi, lo + dhi + slack), hi)

    few = (i * tq + lax.broadcasted_iota(jnp.int32, (1, tq), 1)) < k_sel

    def max_le(x):
        (c,) = sweep(lambda c, s, x: (jnp.maximum(c[0], jnp.where(s <= x, s, neg_inf)),), (-jnp.inf,), x)
        return col_max(c)

    def probe(x):
        def fn(c, s, x):
            ge = s >= x
            return c[0] + jnp.where(ge, 1.0, 0.0), jnp.maximum(c[1], jnp.where(ge, neg_inf, s))
        cnt, nxt = sweep(fn, (0.0, -jnp.inf), x)
        return col_sum(cnt), col_max(nxt)

    cand0 = jnp.where(few, rmin, max_le(hi))
    cnt0, nxt0 = probe(cand0)

    def unresolved(cnt):
        return jnp.logical_and(jnp.logical_not(few), cnt < kf)

    def finish_cond(c):
        _, cnt, _ = c
        return jnp.max(jnp.where(unresolved(cnt), 1.0, 0.0)) > 0.0

    def finish_body(c):
        cand, cnt, nxt = c
        cand = jnp.where(unresolved(cnt), nxt, cand)
        cnt, nxt = probe(cand)
        return cand, cnt, nxt

    thr_t, cnt_ge, _ = lax.while_loop(finish_cond, finish_body, (cand0, cnt0, nxt0))

    def to_rows(x):
        return jnp.broadcast_to(x, (LANES, tq)).T

    thr = to_rows(thr_t)

    tied_t = jnp.logical_and(jnp.logical_not(few), cnt_ge > kf)

    @pl.when(jnp.max(jnp.where(tied_t, 1.0, 0.0)) > 0.0)
    def _():
        (c,) = sweep(lambda c, s, x: (c[0] + jnp.where(s > x, 1.0, 0.0),), (0.0,), thr_t)
        need = to_rows(kf - col_sum(c))[:, :1]
        thr1 = thr[:, :1]
        tied1 = to_rows(jnp.where(tied_t, 1.0, 0.0))[:, :1] > 0.0
        tri = (lax.broadcasted_iota(jnp.int32, (tk, tk), 0)
               <= lax.broadcasted_iota(jnp.int32, (tk, tk), 1)).astype(jnp.bfloat16)

        def drop_body(kt, run):
            s = sc_ref[kt]
            eq = jnp.logical_and(s == thr1, tied1)
            eqf = jnp.where(eq, 1.0, 0.0)
            rank = run + _dot(eqf.astype(jnp.bfloat16), tri)
            sc_ref[kt] = jnp.where(jnp.logical_and(eq, rank > need), neg_inf, s)
            return run + jnp.sum(eqf, axis=1, keepdims=True)

        lax.fori_loop(0, n_kt, drop_body, jnp.zeros((tq, 1), jnp.float32))

    bound = jnp.sqrt(wi[:, n_idx:n_idx + 1] * jnp.max(kn_ref[0:1, :], axis=1, keepdims=True)) * BOUND_SLACK
    shift = jnp.where(lane == HEAD_DIM, -bound, 0.0)
    for c in range(n_heads // 2):
        qc = q_ref[:, c * LANES:(c + 1) * LANES].astype(jnp.float32)
        for half in range(2):
            h = 2 * c + half
            n, g = h // KV_GROUP, h % KV_GROUP
            x = qc if half == 0 else pltpu.roll(qc, HEAD_DIM, axis=1)
            qx_ref[n, g * tq:(g + 1) * tq, :] = jnp.where(is_head, x, shift).astype(qx_ref.dtype)

    thr_tile = jnp.concatenate([thr] * (tk // LANES), axis=1)

    def attend(online):
        acc_ref[...] = jnp.zeros_like(acc_ref)
        if online:
            m_ref[...] = jnp.full_like(m_ref, neg_inf)

        def att_block(kt, width):
            ks = pl.ds(pl.multiple_of(kt * tk, tk), width)
            keep = sc_ref[kt, :, :width] >= thr_tile[:, :width]
            keep_b = jnp.where(keep, 1.0, 0.0).astype(jnp.bfloat16)
            for n in range(n_kv):
                s = _dot_nt(qx_ref[n], k_ref[ks, n * LANES:(n + 1) * LANES])
                vc = v_ref[ks, n * LANES:(n + 1) * LANES]
                if online:
                    s = jnp.where(jnp.concatenate([keep] * KV_GROUP, axis=0), s, neg_inf)
                    m_old = m_ref[n]
                    m_new = jnp.maximum(m_old, jnp.max(s, axis=1, keepdims=True))
                    m_safe = jnp.where(m_new == neg_inf, 0.0, m_new)
                    p = jnp.exp2(s - m_safe).astype(jnp.bfloat16)
                    acc_ref[n] = jnp.exp2(m_old - m_safe) * acc_ref[n] + _dot(p, vc)
                    m_ref[n] = m_new
                else:
                    p = jnp.exp2(s).astype(jnp.bfloat16).reshape(KV_GROUP, tq, width) * keep_b[None]
                    acc_ref[n] += _dot(p.reshape(KV_GROUP * tq, width), vc)

        def att_tile(kt, carry):
            att_block(kt, tk)
            return carry

        n_full = (i * tq + tq) // tk
        lax.fori_loop(0, n_full, att_tile, 0)
        rest = (i * tq + tq) % tk
        for part in range(1, tk // tq):
            @pl.when(rest == part * tq)
            def _(part=part):
                att_block(n_full, part * tq)

    attend(False)
    den = acc_ref[0]
    for n in range(1, n_kv):
        den = jnp.minimum(den, acc_ref[n])
    den_lane = lax.broadcasted_iota(jnp.int32, den.shape, 1) >= HEAD_DIM
    safe = jnp.min(jnp.where(den_lane, den, 1.0)) > MIN_DENOMINATOR

    @pl.when(jnp.logical_not(safe))
    def _():
        attend(True)

    for c in range(n_heads // 2):
        parts = []
        for half in range(2):
            h = 2 * c + half
            n, g = h // KV_GROUP, h % KV_GROUP
            a = acc_ref[n, g * tq:(g + 1) * tq, :]
            parts.append(a / pltpu.roll(a, HEAD_DIM, axis=1))
        out = jnp.where(is_head, parts[0], pltpu.roll(parts[1], HEAD_DIM, axis=1))
        o_ref[:, c * LANES:(c + 1) * LANES] = out.astype(o_ref.dtype)


def _dsa(q, qi, wi, kx, vx, kix, bsz, seq, k_sel):
    t, n_q = q.shape
    n_heads = n_q // HEAD_DIM
    n_kv = n_heads // KV_GROUP
    n_idx = qi.shape[1] // IDX_DIM
    tq = min(Q_TILE, seq)
    tk = min(K_TILE, seq)
    nq = seq // tq
    assert seq % tk == 0 and tk % tq == 0 and tk % FOLD_ROWS == 0 and tq % LANES == 0
    assert seq // FOLD_ROWS <= 256, "bf16 partial counts must stay exactly representable"
    row = lambda b, i: (b * nq + i, 0)
    per_batch = lambda b, i: (b, 0)
    kern = functools.partial(_dsa_kernel, tq=tq, tk=tk, k_sel=k_sel, n_heads=n_heads, n_idx=n_idx)
    return pl.pallas_call(
        kern,
        out_shape=jax.ShapeDtypeStruct((t, n_q), jnp.bfloat16),
        grid=(bsz, nq),
        in_specs=[
            pl.BlockSpec((tq, n_q), row),
            pl.BlockSpec((tq, qi.shape[1]), row),
            pl.BlockSpec((tq, LANES), row),
            pl.BlockSpec((seq, kx.shape[1]), per_batch),
            pl.BlockSpec((seq, vx.shape[1]), per_batch),
            pl.BlockSpec((seq, kix.shape[1]), per_batch),
        ],
        out_specs=pl.BlockSpec((tq, n_q), row),
        scratch_shapes=[
            pltpu.VMEM((seq // tk, tq, tk), jnp.float32),
            pltpu.VMEM((seq // tk, tk, tq), jnp.float32),
            pltpu.VMEM((seq // tk, tk, tq), jnp.bfloat16),
            pltpu.VMEM((tq, tk), jnp.float32),
            pltpu.VMEM((n_idx, tq, LANES), jnp.float32),
            pltpu.VMEM((n_kv, KV_GROUP * tq, LANES), jnp.bfloat16),
            pltpu.VMEM((n_kv, KV_GROUP * tq, LANES), jnp.float32),
            pltpu.VMEM((n_kv, KV_GROUP * tq, 1), jnp.float32),
            pltpu.VMEM((SUBLANES, LANES), jnp.float32),
        ],
        compiler_params=_params("parallel", "arbitrary"),
        name="dsa",
    )(q, qi, wi, kx, vx, kix)


def _rope_tables(seq):
    half = HEAD_DIM // 2
    inv_freq = ROPE_THETA ** (-jnp.arange(half, dtype=jnp.float32) * 2.0 / HEAD_DIM)
    ang = jnp.arange(seq, dtype=jnp.float32)[:, None] * inv_freq[None, :]
    cos = jnp.concatenate([jnp.cos(ang)] * 4, axis=-1)
    sin = jnp.concatenate([-jnp.sin(ang), jnp.sin(ang)] * 2, axis=-1)
    return cos, sin


def _att_weights(w_in, d_model):
    n_heads = d_model // HEAD_DIM
    n_kv = n_heads // KV_GROUP
    n_idx = max(4, d_model // 128)
    n_q, n_qi = n_heads * HEAD_DIM, n_idx * IDX_DIM
    cols = n_q + 2 * n_kv * HEAD_DIM + n_qi + IDX_DIM + n_idx
    assert w_in.shape[1] == cols and IDX_DIM + n_idx <= LANES and n_kv % 2 == 0
    padded = -(-cols // LANES) * LANES
    w_cat = jnp.pad(w_in, ((0, 0), (0, padded - cols))).astype(jnp.bfloat16)
    return w_cat, n_q, n_kv, n_qi, n_idx


def kernel(x, norm_g, mlp_w1, mlp_w2, ssm_w_in, ssm_lam_re, ssm_lam_im, ssm_log_dt, ssm_b_re, ssm_b_im,
           ssm_c_re, ssm_c_im, ssm_d, ssm_w_glu, ssm_w_out, att_w_in, att_w_out):
    bsz, seq, d_model = x.shape
    depth = norm_g.shape[0]
    bf = jnp.bfloat16
    h = x.reshape(bsz * seq, d_model)
    cos, sin = _rope_tables(seq)
    k_sel = min(TOPK_MAX, seq // 4)
    for i in range(depth):
        j = i // 2
        g = norm_g[i][:, None, :]
        if i % 2 == 0:
            ar, ai, bmat, cmat, dvec = _ssm_discretise(
                ssm_lam_re[j], ssm_lam_im[j], ssm_log_dt[j], ssm_b_re[j], ssm_b_im[j],
                ssm_c_re[j], ssm_c_im[j], ssm_d[j], bsz)
            mixed = _ssm_scan(h.reshape(bsz, seq, d_model), g[0], ssm_w_in[j].astype(bf), ar, ai, bmat, cmat, dvec)
            w_glu, w_out = ssm_w_glu[j].astype(bf), ssm_w_out[j].astype(bf)
        else:
            w_cat, n_q, n_kv, n_qi, n_idx = _att_weights(att_w_in[j], d_model)
            q, kx, vx, qi, kix, wi = _att_in(h, g[0], w_cat, cos, sin, bsz, seq, n_q, n_kv, n_qi,
                                             HEAD_DIM ** -0.5 * LOG2_E, n_idx ** -0.5 * IDX_DIM ** -0.5)
            mixed = _dsa(q, qi, wi, kx, vx, kix, bsz, seq, k_sel)
            w_glu, w_out = None, att_w_out[j].astype(bf)
        h = _mix_mlp(mixed, h, norm_g[i][1:], w_glu, w_out, mlp_w1[i].astype(bf), mlp_w2[i].astype(bf),
                     seq, time_major=(i % 2 == 0))
    return h.reshape(bsz, seq, d_model)
```

```python
import functools
import math

import jax
import jax.numpy as jnp
from jax import lax
from jax.experimental import pallas as pl
from jax.experimental.pallas import tpu as pltpu

NORM_EPS = 1e-6
HEAD_DIM = 64
KV_GROUP = 4
IDX_DIM = 64
TOPK_MAX = 256
ROPE_THETA = 10000.0

LANES = 128
SUBLANES = 8
MXU_DIM = 256
VMEM_LIMIT_BYTES = 56 * 1024 * 1024

TOKEN_TILE = 512
MLP_TOKEN_TILE = 1024
MLP_SUB_TILE = 512
FF_TILE = 1024
SCAN_CHUNK = 64
SCAN_LANES = 1024
SCAN_UNROLL = 2
Q_TILE = 256
K_TILE = 512
COARSE_ITERS = 9
RESIDUAL_ITERS = 8
BF16_ULP = 2.0 ** -7
F32_SLACK = 2.0 ** -21
FOLD_ROWS = 64
LOG2_E = math.log2(math.e)
BOUND_SLACK = 1.01
MIN_DENOMINATOR = 2.0 ** -60


def _params(*sem):
    return pltpu.CompilerParams(dimension_semantics=sem, vmem_limit_bytes=VMEM_LIMIT_BYTES)


def _rms(x, g):
    return x * lax.rsqrt(jnp.mean(x * x, axis=-1, keepdims=True) + NORM_EPS) * g


def _dot(a, b):
    return jnp.dot(a, b, preferred_element_type=jnp.float32)


def _dot_nt(a, b):
    return lax.dot_general(a, b, (((1,), (1,)), ((), ())), preferred_element_type=jnp.float32)


def _mix_mlp_kernel(x_ref, h_ref, g_ref, *rest, gated, sub, tf):
    if gated:
        wg_ref, wo_ref, w1_ref, w2_ref, o_ref = rest
    else:
        wo_ref, w1_ref, w2_ref, o_ref = rest
    g = g_ref[...]
    ff = w1_ref.shape[1]
    for r in range(h_ref.shape[0] // sub):
        rows = slice(r * sub, (r + 1) * sub)
        x = x_ref[rows, :]
        if gated:
            x = x * jax.nn.sigmoid(_dot(x.astype(jnp.bfloat16), wg_ref[...]))
        h = h_ref[rows, :] + _rms(_dot(x.astype(jnp.bfloat16), wo_ref[...]), g[0:1])
        xn = _rms(h, g[1:2]).astype(jnp.bfloat16)
        acc = None
        for c in range(ff // tf):
            cols = slice(c * tf, (c + 1) * tf)
            a = jnp.maximum(_dot(xn, w1_ref[:, cols]), 0.0)
            part = _dot((a * a).astype(jnp.bfloat16), w2_ref[cols, :])
            acc = part if acc is None else acc + part
        o_ref[rows, :] = h + _rms(acc, g[2:3])


def _mix_mlp(x, h, gains, w_glu, w_out, w1, w2, seq, time_major):
    t, d = h.shape
    ff = w1.shape[1]
    n = w_out.shape[0]
    tm = min(MLP_TOKEN_TILE, seq)
    nt = seq // tm
    gated = w_glu is not None
    kern = functools.partial(_mix_mlp_kernel, gated=gated, sub=min(MLP_SUB_TILE, tm), tf=min(FF_TILE, ff))
    resident = dict(pipeline_mode=pl.Buffered(1))
    whole = lambda a: pl.BlockSpec(a.shape, lambda i: (0, 0), **resident)
    x_map = (lambda i: (i % nt, i // nt)) if time_major else (lambda i: (i, 0))
    weights = ([w_glu] if gated else []) + [w_out, w1, w2]
    return pl.pallas_call(
        kern,
        out_shape=jax.ShapeDtypeStruct((t, d), jnp.float32),
        grid=(t // tm,),
        in_specs=[pl.BlockSpec((tm, n), x_map),
                  pl.BlockSpec((tm, d), lambda i: (i, 0)),
                  pl.BlockSpec(gains.shape, lambda i: (0, 0))] + [whole(w) for w in weights],
        out_specs=pl.BlockSpec((tm, d), lambda i: (i, 0)),
        compiler_params=_params("parallel"),
        name="mix_mlp",
    )(x, h, gains, *weights)


def _ssm_scan_kernel(h_ref, g_ref, w_ref, ar_ref, ai_ref, bmat_ref, cmat_ref, d_ref, z_ref, x_ref, st_ref, io_ref,
                     *, bsz, chunk, half, lanes):
    @pl.when(pl.program_id(0) == 0)
    def _():
        st_ref[...] = jnp.zeros_like(st_ref)

    e = d_ref.shape[1]
    n_blk = e // MXU_DIM
    sl = half // n_blk

    def state_lanes(j):
        return slice(j * sl, (j + 1) * sl), slice(half + j * sl, half + (j + 1) * sl)

    n_slab = e // LANES
    hb = h_ref[...]
    xn = _rms(hb.reshape(bsz * chunk, hb.shape[2]), g_ref[...]).astype(jnp.bfloat16)
    u_bt = _dot(xn, w_ref[...])
    for b in range(bsz):
        for c in range(n_slab):
            io_ref[c, pl.ds(b, chunk, stride=bsz), :] = u_bt[b * chunk:(b + 1) * chunk, c * LANES:(c + 1) * LANES]
    u = jnp.concatenate([io_ref[c] for c in range(n_slab)], axis=1)
    ub = u.astype(jnp.bfloat16)
    for j in range(n_blk):
        ch = slice(j * MXU_DIM, (j + 1) * MXU_DIM)
        for lanes_j in state_lanes(j):
            x_ref[:, lanes_j] = _dot(ub[:, ch], bmat_ref[ch, lanes_j])

    for c in range(half // lanes):
        re = slice(c * lanes, (c + 1) * lanes)
        im = slice(half + c * lanes, half + (c + 1) * lanes)
        ar = ar_ref[:, re]
        ai = ai_ref[:, re]

        def step(t, carry, re=re, im=im, ar=ar, ai=ai):
            xr, xi = carry
            rows = pl.ds(pl.multiple_of(t * bsz, bsz), bsz)
            nr = ar * xr - ai * xi + x_ref[rows, re]
            ni = ar * xi + ai * xr + x_ref[rows, im]
            x_ref[rows, re] = nr
            x_ref[rows, im] = ni
            return nr, ni

        xr, xi = lax.fori_loop(0, chunk, step, (st_ref[:, re], st_ref[:, im]), unroll=SCAN_UNROLL)
        st_ref[:, re] = xr
        st_ref[:, im] = xi

    for j in range(n_blk):
        ch = slice(j * MXU_DIM, (j + 1) * MXU_DIM)
        re_j, im_j = state_lanes(j)
        y = (_dot(x_ref[:, re_j].astype(jnp.bfloat16), cmat_ref[re_j, ch])
             + _dot(x_ref[:, im_j].astype(jnp.bfloat16), cmat_ref[im_j, ch])
             + d_ref[:, ch] * u[:, ch])
        z = jax.nn.gelu(y)
        for c in range(MXU_DIM // LANES):
            io_ref[j * (MXU_DIM // LANES) + c] = z[:, c * LANES:(c + 1) * LANES]
    for b in range(bsz):
        for c in range(n_slab):
            z_ref[:, b * e + c * LANES:b * e + (c + 1) * LANES] = io_ref[c, pl.ds(b, chunk, stride=bsz), :]


def _ssm_scan(h, g, w_in, ar, ai, bmat, cmat, dvec):
    bsz, seq, d = h.shape
    e = w_in.shape[1]
    n2 = bmat.shape[1]
    half = n2 // 2
    chunk = min(SCAN_CHUNK, seq)
    lanes = min(SCAN_LANES, half)
    kern = functools.partial(_ssm_scan_kernel, bsz=bsz, chunk=chunk, half=half, lanes=lanes)
    return pl.pallas_call(
        kern,
        out_shape=jax.ShapeDtypeStruct((seq, bsz * e), jnp.float32),
        grid=(seq // chunk,),
        in_specs=[
            pl.BlockSpec((bsz, chunk, d), lambda i: (0, i, 0)),
            pl.BlockSpec((1, d), lambda i: (0, 0)),
            pl.BlockSpec((d, e), lambda i: (0, 0)),
            pl.BlockSpec((bsz, half), lambda i: (0, 0)),
            pl.BlockSpec((bsz, half), lambda i: (0, 0)),
            pl.BlockSpec((e, n2), lambda i: (0, 0)),
            pl.BlockSpec((n2, e), lambda i: (0, 0)),
            pl.BlockSpec((1, e), lambda i: (0, 0)),
        ],
        out_specs=pl.BlockSpec((chunk, bsz * e), lambda i: (i, 0)),
        scratch_shapes=[pltpu.VMEM((chunk * bsz, n2), jnp.float32),
                        pltpu.VMEM((bsz, n2), jnp.float32),
                        pltpu.VMEM((e // LANES, chunk * bsz, LANES), jnp.float32)],
        compiler_params=_params("arbitrary"),
        name="ssm_scan",
    )(h, g, w_in, ar, ai, bmat, cmat, dvec)


def _ssm_discretise(lam_re, lam_im, log_dt, b_re, b_im, c_re, c_im, d_skip, bsz):
    g, p = lam_re.shape
    c = b_re.shape[2]
    dt = jnp.exp(log_dt)[:, None]
    mag = jnp.exp(lam_re * dt)
    abar_re = mag * jnp.cos(lam_im * dt)
    abar_im = mag * jnp.sin(lam_im * dt)
    den = lam_re * lam_re + lam_im * lam_im
    nr = abar_re - 1.0
    ni = abar_im
    fr = (nr * lam_re + ni * lam_im) / den
    fi = (ni * lam_re - nr * lam_im) / den
    bbar_re = fr[..., None] * b_re - fi[..., None] * b_im
    bbar_im = fr[..., None] * b_im + fi[..., None] * b_re
    bf = jnp.bfloat16
    state_of_col = jnp.arange(g * p) % p
    spread_p = (jnp.arange(p)[:, None] == state_of_col[None, :]).astype(bf)
    same_group = (jnp.arange(g * c)[:, None] // c) == (jnp.arange(g * p)[None, :] // p)

    def bd(m):
        rows = m.transpose(0, 2, 1).reshape(g * c, p).astype(bf)
        return jnp.where(same_group, jnp.dot(rows, spread_p, preferred_element_type=jnp.float32), 0.0).astype(bf)

    def cd(m):
        cols = m.transpose(2, 0, 1).reshape(p, g * c).astype(bf)
        return jnp.where(same_group.T, jnp.dot(spread_p.T, cols, preferred_element_type=jnp.float32), 0.0).astype(bf)

    bmat = jnp.concatenate([bd(bbar_re), bd(bbar_im)], axis=1)
    cmat = jnp.concatenate([cd(c_re), cd(-c_im)], axis=0)
    ar = jnp.broadcast_to(abar_re.reshape(1, g * p), (bsz, g * p))
    ai = jnp.broadcast_to(abar_im.reshape(1, g * p), (bsz, g * p))
    return ar, ai, bmat, cmat, d_skip.reshape(1, g * c)


def _rope128(x, cos, sin_signed, lane):
    swapped = jnp.where((lane % HEAD_DIM) < HEAD_DIM // 2,
                        pltpu.roll(x, LANES - HEAD_DIM // 2, axis=1),
                        pltpu.roll(x, HEAD_DIM // 2, axis=1))
    return x * cos + swapped * sin_signed


def _att_in_kernel(h_ref, g_ref, w_ref, cos_ref, sin_ref, q_ref, k_ref, v_ref, qi_ref, ki_ref, wi_ref,
                   *, n_q, n_kv, n_qi, q_scale, wi_scale):
    xn = _rms(h_ref[...], g_ref[...]).astype(jnp.bfloat16)
    proj = _dot(xn, w_ref[...])
    cos = cos_ref[...]
    sin = sin_ref[...]
    lane = lax.broadcasted_iota(jnp.int32, cos.shape, 1)
    is_head = lane < HEAD_DIM

    def chunk(off, c):
        return proj[:, off + c * LANES: off + (c + 1) * LANES]

    def head_of(x, n):
        return x if n % 2 == 0 else pltpu.roll(x, HEAD_DIM, axis=1)

    for c in range(n_q // LANES):
        q_ref[:, c * LANES:(c + 1) * LANES] = (_rope128(chunk(0, c), cos, sin, lane) * q_scale).astype(q_ref.dtype)
    off = n_q
    for n in range(n_kv):
        k2 = _rope128(chunk(off, n // 2), cos, sin, lane)
        kx = jnp.where(is_head, head_of(k2, n), jnp.where(lane == HEAD_DIM, 1.0, 0.0))
        k_ref[:, n * LANES:(n + 1) * LANES] = kx.astype(k_ref.dtype)
    off += n_kv * HEAD_DIM
    for n in range(n_kv):
        vx = jnp.where(is_head, head_of(chunk(off, n // 2), n), 1.0)
        v_ref[:, n * LANES:(n + 1) * LANES] = vx.astype(v_ref.dtype)
    off += n_kv * HEAD_DIM
    for c in range(n_qi // LANES):
        qi_ref[:, c * LANES:(c + 1) * LANES] = _rope128(chunk(off, c), cos, sin, lane).astype(qi_ref.dtype)
    off += n_qi
    last = chunk(off, 0)
    ki_lo = jnp.where(is_head, _rope128(last, cos, sin, lane), 0.0)
    ki_ref[:, :LANES] = ki_lo.astype(ki_ref.dtype)
    ki_ref[:, LANES:] = pltpu.roll(ki_lo, HEAD_DIM, axis=1).astype(ki_ref.dtype)
    wi_ref[...] = pltpu.roll(last, HEAD_DIM, axis=1) * wi_scale


def _att_in(h, g, w, cos, sin, bsz, seq, n_q, n_kv, n_qi, q_scale, wi_scale):
    d = h.shape[1]
    n_k = n_v = n_kv * LANES
    n_ki = 2 * LANES
    ncols = w.shape[1]
    tm = min(TOKEN_TILE, seq)
    nt = seq // tm
    t = bsz * seq
    row = lambda b, i: (b * nt + i, 0)
    kern = functools.partial(_att_in_kernel, n_q=n_q, n_kv=n_kv, n_qi=n_qi, q_scale=q_scale, wi_scale=wi_scale)
    bf = jnp.bfloat16
    return pl.pallas_call(
        kern,
        out_shape=[jax.ShapeDtypeStruct((t, n_q), bf), jax.ShapeDtypeStruct((t, n_k), bf),
                   jax.ShapeDtypeStruct((t, n_v), bf), jax.ShapeDtypeStruct((t, n_qi), bf),
                   jax.ShapeDtypeStruct((t, n_ki), bf), jax.ShapeDtypeStruct((t, LANES), jnp.float32)],
        grid=(bsz, nt),
        in_specs=[
            pl.BlockSpec((tm, d), row),
            pl.BlockSpec((1, d), lambda b, i: (0, 0)),
            pl.BlockSpec((d, ncols), lambda b, i: (0, 0)),
            pl.BlockSpec((tm, LANES), lambda b, i: (i, 0)),
            pl.BlockSpec((tm, LANES), lambda b, i: (i, 0)),
        ],
        out_specs=[pl.BlockSpec((tm, n_q), row), pl.BlockSpec((tm, n_k), row), pl.BlockSpec((tm, n_v), row),
                   pl.BlockSpec((tm, n_qi), row), pl.BlockSpec((tm, n_ki), row), pl.BlockSpec((tm, LANES), row)],
        compiler_params=_params("parallel", "parallel"),
        name="att_in",
    )(h, g, w, cos, sin)


def _row_fold(x, op):
    out = x[:FOLD_ROWS]
    for j in range(1, x.shape[0] // FOLD_ROWS):
        out = op(out, x[j * FOLD_ROWS:(j + 1) * FOLD_ROWS])
    return out


def _dsa_kernel(q_ref, qi_ref, wi_ref, k_ref, v_ref, ki_ref, o_ref,
                sc_ref, sct_ref, sct16_ref, stage_ref, wb_ref, qx_ref, acc_ref, m_ref, kn_ref,
                *, tq, tk, k_sel, n_heads, n_idx):
    i = pl.program_id(1)
    n_kv = n_heads // KV_GROUP
    lane = lax.broadcasted_iota(jnp.int32, (tq, LANES), 1)
    is_head = lane < HEAD_DIM

    @pl.when(i == 0)
    def _():
        def body(r, c):
            x = k_ref[pl.ds(pl.multiple_of(r * tk, tk), tk), :].astype(jnp.float32)
            return jnp.maximum(c, jnp.max(x * x, axis=0, keepdims=True))
        sq_max = lax.fori_loop(0, k_ref.shape[0] // tk, body, jnp.zeros((1, k_ref.shape[1]), jnp.float32))
        lane_k = lax.broadcasted_iota(jnp.int32, (1, LANES), 1)
        head_lane = lax.broadcasted_iota(jnp.int32, kn_ref.shape, 1)
        kn = jnp.zeros(kn_ref.shape, jnp.float32)
        for n in range(n_kv):
            chunk = jnp.where(lane_k < HEAD_DIM, sq_max[:, n * LANES:(n + 1) * LANES], 0.0)
            kn = jnp.where(head_lane // KV_GROUP == n, jnp.sum(chunk, axis=1, keepdims=True), kn)
        kn_ref[...] = kn

    n_kt = (i * tq + tq + tk - 1) // tk
    neg_inf = jnp.float32(-jnp.inf)
    row = i * tq + lax.broadcasted_iota(jnp.int32, (tq, tk), 0)
    col0 = lax.broadcasted_iota(jnp.int32, (tq, tk), 1)
    wi = wi_ref[...]
    for hh in range(n_idx):
        wb_ref[hh] = jnp.broadcast_to(wi[:, hh:hh + 1], (tq, LANES))

    def raw_scores(kt):
        ks = pl.ds(pl.multiple_of(kt * tk, tk), tk)
        acc = jnp.zeros((tq, tk), jnp.float32)
        for hh in range(n_idx):
            qc = qi_ref[:, (hh // 2) * LANES:(hh // 2 + 1) * LANES]
            kc = ki_ref[ks, (hh % 2) * LANES:(hh % 2 + 1) * LANES]
            acc = acc + jnp.maximum(_dot_nt(qc, kc), 0.0) * jnp.concatenate([wb_ref[hh]] * (tk // LANES), axis=1)
        return acc

    def mask_tile(kt, carry):
        rmax, rmin = carry
        causal = (col0 + kt * tk) <= row
        masked = jnp.where(causal, stage_ref[...], neg_inf)
        sc_ref[kt] = masked
        masked_t = masked.T
        sct_ref[kt] = masked_t
        sct16_ref[kt] = masked_t.astype(jnp.bfloat16)
        rmax = jnp.maximum(rmax, _row_fold(masked_t, jnp.maximum))
        rmin = jnp.minimum(rmin, _row_fold(jnp.where(masked_t == neg_inf, -neg_inf, masked_t), jnp.minimum))
        return rmax, rmin

    def score_step(kt, carry):
        carry = mask_tile(kt - 1, carry)
        stage_ref[...] = raw_scores(kt)
        return carry

    stage_ref[...] = raw_scores(0)
    extremes = lax.fori_loop(1, n_kt, score_step,
                             (jnp.full((FOLD_ROWS, tq), neg_inf), jnp.full((FOLD_ROWS, tq), -neg_inf)))
    rmax, rmin = mask_tile(n_kt - 1, extremes)

    kf = jnp.float32(k_sel)

    def sweep(fn, init, x):
        def body(kt, c):
            for j in range(tk // FOLD_ROWS):
                c = fn(c, sct_ref[kt, j * FOLD_ROWS:(j + 1) * FOLD_ROWS, :], x)
            return c
        return lax.fori_loop(0, n_kt, body, tuple(jnp.full((FOLD_ROWS, tq), v, jnp.float32) for v in init))

    def col_sum(part):
        return jnp.sum(part, axis=0, keepdims=True)

    def col_max(part):
        return jnp.max(part, axis=0, keepdims=True)

    def count_ge_coarse(x16):
        one, zero = jnp.bfloat16(1), jnp.bfloat16(0)

        def body(kt, c):
            for j in range(tk // FOLD_ROWS):
                c = c + jnp.where(sct16_ref[kt, j * FOLD_ROWS:(j + 1) * FOLD_ROWS, :] >= x16, one, zero)
            return c
        part = lax.fori_loop(0, n_kt, body, jnp.zeros((FOLD_ROWS, tq), jnp.bfloat16))
        return col_sum(part.astype(jnp.float32))

    def bisect_coarse(_, carry):
        lo, hi = carry
        mid16 = (0.5 * lo + 0.5 * hi).astype(jnp.bfloat16)
        mid = mid16.astype(jnp.float32)
        ge = count_ge_coarse(mid16) >= kf
        return jnp.where(ge, mid, lo), jnp.where(ge, hi, mid)

    rmin = -col_max(-rmin)
    rmax = col_max(rmax)
    lo, hi = lax.fori_loop(0, COARSE_ITERS, bisect_coarse, (rmin, rmax))
    lo = lo - (jnp.abs(lo) * BF16_ULP + jnp.float32(1e-30))

    def rebase(kt, c):
        for j in range(tk // FOLD_ROWS):
            rows = slice(j * FOLD_ROWS, (j + 1) * FOLD_ROWS)
            sct16_ref[kt, rows, :] = (sct_ref[kt, rows, :] - lo).astype(jnp.bfloat16)
        return c

    lax.fori_loop(0, n_kt, rebase, 0)
    width = (hi - lo) * (1.0 + BF16_ULP)
    dlo, dhi = lax.fori_loop(0, RESIDUAL_ITERS, bisect_coarse, (jnp.zeros_like(width), width))
    slack = (jnp.abs(lo) + dhi) * F32_SLACK
    hi = jnp.where(dhi < width, jnp.minimum(hi, lo + dhi + slack), hi)

    few = (i * tq + lax.broadcasted_iota(jnp.int32, (1, tq), 1)) < k_sel

    def max_le(x):
        (c,) = sweep(lambda c, s, x: (jnp.maximum(c[0], jnp.where(s <= x, s, neg_inf)),), (-jnp.inf,), x)
        return col_max(c)

    def probe(x):
        def fn(c, s, x):
            ge = s >= x
            return c[0] + jnp.where(ge, 1.0, 0.0), jnp.maximum(c[1], jnp.where(ge, neg_inf, s))
        cnt, nxt = sweep(fn, (0.0, -jnp.inf), x)
        return col_sum(cnt), col_max(nxt)

    cand0 = jnp.where(few, rmin, max_le(hi))
    cnt0, nxt0 = probe(cand0)

    def unresolved(cnt):
        return jnp.logical_and(jnp.logical_not(few), cnt < kf)

    def finish_cond(c):
        _, cnt, _ = c
        return jnp.max(jnp.where(unresolved(cnt), 1.0, 0.0)) > 0.0

    def finish_body(c):
        cand, cnt, nxt = c
        cand = jnp.where(unresolved(cnt), nxt, cand)
        cnt, nxt = probe(cand)
        return cand, cnt, nxt

    thr_t, cnt_ge, _ = lax.while_loop(finish_cond, finish_body, (cand0, cnt0, nxt0))

    def to_rows(x):
        return jnp.broadcast_to(x, (LANES, tq)).T

    thr = to_rows(thr_t)

    tied_t = jnp.logical_and(jnp.logical_not(few), cnt_ge > kf)

    @pl.when(jnp.max(jnp.where(tied_t, 1.0, 0.0)) > 0.0)
    def _():
        (c,) = sweep(lambda c, s, x: (c[0] + jnp.where(s > x, 1.0, 0.0),), (0.0,), thr_t)
        need = to_rows(kf - col_sum(c))[:, :1]
        thr1 = thr[:, :1]
        tied1 = to_rows(jnp.where(tied_t, 1.0, 0.0))[:, :1] > 0.0
        tri = (lax.broadcasted_iota(jnp.int32, (tk, tk), 0)
               <= lax.broadcasted_iota(jnp.int32, (tk, tk), 1)).astype(jnp.bfloat16)

        def drop_body(kt, run):
            s = sc_ref[kt]
            eq = jnp.logical_and(s == thr1, tied1)
            eqf = jnp.where(eq, 1.0, 0.0)
            rank = run + _dot(eqf.astype(jnp.bfloat16), tri)
            sc_ref[kt] = jnp.where(jnp.logical_and(eq, rank > need), neg_inf, s)
            return run + jnp.sum(eqf, axis=1, keepdims=True)

        lax.fori_loop(0, n_kt, drop_body, jnp.zeros((tq, 1), jnp.float32))

    qf = q_ref[...].astype(jnp.float32)
    n_q = qf.shape[1]
    head_of_col = lax.broadcasted_iota(jnp.int32, (n_q, LANES), 0) // HEAD_DIM
    head_sel = jnp.where(head_of_col == lax.broadcasted_iota(jnp.int32, (n_q, LANES), 1), 1.0, 0.0)
    qss = _dot((qf * qf).astype(jnp.bfloat16), head_sel.astype(jnp.bfloat16))
    bound = jnp.sqrt(jnp.max(qss * kn_ref[0:1, :], axis=1, keepdims=True)) * BOUND_SLACK
    shift = jnp.where(lane == HEAD_DIM, -bound, 0.0)
    for c in range(n_heads // 2):
        qc = qf[:, c * LANES:(c + 1) * LANES]
        for half in range(2):
            h = 2 * c + half
            n, g = h // KV_GROUP, h % KV_GROUP
            x = qc if half == 0 else pltpu.roll(qc, HEAD_DIM, axis=1)
            qx_ref[n, g * tq:(g + 1) * tq, :] = jnp.where(is_head, x, shift).astype(qx_ref.dtype)

    thr_tile = jnp.concatenate([thr] * (tk // LANES), axis=1)

    def attend(online):
        acc_ref[...] = jnp.zeros_like(acc_ref)
        if online:
            m_ref[...] = jnp.full_like(m_ref, neg_inf)

        def att_block(kt, width):
            ks = pl.ds(pl.multiple_of(kt * tk, tk), width)
            keep = sc_ref[kt, :, :width] >= thr_tile[:, :width]
            keep_b = jnp.where(keep, 1.0, 0.0).astype(jnp.bfloat16)
            for n in range(n_kv):
                s = _dot_nt(qx_ref[n], k_ref[ks, n * LANES:(n + 1) * LANES])
                vc = v_ref[ks, n * LANES:(n + 1) * LANES]
                if online:
                    s = jnp.where(jnp.concatenate([keep] * KV_GROUP, axis=0), s, neg_inf)
                    m_old = m_ref[n]
                    m_new = jnp.maximum(m_old, jnp.max(s, axis=1, keepdims=True))
                    m_safe = jnp.where(m_new == neg_inf, 0.0, m_new)
                    p = jnp.exp2(s - m_safe).astype(jnp.bfloat16)
                    acc_ref[n] = jnp.exp2(m_old - m_safe) * acc_ref[n] + _dot(p, vc)
                    m_ref[n] = m_new
                else:
                    p = jnp.exp2(s).astype(jnp.bfloat16).reshape(KV_GROUP, tq, width) * keep_b[None]
                    acc_ref[n] += _dot(p.reshape(KV_GROUP * tq, width), vc)

        def att_tile(kt, carry):
            att_block(kt, tk)
            return carry

        n_full = (i * tq + tq) // tk
        lax.fori_loop(0, n_full, att_tile, 0)
        rest = (i * tq + tq) % tk
        for part in range(1, tk // tq):
            @pl.when(rest == part * tq)
            def _(part=part):
                att_block(n_full, part * tq)

    attend(False)
    den = acc_ref[0]
    for n in range(1, n_kv):
        den = jnp.minimum(den, acc_ref[n])
    den_lane = lax.broadcasted_iota(jnp.int32, den.shape, 1) >= HEAD_DIM
    safe = jnp.min(jnp.where(den_lane, den, 1.0)) > MIN_DENOMINATOR

    @pl.when(jnp.logical_not(safe))
    def _():
        attend(True)

    for c in range(n_heads // 2):
        a0, a1 = (acc_ref[h // KV_GROUP, (h % KV_GROUP) * tq:(h % KV_GROUP + 1) * tq, :] for h in (2 * c, 2 * c + 1))
        a1r = pltpu.roll(a1, HEAD_DIM, axis=1)
        num = jnp.where(is_head, a0, a1r)
        den = jnp.where(is_head, pltpu.roll(a0, HEAD_DIM, axis=1), a1)
        o_ref[:, c * LANES:(c + 1) * LANES] = (num / den).astype(o_ref.dtype)


def _dsa(q, qi, wi, kx, vx, kix, bsz, seq, k_sel):
    t, n_q = q.shape
    n_heads = n_q // HEAD_DIM
    n_kv = n_heads // KV_GROUP
    n_idx = qi.shape[1] // IDX_DIM
    tq = min(Q_TILE, seq)
    tk = min(K_TILE, seq)
    nq = seq // tq
    assert seq % tk == 0 and tk % tq == 0 and tk % FOLD_ROWS == 0 and tq % LANES == 0
    assert seq // FOLD_ROWS <= 256, "bf16 partial counts must stay exactly representable"
    row = lambda b, i: (b * nq + i, 0)
    per_batch = lambda b, i: (b, 0)
    kern = functools.partial(_dsa_kernel, tq=tq, tk=tk, k_sel=k_sel, n_heads=n_heads, n_idx=n_idx)
    return pl.pallas_call(
        kern,
        out_shape=jax.ShapeDtypeStruct((t, n_q), jnp.bfloat16),
        grid=(bsz, nq),
        in_specs=[
            pl.BlockSpec((tq, n_q), row),
            pl.BlockSpec((tq, qi.shape[1]), row),
            pl.BlockSpec((tq, LANES), row),
            pl.BlockSpec((seq, kx.shape[1]), per_batch),
            pl.BlockSpec((seq, vx.shape[1]), per_batch),
            pl.BlockSpec((seq, kix.shape[1]), per_batch),
        ],
        out_specs=pl.BlockSpec((tq, n_q), row),
        scratch_shapes=[
            pltpu.VMEM((seq // tk, tq, tk), jnp.float32),
            pltpu.VMEM((seq // tk, tk, tq), jnp.float32),
            pltpu.VMEM((seq // tk, tk, tq), jnp.bfloat16),
            pltpu.VMEM((tq, tk), jnp.float32),
            pltpu.VMEM((n_idx, tq, LANES), jnp.float32),
            pltpu.VMEM((n_kv, KV_GROUP * tq, LANES), jnp.bfloat16),
            pltpu.VMEM((n_kv, KV_GROUP * tq, LANES), jnp.float32),
            pltpu.VMEM((n_kv, KV_GROUP * tq, 1), jnp.float32),
            pltpu.VMEM((SUBLANES, LANES), jnp.float32),
        ],
        compiler_params=_params("parallel", "arbitrary"),
        name="dsa",
    )(q, qi, wi, kx, vx, kix)


def _rope_tables(seq):
    half = HEAD_DIM // 2
    inv_freq = ROPE_THETA ** (-jnp.arange(half, dtype=jnp.float32) * 2.0 / HEAD_DIM)
    ang = jnp.arange(seq, dtype=jnp.float32)[:, None] * inv_freq[None, :]
    cos = jnp.concatenate([jnp.cos(ang)] * 4, axis=-1)
    sin = jnp.concatenate([-jnp.sin(ang), jnp.sin(ang)] * 2, axis=-1)
    return cos, sin


def _att_weights(w_in, d_model):
    n_heads = d_model // HEAD_DIM
    n_kv = n_heads // KV_GROUP
    n_idx = max(4, d_model // 128)
    n_q, n_qi = n_heads * HEAD_DIM, n_idx * IDX_DIM
    cols = n_q + 2 * n_kv * HEAD_DIM + n_qi + IDX_DIM + n_idx
    assert w_in.shape[1] == cols and IDX_DIM + n_idx <= LANES and n_kv % 2 == 0
    padded = -(-cols // LANES) * LANES
    w_cat = jnp.pad(w_in, ((0, 0), (0, padded - cols))).astype(jnp.bfloat16)
    return w_cat, n_q, n_kv, n_qi, n_idx


def kernel(x, norm_g, mlp_w1, mlp_w2, ssm_w_in, ssm_lam_re, ssm_lam_im, ssm_log_dt, ssm_b_re, ssm_b_im,
           ssm_c_re, ssm_c_im, ssm_d, ssm_w_glu, ssm_w_out, att_w_in, att_w_out):
    bsz, seq, d_model = x.shape
    depth = norm_g.shape[0]
    bf = jnp.bfloat16
    h = x.reshape(bsz * seq, d_model)
    cos, sin = _rope_tables(seq)
    k_sel = min(TOPK_MAX, seq // 4)
    for i in range(depth):
        j = i // 2
        g = norm_g[i][:, None, :]
        if i % 2 == 0:
            ar, ai, bmat, cmat, dvec = _ssm_discretise(
                ssm_lam_re[j], ssm_lam_im[j], ssm_log_dt[j], ssm_b_re[j], ssm_b_im[j],
                ssm_c_re[j], ssm_c_im[j], ssm_d[j], bsz)
            mixed = _ssm_scan(h.reshape(bsz, seq, d_model), g[0], ssm_w_in[j].astype(bf), ar, ai, bmat, cmat, dvec)
            w_glu, w_out = ssm_w_glu[j].astype(bf), ssm_w_out[j].astype(bf)
        else:
            w_cat, n_q, n_kv, n_qi, n_idx = _att_weights(att_w_in[j], d_model)
            q, kx, vx, qi, kix, wi = _att_in(h, g[0], w_cat, cos, sin, bsz, seq, n_q, n_kv, n_qi,
                                             HEAD_DIM ** -0.5 * LOG2_E, n_idx ** -0.5 * IDX_DIM ** -0.5)
            mixed = _dsa(q, qi, wi, kx, vx, kix, bsz, seq, k_sel)
            w_glu, w_out = None, att_w_out[j].astype(bf)
        h = _mix_mlp(mixed, h, norm_g[i][1:], w_glu, w_out, mlp_w1[i].astype(bf), mlp_w2[i].astype(bf),
                     seq, time_major=(i % 2 == 0))
    return h.reshape(bsz, seq, d_model)
```

```python
import functools
import math

import jax
import jax.numpy as jnp
from jax import lax
from jax.experimental import pallas as pl
from jax.experimental.pallas import tpu as pltpu

NORM_EPS = 1e-6
HEAD_DIM = 64
KV_GROUP = 4
IDX_DIM = 64
TOPK_MAX = 256
ROPE_THETA = 10000.0

LANES = 128
SUBLANES = 8
MXU_DIM = 256
VMEM_LIMIT_BYTES = 56 * 1024 * 1024

TOKEN_TILE = 512
MLP_TOKEN_TILE = 1024
MLP_SUB_TILE = 512
FF_TILE = 1024
SCAN_CHUNK = 64
SCAN_LANES = 1024
SCAN_UNROLL = 2
Q_TILE = 256
K_TILE = 512
COARSE_ITERS = 9
RESIDUAL_ITERS = 8
BF16_ULP = 2.0 ** -7
F32_SLACK = 2.0 ** -21
FOLD_ROWS = 64
LOG2_E = math.log2(math.e)
BOUND_SLACK = 1.01
MIN_DENOMINATOR = 2.0 ** -60


def _params(*sem):
    return pltpu.CompilerParams(dimension_semantics=sem, vmem_limit_bytes=VMEM_LIMIT_BYTES)


def _rms(x, g):
    return x * lax.rsqrt(jnp.mean(x * x, axis=-1, keepdims=True) + NORM_EPS) * g


def _dot(a, b):
    return jnp.dot(a, b, preferred_element_type=jnp.float32)


def _dot_nt(a, b):
    return lax.dot_general(a, b, (((1,), (1,)), ((), ())), preferred_element_type=jnp.float32)


def _mix_mlp_kernel(x_ref, h_ref, g_ref, *rest, gated, sub, tf):
    if gated:
        wg_ref, wo_ref, w1_ref, w2_ref, o_ref = rest
    else:
        wo_ref, w1_ref, w2_ref, o_ref = rest
    g = g_ref[...]
    ff = w1_ref.shape[1]
    for r in range(h_ref.shape[0] // sub):
        rows = slice(r * sub, (r + 1) * sub)
        x = x_ref[rows, :]
        if gated:
            x = x * jax.nn.sigmoid(_dot(x.astype(jnp.bfloat16), wg_ref[...]))
        h = h_ref[rows, :] + _rms(_dot(x.astype(jnp.bfloat16), wo_ref[...]), g[0:1])
        xn = _rms(h, g[1:2]).astype(jnp.bfloat16)
        acc = None
        for c in range(ff // tf):
            cols = slice(c * tf, (c + 1) * tf)
            a = jnp.maximum(_dot(xn, w1_ref[:, cols]), 0.0)
            part = _dot((a * a).astype(jnp.bfloat16), w2_ref[cols, :])
            acc = part if acc is None else acc + part
        o_ref[rows, :] = h + _rms(acc, g[2:3])


def _mix_mlp(x, h, gains, w_glu, w_out, w1, w2, seq, time_major):
    t, d = h.shape
    ff = w1.shape[1]
    n = w_out.shape[0]
    tm = min(MLP_TOKEN_TILE, seq)
    nt = seq // tm
    gated = w_glu is not None
    kern = functools.partial(_mix_mlp_kernel, gated=gated, sub=min(MLP_SUB_TILE, tm), tf=min(FF_TILE, ff))
    resident = dict(pipeline_mode=pl.Buffered(1))
    whole = lambda a: pl.BlockSpec(a.shape, lambda i: (0, 0), **resident)
    x_map = (lambda i: (i % nt, i // nt)) if time_major else (lambda i: (i, 0))
    weights = ([w_glu] if gated else []) + [w_out, w1, w2]
    return pl.pallas_call(
        kern,
        out_shape=jax.ShapeDtypeStruct((t, d), jnp.float32),
        grid=(t // tm,),
        in_specs=[pl.BlockSpec((tm, n), x_map),
                  pl.BlockSpec((tm, d), lambda i: (i, 0)),
                  pl.BlockSpec(gains.shape, lambda i: (0, 0))] + [whole(w) for w in weights],
        out_specs=pl.BlockSpec((tm, d), lambda i: (i, 0)),
        compiler_params=_params("parallel"),
        name="mix_mlp",
    )(x, h, gains, *weights)


def _ssm_scan_kernel(h_ref, g_ref, w_ref, ar_ref, ai_ref, bmat_ref, cmat_ref, d_ref, z_ref, x_ref, st_ref, io_ref,
                     *, bsz, chunk, half, lanes):
    @pl.when(pl.program_id(0) == 0)
    def _():
        st_ref[...] = jnp.zeros_like(st_ref)

    e = d_ref.shape[1]
    n_blk = e // MXU_DIM
    sl = half // n_blk

    def state_lanes(j):
        return slice(j * sl, (j + 1) * sl), slice(half + j * sl, half + (j + 1) * sl)

    n_slab = e // LANES
    hb = h_ref[...]
    xn = _rms(hb.reshape(bsz * chunk, hb.shape[2]), g_ref[...]).astype(jnp.bfloat16)
    u_bt = _dot(xn, w_ref[...])
    for b in range(bsz):
        for c in range(n_slab):
            io_ref[c, pl.ds(b, chunk, stride=bsz), :] = u_bt[b * chunk:(b + 1) * chunk, c * LANES:(c + 1) * LANES]
    u = jnp.concatenate([io_ref[c] for c in range(n_slab)], axis=1)
    ub = u.astype(jnp.bfloat16)
    for j in range(n_blk):
        ch = slice(j * MXU_DIM, (j + 1) * MXU_DIM)
        for lanes_j in state_lanes(j):
            x_ref[:, lanes_j] = _dot(ub[:, ch], bmat_ref[ch, lanes_j])

    for c in range(half // lanes):
        re = slice(c * lanes, (c + 1) * lanes)
        im = slice(half + c * lanes, half + (c + 1) * lanes)
        ar = ar_ref[:, re]
        ai = ai_ref[:, re]

        def step(t, carry, re=re, im=im, ar=ar, ai=ai):
            xr, xi = carry
            rows = pl.ds(pl.multiple_of(t * bsz, bsz), bsz)
            nr = ar * xr - ai * xi + x_ref[rows, re]
            ni = ar * xi + ai * xr + x_ref[rows, im]
            x_ref[rows, re] = nr
            x_ref[rows, im] = ni
            return nr, ni

        xr, xi = lax.fori_loop(0, chunk, step, (st_ref[:, re], st_ref[:, im]), unroll=SCAN_UNROLL)
        st_ref[:, re] = xr
        st_ref[:, im] = xi

    for j in range(n_blk):
        ch = slice(j * MXU_DIM, (j + 1) * MXU_DIM)
        re_j, im_j = state_lanes(j)
        y = (_dot(x_ref[:, re_j].astype(jnp.bfloat16), cmat_ref[re_j, ch])
             + _dot(x_ref[:, im_j].astype(jnp.bfloat16), cmat_ref[im_j, ch])
             + d_ref[:, ch] * u[:, ch])
        z = jax.nn.gelu(y)
        for c in range(MXU_DIM // LANES):
            io_ref[j * (MXU_DIM // LANES) + c] = z[:, c * LANES:(c + 1) * LANES]
    for b in range(bsz):
        for c in range(n_slab):
            z_ref[:, b * e + c * LANES:b * e + (c + 1) * LANES] = io_ref[c, pl.ds(b, chunk, stride=bsz), :]


def _ssm_scan(h, g, w_in, ar, ai, bmat, cmat, dvec):
    bsz, seq, d = h.shape
    e = w_in.shape[1]
    n2 = bmat.shape[1]
    half = n2 // 2
    chunk = min(SCAN_CHUNK, seq)
    lanes = min(SCAN_LANES, half)
    kern = functools.partial(_ssm_scan_kernel, bsz=bsz, chunk=chunk, half=half, lanes=lanes)
    return pl.pallas_call(
        kern,
        out_shape=jax.ShapeDtypeStruct((seq, bsz * e), jnp.float32),
        grid=(seq // chunk,),
        in_specs=[
            pl.BlockSpec((bsz, chunk, d), lambda i: (0, i, 0)),
            pl.BlockSpec((1, d), lambda i: (0, 0)),
            pl.BlockSpec((d, e), lambda i: (0, 0)),
            pl.BlockSpec((bsz, half), lambda i: (0, 0)),
            pl.BlockSpec((bsz, half), lambda i: (0, 0)),
            pl.BlockSpec((e, n2), lambda i: (0, 0)),
            pl.BlockSpec((n2, e), lambda i: (0, 0)),
            pl.BlockSpec((1, e), lambda i: (0, 0)),
        ],
        out_specs=pl.BlockSpec((chunk, bsz * e), lambda i: (i, 0)),
        scratch_shapes=[pltpu.VMEM((chunk * bsz, n2), jnp.float32),
                        pltpu.VMEM((bsz, n2), jnp.float32),
                        pltpu.VMEM((e // LANES, chunk * bsz, LANES), jnp.float32)],
        compiler_params=_params("arbitrary"),
        name="ssm_scan",
    )(h, g, w_in, ar, ai, bmat, cmat, dvec)


def _ssm_discretise(lam_re, lam_im, log_dt, b_re, b_im, c_re, c_im, d_skip, bsz):
    g, p = lam_re.shape
    c = b_re.shape[2]
    dt = jnp.exp(log_dt)[:, None]
    mag = jnp.exp(lam_re * dt)
    abar_re = mag * jnp.cos(lam_im * dt)
    abar_im = mag * jnp.sin(lam_im * dt)
    den = lam_re * lam_re + lam_im * lam_im
    nr = abar_re - 1.0
    ni = abar_im
    fr = (nr * lam_re + ni * lam_im) / den
    fi = (ni * lam_re - nr * lam_im) / den
    bbar_re = fr[..., None] * b_re - fi[..., None] * b_im
    bbar_im = fr[..., None] * b_im + fi[..., None] * b_re
    bf = jnp.bfloat16
    state_of_col = jnp.arange(g * p) % p
    spread_p = (jnp.arange(p)[:, None] == state_of_col[None, :]).astype(bf)
    same_group = (jnp.arange(g * c)[:, None] // c) == (jnp.arange(g * p)[None, :] // p)

    def bd(m):
        rows = m.transpose(0, 2, 1).reshape(g * c, p).astype(bf)
        return jnp.where(same_group, jnp.dot(rows, spread_p, preferred_element_type=jnp.float32), 0.0).astype(bf)

    def cd(m):
        cols = m.transpose(2, 0, 1).reshape(p, g * c).astype(bf)
        return jnp.where(same_group.T, jnp.dot(spread_p.T, cols, preferred_element_type=jnp.float32), 0.0).astype(bf)

    bmat = jnp.concatenate([bd(bbar_re), bd(bbar_im)], axis=1)
    cmat = jnp.concatenate([cd(c_re), cd(-c_im)], axis=0)
    ar = jnp.broadcast_to(abar_re.reshape(1, g * p), (bsz, g * p))
    ai = jnp.broadcast_to(abar_im.reshape(1, g * p), (bsz, g * p))
    return ar, ai, bmat, cmat, d_skip.reshape(1, g * c)


def _rope128(x, cos, sin_signed, lane):
    swapped = jnp.where((lane % HEAD_DIM) < HEAD_DIM // 2,
                        pltpu.roll(x, LANES - HEAD_DIM // 2, axis=1),
                        pltpu.roll(x, HEAD_DIM // 2, axis=1))
    return x * cos + swapped * sin_signed


def _att_in_kernel(h_ref, g_ref, w_ref, cos_ref, sin_ref, q_ref, k_ref, v_ref, qi_ref, ki_ref, wi_ref,
                   *, n_q, n_kv, n_qi, q_scale, wi_scale):
    xn = _rms(h_ref[...], g_ref[...]).astype(jnp.bfloat16)
    proj = _dot(xn, w_ref[...])
    cos = cos_ref[...]
    sin = sin_ref[...]
    lane = lax.broadcasted_iota(jnp.int32, cos.shape, 1)
    is_head = lane < HEAD_DIM

    def chunk(off, c):
        return proj[:, off + c * LANES: off + (c + 1) * LANES]

    def head_of(x, n):
        return x if n % 2 == 0 else pltpu.roll(x, HEAD_DIM, axis=1)

    for c in range(n_q // LANES):
        q_ref[:, c * LANES:(c + 1) * LANES] = (_rope128(chunk(0, c), cos, sin, lane) * q_scale).astype(q_ref.dtype)
    off = n_q
    for n in range(n_kv):
        k2 = _rope128(chunk(off, n // 2), cos, sin, lane)
        kx = jnp.where(is_head, head_of(k2, n), jnp.where(lane == HEAD_DIM, 1.0, 0.0))
        k_ref[:, n * LANES:(n + 1) * LANES] = kx.astype(k_ref.dtype)
    off += n_kv * HEAD_DIM
    for n in range(n_kv):
        vx = jnp.where(is_head, head_of(chunk(off, n // 2), n), 1.0)
        v_ref[:, n * LANES:(n + 1) * LANES] = vx.astype(v_ref.dtype)
    off += n_kv * HEAD_DIM
    for c in range(n_qi // LANES):
        qi_ref[:, c * LANES:(c + 1) * LANES] = _rope128(chunk(off, c), cos, sin, lane).astype(qi_ref.dtype)
    off += n_qi
    last = chunk(off, 0)
    ki_lo = jnp.where(is_head, _rope128(last, cos, sin, lane), 0.0)
    ki_ref[:, :LANES] = ki_lo.astype(ki_ref.dtype)
    ki_ref[:, LANES:] = pltpu.roll(ki_lo, HEAD_DIM, axis=1).astype(ki_ref.dtype)
    wi_ref[...] = pltpu.roll(last, HEAD_DIM, axis=1) * wi_scale


def _att_in(h, g, w, cos, sin, bsz, seq, n_q, n_kv, n_qi, q_scale, wi_scale):
    d = h.shape[1]
    n_k = n_v = n_kv * LANES
    n_ki = 2 * LANES
    ncols = w.shape[1]
    tm = min(TOKEN_TILE, seq)
    nt = seq // tm
    t = bsz * seq
    row = lambda b, i: (b * nt + i, 0)
    kern = functools.partial(_att_in_kernel, n_q=n_q, n_kv=n_kv, n_qi=n_qi, q_scale=q_scale, wi_scale=wi_scale)
    bf = jnp.bfloat16
    return pl.pallas_call(
        kern,
        out_shape=[jax.ShapeDtypeStruct((t, n_q), bf), jax.ShapeDtypeStruct((t, n_k), bf),
                   jax.ShapeDtypeStruct((t, n_v), bf), jax.ShapeDtypeStruct((t, n_qi), bf),
                   jax.ShapeDtypeStruct((t, n_ki), bf), jax.ShapeDtypeStruct((t, LANES), jnp.float32)],
        grid=(bsz, nt),
        in_specs=[
            pl.BlockSpec((tm, d), row),
            pl.BlockSpec((1, d), lambda b, i: (0, 0)),
            pl.BlockSpec((d, ncols), lambda b, i: (0, 0)),
            pl.BlockSpec((tm, LANES), lambda b, i: (i, 0)),
            pl.BlockSpec((tm, LANES), lambda b, i: (i, 0)),
        ],
        out_specs=[pl.BlockSpec((tm, n_q), row), pl.BlockSpec((tm, n_k), row), pl.BlockSpec((tm, n_v), row),
                   pl.BlockSpec((tm, n_qi), row), pl.BlockSpec((tm, n_ki), row), pl.BlockSpec((tm, LANES), row)],
        compiler_params=_params("parallel", "parallel"),
        name="att_in",
    )(h, g, w, cos, sin)


def _row_fold(x, op):
    out = x[:FOLD_ROWS]
    for j in range(1, x.shape[0] // FOLD_ROWS):
        out = op(out, x[j * FOLD_ROWS:(j + 1) * FOLD_ROWS])
    return out


def _dsa_kernel(q_ref, qi_ref, wi_ref, k_ref, v_ref, ki_ref, o_ref,
                sc_ref, sct_ref, sct16_ref, stage_ref, wb_ref, qx_ref, acc_ref, m_ref, kn_ref,
                *, tq, tk, k_sel, n_heads, n_idx):
    i = pl.program_id(1)
    n_kv = n_heads // KV_GROUP
    lane = lax.broadcasted_iota(jnp.int32, (tq, LANES), 1)
    is_head = lane < HEAD_DIM

    @pl.when(i == 0)
    def _():
        def body(r, c):
            x = k_ref[pl.ds(pl.multiple_of(r * tk, tk), tk), :].astype(jnp.float32)
            return jnp.maximum(c, jnp.max(x * x, axis=0, keepdims=True))
        sq_max = lax.fori_loop(0, k_ref.shape[0] // tk, body, jnp.zeros((1, k_ref.shape[1]), jnp.float32))
        lane_k = lax.broadcasted_iota(jnp.int32, (1, LANES), 1)
        head_lane = lax.broadcasted_iota(jnp.int32, kn_ref.shape, 1)
        kn = jnp.zeros(kn_ref.shape, jnp.float32)
        for n in range(n_kv):
            chunk = jnp.where(lane_k < HEAD_DIM, sq_max[:, n * LANES:(n + 1) * LANES], 0.0)
            kn = jnp.where(head_lane // KV_GROUP == n, jnp.sum(chunk, axis=1, keepdims=True), kn)
        kn_ref[...] = kn

    n_kt = (i * tq + tq + tk - 1) // tk
    neg_inf = jnp.float32(-jnp.inf)
    row = i * tq + lax.broadcasted_iota(jnp.int32, (tq, tk), 0)
    col0 = lax.broadcasted_iota(jnp.int32, (tq, tk), 1)
    wi = wi_ref[...]
    for hh in range(n_idx):
        wb_ref[hh] = jnp.broadcast_to(wi[:, hh:hh + 1], (tq, LANES))

    def raw_scores(kt):
        ks = pl.ds(pl.multiple_of(kt * tk, tk), tk)
        acc = jnp.zeros((tq, tk), jnp.float32)
        for hh in range(n_idx):
            qc = qi_ref[:, (hh // 2) * LANES:(hh // 2 + 1) * LANES]
            kc = ki_ref[ks, (hh % 2) * LANES:(hh % 2 + 1) * LANES]
            acc = acc + jnp.maximum(_dot_nt(qc, kc), 0.0) * jnp.concatenate([wb_ref[hh]] * (tk // LANES), axis=1)
        return acc

    def mask_tile(kt, carry):
        rmax, rmin = carry
        causal = (col0 + kt * tk) <= row
        masked = jnp.where(causal, stage_ref[...], neg_inf)
        sc_ref[kt] = masked
        masked_t = masked.T
        sct_ref[kt] = masked_t
        sct16_ref[kt] = masked_t.astype(jnp.bfloat16)
        rmax = jnp.maximum(rmax, _row_fold(masked_t, jnp.maximum))
        rmin = jnp.minimum(rmin, _row_fold(jnp.where(masked_t == neg_inf, -neg_inf, masked_t), jnp.minimum))
        return rmax, rmin

    def score_step(kt, carry):
        carry = mask_tile(kt - 1, carry)
        stage_ref[...] = raw_scores(kt)
        return carry

    stage_ref[...] = raw_scores(0)
    extremes = lax.fori_loop(1, n_kt, score_step,
                             (jnp.full((FOLD_ROWS, tq), neg_inf), jnp.full((FOLD_ROWS, tq), -neg_inf)))
    rmax, rmin = mask_tile(n_kt - 1, extremes)

    kf = jnp.float32(k_sel)

    def sweep(fn, init, x):
        def body(kt, c):
            for j in range(tk // FOLD_ROWS):
                c = fn(c, sct_ref[kt, j * FOLD_ROWS:(j + 1) * FOLD_ROWS, :], x)
            return c
        return lax.fori_loop(0, n_kt, body, tuple(jnp.full((FOLD_ROWS, tq), v, jnp.float32) for v in init))

    def col_sum(part):
        return jnp.sum(part, axis=0, keepdims=True)

    def col_max(part):
        return jnp.max(part, axis=0, keepdims=True)

    def count_ge_coarse(x16):
        one, zero = jnp.bfloat16(1), jnp.bfloat16(0)

        def body(kt, c):
            for j in range(tk // FOLD_ROWS):
                c = c + jnp.where(sct16_ref[kt, j * FOLD_ROWS:(j + 1) * FOLD_ROWS, :] >= x16, one, zero)
            return c
        part = lax.fori_loop(0, n_kt, body, jnp.zeros((FOLD_ROWS, tq), jnp.bfloat16))
        return col_sum(part.astype(jnp.float32))

    def bisect_coarse(_, carry):
        lo, hi = carry
        mid16 = (0.5 * lo + 0.5 * hi).astype(jnp.bfloat16)
        mid = mid16.astype(jnp.float32)
        ge = count_ge_coarse(mid16) >= kf
        return jnp.where(ge, mid, lo), jnp.where(ge, hi, mid)

    rmin = -col_max(-rmin)
    rmax = col_max(rmax)
    lo, hi = lax.fori_loop(0, COARSE_ITERS, bisect_coarse, (rmin, rmax))
    lo = lo - (jnp.abs(lo) * BF16_ULP + jnp.float32(1e-30))

    def rebase(kt, c):
        for j in range(tk // FOLD_ROWS):
            rows = slice(j * FOLD_ROWS, (j + 1) * FOLD_ROWS)
            sct16_ref[kt, rows, :] = (sct_ref[kt, rows, :] - lo).astype(jnp.bfloat16)
        return c

    lax.fori_loop(0, n_kt, rebase, 0)
    width = (hi - lo) * (1.0 + BF16_ULP)
    dlo, dhi = lax.fori_loop(0, RESIDUAL_ITERS, bisect_coarse, (jnp.zeros_like(width), width))
    slack = (jnp.abs(lo) + dhi) * F32_SLACK
    hi = jnp.where(dhi < width, jnp.minimum(hi, lo + dhi + slack), hi)

    few = (i * tq + lax.broadcasted_iota(jnp.int32, (1, tq), 1)) < k_sel

    def max_le(x):
        (c,) = sweep(lambda c, s, x: (jnp.maximum(c[0], jnp.where(s <= x, s, neg_inf)),), (-jnp.inf,), x)
        return col_max(c)

    def probe(x):
        def fn(c, s, x):
            ge = s >= x
            return c[0] + jnp.where(ge, 1.0, 0.0), jnp.maximum(c[1], jnp.where(ge, neg_inf, s))
        cnt, nxt = sweep(fn, (0.0, -jnp.inf), x)
        return col_sum(cnt), col_max(nxt)

    cand0 = jnp.where(few, rmin, max_le(hi))
    cnt0, nxt0 = probe(cand0)

    def unresolved(cnt):
        return jnp.logical_and(jnp.logical_not(few), cnt < kf)

    def finish_cond(c):
        _, cnt, _ = c
        return jnp.max(jnp.where(unresolved(cnt), 1.0, 0.0)) > 0.0

    def finish_body(c):
        cand, cnt, nxt = c
        cand = jnp.where(unresolved(cnt), nxt, cand)
        cnt, nxt = probe(cand)
        return cand, cnt, nxt

    thr_t, cnt_ge, _ = lax.while_loop(finish_cond, finish_body, (cand0, cnt0, nxt0))

    def to_rows(x):
        return jnp.broadcast_to(x, (LANES, tq)).T

    thr = to_rows(thr_t)

    tied_t = jnp.logical_and(jnp.logical_not(few), cnt_ge > kf)

    @pl.when(jnp.max(jnp.where(tied_t, 1.0, 0.0)) > 0.0)
    def _():
        (c,) = sweep(lambda c, s, x: (c[0] + jnp.where(s > x, 1.0, 0.0),), (0.0,), thr_t)
        need = to_rows(kf - col_sum(c))[:, :1]
        thr1 = thr[:, :1]
        tied1 = to_rows(jnp.where(tied_t, 1.0, 0.0))[:, :1] > 0.0
        tri = (lax.broadcasted_iota(jnp.int32, (tk, tk), 0)
               <= lax.broadcasted_iota(jnp.int32, (tk, tk), 1)).astype(jnp.bfloat16)

        def drop_body(kt, run):
            s = sc_ref[kt]
            eq = jnp.logical_and(s == thr1, tied1)
            eqf = jnp.where(eq, 1.0, 0.0)
            rank = run + _dot(eqf.astype(jnp.bfloat16), tri)
            sc_ref[kt] = jnp.where(jnp.logical_and(eq, rank > need), neg_inf, s)
            return run + jnp.sum(eqf, axis=1, keepdims=True)

        lax.fori_loop(0, n_kt, drop_body, jnp.zeros((tq, 1), jnp.float32))

    qf = q_ref[...].astype(jnp.float32)
    n_q = qf.shape[1]
    head_of_col = lax.broadcasted_iota(jnp.int32, (n_q, LANES), 0) // HEAD_DIM
    head_sel = jnp.where(head_of_col == lax.broadcasted_iota(jnp.int32, (n_q, LANES), 1), 1.0, 0.0)
    qss = _dot((qf * qf).astype(jnp.bfloat16), head_sel.astype(jnp.bfloat16))
    bound = jnp.sqrt(jnp.max(qss * kn_ref[0:1, :], axis=1, keepdims=True)) * BOUND_SLACK
    shift = jnp.where(lane == HEAD_DIM, -bound, 0.0)
    for c in range(n_heads // 2):
        qc = qf[:, c * LANES:(c + 1) * LANES]
        for half in range(2):
            h = 2 * c + half
            n, g = h // KV_GROUP, h % KV_GROUP
            x = qc if half == 0 else pltpu.roll(qc, HEAD_DIM, axis=1)
            qx_ref[n, g * tq:(g + 1) * tq, :] = jnp.where(is_head, x, shift).astype(qx_ref.dtype)

    thr_tile = jnp.concatenate([thr] * (tk // LANES), axis=1)

    def attend(online):
        acc_ref[...] = jnp.zeros_like(acc_ref)
        if online:
            m_ref[...] = jnp.full_like(m_ref, neg_inf)

        def att_block(kt, width):
            ks = pl.ds(pl.multiple_of(kt * tk, tk), width)
            keep = sc_ref[kt, :, :width] >= thr_tile[:, :width]
            keep_b = jnp.where(keep, 1.0, 0.0).astype(jnp.bfloat16)
            for n in range(n_kv):
                s = _dot_nt(qx_ref[n], k_ref[ks, n * LANES:(n + 1) * LANES])
                vc = v_ref[ks, n * LANES:(n + 1) * LANES]
                if online:
                    s = jnp.where(jnp.concatenate([keep] * KV_GROUP, axis=0), s, neg_inf)
                    m_old = m_ref[n]
                    m_new = jnp.maximum(m_old, jnp.max(s, axis=1, keepdims=True))
                    m_safe = jnp.where(m_new == neg_inf, 0.0, m_new)
                    p = jnp.exp2(s - m_safe).astype(jnp.bfloat16)
                    acc_ref[n] = jnp.exp2(m_old - m_safe) * acc_ref[n] + _dot(p, vc)
                    m_ref[n] = m_new
                else:
                    p = jnp.exp2(s).astype(jnp.bfloat16).reshape(KV_GROUP, tq, width) * keep_b[None]
                    acc_ref[n] += _dot(p.reshape(KV_GROUP * tq, width), vc)

        def att_tile(kt, carry):
            att_block(kt, tk)
            return carry

        n_full = (i * tq + tq) // tk
        lax.fori_loop(0, n_full, att_tile, 0)
        rest = (i * tq + tq) % tk
        for part in range(1, tk // tq):
            @pl.when(rest == part * tq)
            def _(part=part):
                att_block(n_full, part * tq)

    def normalise():
        least = None
        for c in range(n_heads // 2):
            a0, a1 = (acc_ref[h // KV_GROUP, (h % KV_GROUP) * tq:(h % KV_GROUP + 1) * tq, :]
                      for h in (2 * c, 2 * c + 1))
            num = jnp.where(is_head, a0, pltpu.roll(a1, HEAD_DIM, axis=1))
            den = jnp.where(is_head, pltpu.roll(a0, HEAD_DIM, axis=1), a1)
            o_ref[:, c * LANES:(c + 1) * LANES] = (num / den).astype(o_ref.dtype)
            least = den if least is None else jnp.minimum(least, den)
        return jnp.min(least)

    attend(False)

    @pl.when(jnp.logical_not(normalise() > MIN_DENOMINATOR))
    def _():
        attend(True)
        normalise()


def _dsa(q, qi, wi, kx, vx, kix, bsz, seq, k_sel):
    t, n_q = q.shape
    n_heads = n_q // HEAD_DIM
    n_kv = n_heads // KV_GROUP
    n_idx = qi.shape[1] // IDX_DIM
    tq = min(Q_TILE, seq)
    tk = min(K_TILE, seq)
    nq = seq // tq
    assert seq % tk == 0 and tk % tq == 0 and tk % FOLD_ROWS == 0 and tq % LANES == 0
    assert seq // FOLD_ROWS <= 256, "bf16 partial counts must stay exactly representable"
    row = lambda b, i: (b * nq + i, 0)
    per_batch = lambda b, i: (b, 0)
    kern = functools.partial(_dsa_kernel, tq=tq, tk=tk, k_sel=k_sel, n_heads=n_heads, n_idx=n_idx)
    return pl.pallas_call(
        kern,
        out_shape=jax.ShapeDtypeStruct((t, n_q), jnp.bfloat16),
        grid=(bsz, nq),
        in_specs=[
            pl.BlockSpec((tq, n_q), row),
            pl.BlockSpec((tq, qi.shape[1]), row),
            pl.BlockSpec((tq, LANES), row),
            pl.BlockSpec((seq, kx.shape[1]), per_batch),
            pl.BlockSpec((seq, vx.shape[1]), per_batch),
            pl.BlockSpec((seq, kix.shape[1]), per_batch),
        ],
        out_specs=pl.BlockSpec((tq, n_q), row),
        scratch_shapes=[
            pltpu.VMEM((seq // tk, tq, tk), jnp.float32),
            pltpu.VMEM((seq // tk, tk, tq), jnp.float32),
            pltpu.VMEM((seq // tk, tk, tq), jnp.bfloat16),
            pltpu.VMEM((tq, tk), jnp.float32),
            pltpu.VMEM((n_idx, tq, LANES), jnp.float32),
            pltpu.VMEM((n_kv, KV_GROUP * tq, LANES), jnp.bfloat16),
            pltpu.VMEM((n_kv, KV_GROUP * tq, LANES), jnp.float32),
            pltpu.VMEM((n_kv, KV_GROUP * tq, 1), jnp.float32),
            pltpu.VMEM((SUBLANES, LANES), jnp.float32),
        ],
        compiler_params=_params("parallel", "arbitrary"),
        name="dsa",
    )(q, qi, wi, kx, vx, kix)


def _rope_tables(seq):
    half = HEAD_DIM // 2
    inv_freq = ROPE_THETA ** (-jnp.arange(half, dtype=jnp.float32) * 2.0 / HEAD_DIM)
    ang = jnp.arange(seq, dtype=jnp.float32)[:, None] * inv_freq[None, :]
    cos = jnp.concatenate([jnp.cos(ang)] * 4, axis=-1)
    sin = jnp.concatenate([-jnp.sin(ang), jnp.sin(ang)] * 2, axis=-1)
    return cos, sin


def _att_weights(w_in, d_model):
    n_heads = d_model // HEAD_DIM
    n_kv = n_heads // KV_GROUP
    n_idx = max(4, d_model // 128)
    n_q, n_qi = n_heads * HEAD_DIM, n_idx * IDX_DIM
    cols = n_q + 2 * n_kv * HEAD_DIM + n_qi + IDX_DIM + n_idx
    assert w_in.shape[1] == cols and IDX_DIM + n_idx <= LANES and n_kv % 2 == 0
    padded = -(-cols // LANES) * LANES
    w_cat = jnp.pad(w_in, ((0, 0), (0, padded - cols))).astype(jnp.bfloat16)
    return w_cat, n_q, n_kv, n_qi, n_idx


def kernel(x, norm_g, mlp_w1, mlp_w2, ssm_w_in, ssm_lam_re, ssm_lam_im, ssm_log_dt, ssm_b_re, ssm_b_im,
           ssm_c_re, ssm_c_im, ssm_d, ssm_w_glu, ssm_w_out, att_w_in, att_w_out):
    bsz, seq, d_model = x.shape
    depth = norm_g.shape[0]
    bf = jnp.bfloat16
    h = x.reshape(bsz * seq, d_model)
    cos, sin = _rope_tables(seq)
    k_sel = min(TOPK_MAX, seq // 4)
    for i in range(depth):
        j = i // 2
        g = norm_g[i][:, None, :]
        if i % 2 == 0:
            ar, ai, bmat, cmat, dvec = _ssm_discretise(
                ssm_lam_re[j], ssm_lam_im[j], ssm_log_dt[j], ssm_b_re[j], ssm_b_im[j],
                ssm_c_re[j], ssm_c_im[j], ssm_d[j], bsz)
            mixed = _ssm_scan(h.reshape(bsz, seq, d_model), g[0], ssm_w_in[j].astype(bf), ar, ai, bmat, cmat, dvec)
            w_glu, w_out = ssm_w_glu[j].astype(bf), ssm_w_out[j].astype(bf)
        else:
            w_cat, n_q, n_kv, n_qi, n_idx = _att_weights(att_w_in[j], d_model)
            q, kx, vx, qi, kix, wi = _att_in(h, g[0], w_cat, cos, sin, bsz, seq, n_q, n_kv, n_qi,
                                             HEAD_DIM ** -0.5 * LOG2_E, n_idx ** -0.5 * IDX_DIM ** -0.5)
            mixed = _dsa(q, qi, wi, kx, vx, kix, bsz, seq, k_sel)
            w_glu, w_out = None, att_w_out[j].astype(bf)
        h = _mix_mlp(mixed, h, norm_g[i][1:], w_glu, w_out, mlp_w1[i].astype(bf), mlp_w2[i].astype(bf),
                     seq, time_major=(i % 2 == 0))
    return h.reshape(bsz, seq, d_model)
```

```python
import functools
import math

import jax
import jax.numpy as jnp
from jax import lax
from jax.experimental import pallas as pl
from jax.experimental.pallas import tpu as pltpu

NORM_EPS = 1e-6
HEAD_DIM = 64
KV_GROUP = 4
IDX_DIM = 64
TOPK_MAX = 256
ROPE_THETA = 10000.0

LANES = 128
SUBLANES = 8
MXU_DIM = 256
VMEM_LIMIT_BYTES = 56 * 1024 * 1024

TOKEN_TILE = 1024
MLP_TOKEN_TILE = 1024
MLP_SUB_TILE = 512
FF_TILE = 1024
SCAN_CHUNK = 128
SCAN_LANES = 1024
SCAN_UNROLL = 2
Q_TILE = 256
K_TILE = 512
COARSE_ITERS = 9
RESIDUAL_ITERS = 8
BF16_ULP = 2.0 ** -7
F32_SLACK = 2.0 ** -21
FOLD_ROWS = 64
LOG2_E = math.log2(math.e)
BOUND_SLACK = 1.01
MIN_DENOMINATOR = 2.0 ** -60


def _params(*sem):
    return pltpu.CompilerParams(dimension_semantics=sem, vmem_limit_bytes=VMEM_LIMIT_BYTES)


def _rms(x, g):
    return x * lax.rsqrt(jnp.mean(x * x, axis=-1, keepdims=True) + NORM_EPS) * g


def _dot(a, b):
    return jnp.dot(a, b, preferred_element_type=jnp.float32)


def _dot_nt(a, b):
    return lax.dot_general(a, b, (((1,), (1,)), ((), ())), preferred_element_type=jnp.float32)


def _mix_mlp_kernel(x_ref, h_ref, g_ref, *rest, gated, sub, tf):
    if gated:
        wg_ref, wo_ref, w1_ref, w2_ref, o_ref = rest
    else:
        wo_ref, w1_ref, w2_ref, o_ref = rest
    g = g_ref[...]
    ff = w1_ref.shape[1]
    for r in range(h_ref.shape[0] // sub):
        rows = slice(r * sub, (r + 1) * sub)
        x = x_ref[rows, :]
        if gated:
            x = x * jax.nn.sigmoid(_dot(x.astype(jnp.bfloat16), wg_ref[...]))
        h = h_ref[rows, :] + _rms(_dot(x.astype(jnp.bfloat16), wo_ref[...]), g[0:1])
        xn = _rms(h, g[1:2]).astype(jnp.bfloat16)
        acc = None
        for c in range(ff // tf):
            cols = slice(c * tf, (c + 1) * tf)
            a = jnp.maximum(_dot(xn, w1_ref[:, cols]), 0.0)
            part = _dot((a * a).astype(jnp.bfloat16), w2_ref[cols, :])
            acc = part if acc is None else acc + part
        o_ref[rows, :] = h + _rms(acc, g[2:3])


def _mix_mlp(x, h, gains, w_glu, w_out, w1, w2, seq, time_major):
    t, d = h.shape
    ff = w1.shape[1]
    n = w_out.shape[0]
    tm = min(MLP_TOKEN_TILE, seq)
    nt = seq // tm
    gated = w_glu is not None
    kern = functools.partial(_mix_mlp_kernel, gated=gated, sub=min(MLP_SUB_TILE, tm), tf=min(FF_TILE, ff))
    resident = dict(pipeline_mode=pl.Buffered(1))
    whole = lambda a: pl.BlockSpec(a.shape, lambda i: (0, 0), **resident)
    x_map = (lambda i: (i % nt, i // nt)) if time_major else (lambda i: (i, 0))
    weights = ([w_glu] if gated else []) + [w_out, w1, w2]
    return pl.pallas_call(
        kern,
        out_shape=jax.ShapeDtypeStruct((t, d), jnp.float32),
        grid=(t // tm,),
        in_specs=[pl.BlockSpec((tm, n), x_map),
                  pl.BlockSpec((tm, d), lambda i: (i, 0)),
                  pl.BlockSpec(gains.shape, lambda i: (0, 0))] + [whole(w) for w in weights],
        out_specs=pl.BlockSpec((tm, d), lambda i: (i, 0)),
        compiler_params=_params("parallel"),
        name="mix_mlp",
    )(x, h, gains, *weights)


def _ssm_scan_kernel(h_ref, g_ref, w_ref, ar_ref, ai_ref, bmat_ref, cmat_ref, d_ref, z_ref, x_ref, st_ref, io_ref,
                     *, bsz, chunk, half, lanes):
    @pl.when(pl.program_id(0) == 0)
    def _():
        st_ref[...] = jnp.zeros_like(st_ref)

    e = d_ref.shape[1]
    n_blk = e // MXU_DIM
    sl = half // n_blk

    def state_lanes(j):
        return slice(j * sl, (j + 1) * sl), slice(half + j * sl, half + (j + 1) * sl)

    n_slab = e // LANES
    hb = h_ref[...]
    xn = _rms(hb.reshape(bsz * chunk, hb.shape[2]), g_ref[...]).astype(jnp.bfloat16)
    u_bt = _dot(xn, w_ref[...])
    for b in range(bsz):
        for c in range(n_slab):
            io_ref[c, pl.ds(b, chunk, stride=bsz), :] = u_bt[b * chunk:(b + 1) * chunk, c * LANES:(c + 1) * LANES]
    u = jnp.concatenate([io_ref[c] for c in range(n_slab)], axis=1)
    ub = u.astype(jnp.bfloat16)
    for j in range(n_blk):
        ch = slice(j * MXU_DIM, (j + 1) * MXU_DIM)
        for lanes_j in state_lanes(j):
            x_ref[:, lanes_j] = _dot(ub[:, ch], bmat_ref[ch, lanes_j])

    for c in range(half // lanes):
        re = slice(c * lanes, (c + 1) * lanes)
        im = slice(half + c * lanes, half + (c + 1) * lanes)
        ar = ar_ref[:, re]
        ai = ai_ref[:, re]

        def step(t, carry, re=re, im=im, ar=ar, ai=ai):
            xr, xi = carry
            rows = pl.ds(pl.multiple_of(t * bsz, bsz), bsz)
            nr = ar * xr - ai * xi + x_ref[rows, re]
            ni = ar * xi + ai * xr + x_ref[rows, im]
            x_ref[rows, re] = nr
            x_ref[rows, im] = ni
            return nr, ni

        xr, xi = lax.fori_loop(0, chunk, step, (st_ref[:, re], st_ref[:, im]), unroll=SCAN_UNROLL)
        st_ref[:, re] = xr
        st_ref[:, im] = xi

    for j in range(n_blk):
        ch = slice(j * MXU_DIM, (j + 1) * MXU_DIM)
        re_j, im_j = state_lanes(j)
        y = (_dot(x_ref[:, re_j].astype(jnp.bfloat16), cmat_ref[re_j, ch])
             + _dot(x_ref[:, im_j].astype(jnp.bfloat16), cmat_ref[im_j, ch])
             + d_ref[:, ch] * u[:, ch])
        z = jax.nn.gelu(y)
        for c in range(MXU_DIM // LANES):
            io_ref[j * (MXU_DIM // LANES) + c] = z[:, c * LANES:(c + 1) * LANES]
    for b in range(bsz):
        for c in range(n_slab):
            z_ref[:, b * e + c * LANES:b * e + (c + 1) * LANES] = io_ref[c, pl.ds(b, chunk, stride=bsz), :]


def _ssm_scan(h, g, w_in, ar, ai, bmat, cmat, dvec):
    bsz, seq, d = h.shape
    e = w_in.shape[1]
    n2 = bmat.shape[1]
    half = n2 // 2
    chunk = min(SCAN_CHUNK, seq)
    lanes = min(SCAN_LANES, half)
    kern = functools.partial(_ssm_scan_kernel, bsz=bsz, chunk=chunk, half=half, lanes=lanes)
    return pl.pallas_call(
        kern,
        out_shape=jax.ShapeDtypeStruct((seq, bsz * e), jnp.float32),
        grid=(seq // chunk,),
        in_specs=[
            pl.BlockSpec((bsz, chunk, d), lambda i: (0, i, 0)),
            pl.BlockSpec((1, d), lambda i: (0, 0)),
            pl.BlockSpec((d, e), lambda i: (0, 0)),
            pl.BlockSpec((bsz, half), lambda i: (0, 0)),
            pl.BlockSpec((bsz, half), lambda i: (0, 0)),
            pl.BlockSpec((e, n2), lambda i: (0, 0)),
            pl.BlockSpec((n2, e), lambda i: (0, 0)),
            pl.BlockSpec((1, e), lambda i: (0, 0)),
        ],
        out_specs=pl.BlockSpec((chunk, bsz * e), lambda i: (i, 0)),
        scratch_shapes=[pltpu.VMEM((chunk * bsz, n2), jnp.float32),
                        pltpu.VMEM((bsz, n2), jnp.float32),
                        pltpu.VMEM((e // LANES, chunk * bsz, LANES), jnp.float32)],
        compiler_params=_params("arbitrary"),
        name="ssm_scan",
    )(h, g, w_in, ar, ai, bmat, cmat, dvec)


def _ssm_discretise(lam_re, lam_im, log_dt, b_re, b_im, c_re, c_im, d_skip, bsz):
    g, p = lam_re.shape
    c = b_re.shape[2]
    dt = jnp.exp(log_dt)[:, None]
    mag = jnp.exp(lam_re * dt)
    abar_re = mag * jnp.cos(lam_im * dt)
    abar_im = mag * jnp.sin(lam_im * dt)
    den = lam_re * lam_re + lam_im * lam_im
    nr = abar_re - 1.0
    ni = abar_im
    fr = (nr * lam_re + ni * lam_im) / den
    fi = (ni * lam_re - nr * lam_im) / den
    bbar_re = fr[..., None] * b_re - fi[..., None] * b_im
    bbar_im = fr[..., None] * b_im + fi[..., None] * b_re
    bf = jnp.bfloat16
    state_of_col = jnp.arange(g * p) % p
    spread_p = (jnp.arange(p)[:, None] == state_of_col[None, :]).astype(bf)
    same_group = (jnp.arange(g * c)[:, None] // c) == (jnp.arange(g * p)[None, :] // p)

    def bd(m):
        rows = m.transpose(0, 2, 1).reshape(g * c, p).astype(bf)
        return jnp.where(same_group, jnp.dot(rows, spread_p, preferred_element_type=jnp.float32), 0.0).astype(bf)

    def cd(m):
        cols = m.transpose(2, 0, 1).reshape(p, g * c).astype(bf)
        return jnp.where(same_group.T, jnp.dot(spread_p.T, cols, preferred_element_type=jnp.float32), 0.0).astype(bf)

    bmat = jnp.concatenate([bd(bbar_re), bd(bbar_im)], axis=1)
    cmat = jnp.concatenate([cd(c_re), cd(-c_im)], axis=0)
    ar = jnp.broadcast_to(abar_re.reshape(1, g * p), (bsz, g * p))
    ai = jnp.broadcast_to(abar_im.reshape(1, g * p), (bsz, g * p))
    return ar, ai, bmat, cmat, d_skip.reshape(1, g * c)


def _rope128(x, cos, sin_signed, lane):
    swapped = jnp.where((lane % HEAD_DIM) < HEAD_DIM // 2,
                        pltpu.roll(x, LANES - HEAD_DIM // 2, axis=1),
                        pltpu.roll(x, HEAD_DIM // 2, axis=1))
    return x * cos + swapped * sin_signed


def _att_in_kernel(h_ref, g_ref, w_ref, cos_ref, sin_ref, q_ref, k_ref, v_ref, qi_ref, ki_ref, wi_ref,
                   *, n_q, n_kv, n_qi, q_scale, wi_scale):
    xn = _rms(h_ref[...], g_ref[...]).astype(jnp.bfloat16)
    proj = _dot(xn, w_ref[...])
    cos = cos_ref[...]
    sin = sin_ref[...]
    lane = lax.broadcasted_iota(jnp.int32, cos.shape, 1)
    is_head = lane < HEAD_DIM

    def chunk(off, c):
        return proj[:, off + c * LANES: off + (c + 1) * LANES]

    def head_of(x, n):
        return x if n % 2 == 0 else pltpu.roll(x, HEAD_DIM, axis=1)

    for c in range(n_q // LANES):
        q_ref[:, c * LANES:(c + 1) * LANES] = (_rope128(chunk(0, c), cos, sin, lane) * q_scale).astype(q_ref.dtype)
    off = n_q
    for n in range(n_kv):
        k2 = _rope128(chunk(off, n // 2), cos, sin, lane)
        kx = jnp.where(is_head, head_of(k2, n), jnp.where(lane == HEAD_DIM, 1.0, 0.0))
        k_ref[:, n * LANES:(n + 1) * LANES] = kx.astype(k_ref.dtype)
    off += n_kv * HEAD_DIM
    for n in range(n_kv):
        vx = jnp.where(is_head, head_of(chunk(off, n // 2), n), 1.0)
        v_ref[:, n * LANES:(n + 1) * LANES] = vx.astype(v_ref.dtype)
    off += n_kv * HEAD_DIM
    for c in range(n_qi // LANES):
        qi_ref[:, c * LANES:(c + 1) * LANES] = _rope128(chunk(off, c), cos, sin, lane).astype(qi_ref.dtype)
    off += n_qi
    last = chunk(off, 0)
    ki_lo = jnp.where(is_head, _rope128(last, cos, sin, lane), 0.0)
    ki_ref[:, :LANES] = ki_lo.astype(ki_ref.dtype)
    ki_ref[:, LANES:] = pltpu.roll(ki_lo, HEAD_DIM, axis=1).astype(ki_ref.dtype)
    wi_ref[...] = pltpu.roll(last, HEAD_DIM, axis=1) * wi_scale


def _att_in(h, g, w, cos, sin, bsz, seq, n_q, n_kv, n_qi, q_scale, wi_scale):
    d = h.shape[1]
    n_k = n_v = n_kv * LANES
    n_ki = 2 * LANES
    ncols = w.shape[1]
    tm = min(TOKEN_TILE, seq)
    nt = seq // tm
    t = bsz * seq
    row = lambda b, i: (b * nt + i, 0)
    kern = functools.partial(_att_in_kernel, n_q=n_q, n_kv=n_kv, n_qi=n_qi, q_scale=q_scale, wi_scale=wi_scale)
    bf = jnp.bfloat16
    return pl.pallas_call(
        kern,
        out_shape=[jax.ShapeDtypeStruct((t, n_q), bf), jax.ShapeDtypeStruct((t, n_k), bf),
                   jax.ShapeDtypeStruct((t, n_v), bf), jax.ShapeDtypeStruct((t, n_qi), bf),
                   jax.ShapeDtypeStruct((t, n_ki), bf), jax.ShapeDtypeStruct((t, LANES), jnp.float32)],
        grid=(bsz, nt),
        in_specs=[
            pl.BlockSpec((tm, d), row),
            pl.BlockSpec((1, d), lambda b, i: (0, 0)),
            pl.BlockSpec((d, ncols), lambda b, i: (0, 0)),
            pl.BlockSpec((tm, LANES), lambda b, i: (i, 0)),
            pl.BlockSpec((tm, LANES), lambda b, i: (i, 0)),
        ],
        out_specs=[pl.BlockSpec((tm, n_q), row), pl.BlockSpec((tm, n_k), row), pl.BlockSpec((tm, n_v), row),
                   pl.BlockSpec((tm, n_qi), row), pl.BlockSpec((tm, n_ki), row), pl.BlockSpec((tm, LANES), row)],
        compiler_params=_params("parallel", "parallel"),
        name="att_in",
    )(h, g, w, cos, sin)


def _row_fold(x, op):
    out = x[:FOLD_ROWS]
    for j in range(1, x.shape[0] // FOLD_ROWS):
        out = op(out, x[j * FOLD_ROWS:(j + 1) * FOLD_ROWS])
    return out


def _dsa_kernel(q_ref, qi_ref, wi_ref, k_ref, v_ref, ki_ref, o_ref,
                sc_ref, sct_ref, sct16_ref, stage_ref, wb_ref, qx_ref, acc_ref, m_ref, kn_ref,
                *, tq, tk, k_sel, n_heads, n_idx):
    i = pl.program_id(1)
    n_kv = n_heads // KV_GROUP
    lane = lax.broadcasted_iota(jnp.int32, (tq, LANES), 1)
    is_head = lane < HEAD_DIM

    @pl.when(i == 0)
    def _():
        def body(r, c):
            x = k_ref[pl.ds(pl.multiple_of(r * tk, tk), tk), :].astype(jnp.float32)
            return jnp.maximum(c, jnp.max(x * x, axis=0, keepdims=True))
        sq_max = lax.fori_loop(0, k_ref.shape[0] // tk, body, jnp.zeros((1, k_ref.shape[1]), jnp.float32))
        lane_k = lax.broadcasted_iota(jnp.int32, (1, LANES), 1)
        head_lane = lax.broadcasted_iota(jnp.int32, kn_ref.shape, 1)
        kn = jnp.zeros(kn_ref.shape, jnp.float32)
        for n in range(n_kv):
            chunk = jnp.where(lane_k < HEAD_DIM, sq_max[:, n * LANES:(n + 1) * LANES], 0.0)
            kn = jnp.where(head_lane // KV_GROUP == n, jnp.sum(chunk, axis=1, keepdims=True), kn)
        kn_ref[...] = kn

    n_kt = (i * tq + tq + tk - 1) // tk
    neg_inf = jnp.float32(-jnp.inf)
    row = i * tq + lax.broadcasted_iota(jnp.int32, (tq, tk), 0)
    col0 = lax.broadcasted_iota(jnp.int32, (tq, tk), 1)
    wi = wi_ref[...]
    for hh in range(n_idx):
        wb_ref[hh] = jnp.broadcast_to(wi[:, hh:hh + 1], (tq, LANES))

    def raw_scores(kt):
        ks = pl.ds(pl.multiple_of(kt * tk, tk), tk)
        acc = jnp.zeros((tq, tk), jnp.float32)
        for hh in range(n_idx):
            qc = qi_ref[:, (hh // 2) * LANES:(hh // 2 + 1) * LANES]
            kc = ki_ref[ks, (hh % 2) * LANES:(hh % 2 + 1) * LANES]
            acc = acc + jnp.maximum(_dot_nt(qc, kc), 0.0) * jnp.concatenate([wb_ref[hh]] * (tk // LANES), axis=1)
        return acc

    def mask_tile(kt, carry):
        rmax, rmin = carry
        causal = (col0 + kt * tk) <= row
        masked = jnp.where(causal, stage_ref[...], neg_inf)
        sc_ref[kt] = masked
        masked_t = masked.T
        sct_ref[kt] = masked_t
        sct16_ref[kt] = masked_t.astype(jnp.bfloat16)
        rmax = jnp.maximum(rmax, _row_fold(masked_t, jnp.maximum))
        rmin = jnp.minimum(rmin, _row_fold(jnp.where(masked_t == neg_inf, -neg_inf, masked_t), jnp.minimum))
        return rmax, rmin

    def score_step(kt, carry):
        carry = mask_tile(kt - 1, carry)
        stage_ref[...] = raw_scores(kt)
        return carry

    stage_ref[...] = raw_scores(0)
    extremes = lax.fori_loop(1, n_kt, score_step,
                             (jnp.full((FOLD_ROWS, tq), neg_inf), jnp.full((FOLD_ROWS, tq), -neg_inf)))
    rmax, rmin = mask_tile(n_kt - 1, extremes)

    kf = jnp.float32(k_sel)

    def sweep(fn, init, x):
        def body(kt, c):
            for j in range(tk // FOLD_ROWS):
                c = fn(c, sct_ref[kt, j * FOLD_ROWS:(j + 1) * FOLD_ROWS, :], x)
            return c
        return lax.fori_loop(0, n_kt, body, tuple(jnp.full((FOLD_ROWS, tq), v, jnp.float32) for v in init))

    def col_sum(part):
        return jnp.sum(part, axis=0, keepdims=True)

    def col_max(part):
        return jnp.max(part, axis=0, keepdims=True)

    def count_ge_coarse(x16):
        one, zero = jnp.bfloat16(1), jnp.bfloat16(0)

        def body(kt, c):
            for j in range(tk // FOLD_ROWS):
                c = c + jnp.where(sct16_ref[kt, j * FOLD_ROWS:(j + 1) * FOLD_ROWS, :] >= x16, one, zero)
            return c
        part = lax.fori_loop(0, n_kt, body, jnp.zeros((FOLD_ROWS, tq), jnp.bfloat16))
        return col_sum(part.astype(jnp.float32))

    def bisect_coarse(_, carry):
        lo, hi = carry
        mid16 = (0.5 * lo + 0.5 * hi).astype(jnp.bfloat16)
        mid = mid16.astype(jnp.float32)
        ge = count_ge_coarse(mid16) >= kf
        return jnp.where(ge, mid, lo), jnp.where(ge, hi, mid)

    rmin = -col_max(-rmin)
    rmax = col_max(rmax)
    lo, hi = lax.fori_loop(0, COARSE_ITERS, bisect_coarse, (rmin, rmax))
    lo = lo - (jnp.abs(lo) * BF16_ULP + jnp.float32(1e-30))

    def rebase(kt, c):
        for j in range(tk // FOLD_ROWS):
            rows = slice(j * FOLD_ROWS, (j + 1) * FOLD_ROWS)
            sct16_ref[kt, rows, :] = (sct_ref[kt, rows, :] - lo).astype(jnp.bfloat16)
        return c

    lax.fori_loop(0, n_kt, rebase, 0)
    width = (hi - lo) * (1.0 + BF16_ULP)
    dlo, dhi = lax.fori_loop(0, RESIDUAL_ITERS, bisect_coarse, (jnp.zeros_like(width), width))
    slack = (jnp.abs(lo) + dhi) * F32_SLACK
    hi = jnp.where(dhi < width, jnp.minimum(hi, lo + dhi + slack), hi)

    few = (i * tq + lax.broadcasted_iota(jnp.int32, (1, tq), 1)) < k_sel

    def max_le(x):
        (c,) = sweep(lambda c, s, x: (jnp.maximum(c[0], jnp.where(s <= x, s, neg_inf)),), (-jnp.inf,), x)
        return col_max(c)

    def probe(x):
        def fn(c, s, x):
            ge = s >= x
            return c[0] + jnp.where(ge, 1.0, 0.0), jnp.maximum(c[1], jnp.where(ge, neg_inf, s))
        cnt, nxt = sweep(fn, (0.0, -jnp.inf), x)
        return col_sum(cnt), col_max(nxt)

    cand0 = jnp.where(few, rmin, max_le(hi))
    cnt0, nxt0 = probe(cand0)

    def unresolved(cnt):
        return jnp.logical_and(jnp.logical_not(few), cnt < kf)

    def finish_cond(c):
        _, cnt, _ = c
        return jnp.max(jnp.where(unresolved(cnt), 1.0, 0.0)) > 0.0

    def finish_body(c):
        cand, cnt, nxt = c
        cand = jnp.where(unresolved(cnt), nxt, cand)
        cnt, nxt = probe(cand)
        return cand, cnt, nxt

    thr_t, cnt_ge, _ = lax.while_loop(finish_cond, finish_body, (cand0, cnt0, nxt0))

    def to_rows(x):
        return jnp.broadcast_to(x, (LANES, tq)).T

    thr = to_rows(thr_t)

    tied_t = jnp.logical_and(jnp.logical_not(few), cnt_ge > kf)

    @pl.when(jnp.max(jnp.where(tied_t, 1.0, 0.0)) > 0.0)
    def _():
        (c,) = sweep(lambda c, s, x: (c[0] + jnp.where(s > x, 1.0, 0.0),), (0.0,), thr_t)
        need = to_rows(kf - col_sum(c))[:, :1]
        thr1 = thr[:, :1]
        tied1 = to_rows(jnp.where(tied_t, 1.0, 0.0))[:, :1] > 0.0
        tri = (lax.broadcasted_iota(jnp.int32, (tk, tk), 0)
               <= lax.broadcasted_iota(jnp.int32, (tk, tk), 1)).astype(jnp.bfloat16)

        def drop_body(kt, run):
            s = sc_ref[kt]
            eq = jnp.logical_and(s == thr1, tied1)
            eqf = jnp.where(eq, 1.0, 0.0)
            rank = run + _dot(eqf.astype(jnp.bfloat16), tri)
            sc_ref[kt] = jnp.where(jnp.logical_and(eq, rank > need), neg_inf, s)
            return run + jnp.sum(eqf, axis=1, keepdims=True)

        lax.fori_loop(0, n_kt, drop_body, jnp.zeros((tq, 1), jnp.float32))

    qf = q_ref[...].astype(jnp.float32)
    n_q = qf.shape[1]
    head_of_col = lax.broadcasted_iota(jnp.int32, (n_q, LANES), 0) // HEAD_DIM
    head_sel = jnp.where(head_of_col == lax.broadcasted_iota(jnp.int32, (n_q, LANES), 1), 1.0, 0.0)
    qss = _dot((qf * qf).astype(jnp.bfloat16), head_sel.astype(jnp.bfloat16))
    bound = jnp.sqrt(jnp.max(qss * kn_ref[0:1, :], axis=1, keepdims=True)) * BOUND_SLACK
    shift = jnp.where(lane == HEAD_DIM, -bound, 0.0)
    for c in range(n_heads // 2):
        qc = qf[:, c * LANES:(c + 1) * LANES]
        for half in range(2):
            h = 2 * c + half
            n, g = h // KV_GROUP, h % KV_GROUP
            x = qc if half == 0 else pltpu.roll(qc, HEAD_DIM, axis=1)
            qx_ref[n, g * tq:(g + 1) * tq, :] = jnp.where(is_head, x, shift).astype(qx_ref.dtype)

    thr_tile = jnp.concatenate([thr] * (tk // LANES), axis=1)

    def attend(online):
        acc_ref[...] = jnp.zeros_like(acc_ref)
        if online:
            m_ref[...] = jnp.full_like(m_ref, neg_inf)

        def att_block(kt, width):
            ks = pl.ds(pl.multiple_of(kt * tk, tk), width)
            keep = sc_ref[kt, :, :width] >= thr_tile[:, :width]
            keep_b = jnp.where(keep, 1.0, 0.0).astype(jnp.bfloat16)
            for n in range(n_kv):
                s = _dot_nt(qx_ref[n], k_ref[ks, n * LANES:(n + 1) * LANES])
                vc = v_ref[ks, n * LANES:(n + 1) * LANES]
                if online:
                    s = jnp.where(jnp.concatenate([keep] * KV_GROUP, axis=0), s, neg_inf)
                    m_old = m_ref[n]
                    m_new = jnp.maximum(m_old, jnp.max(s, axis=1, keepdims=True))
                    m_safe = jnp.where(m_new == neg_inf, 0.0, m_new)
                    p = jnp.exp2(s - m_safe).astype(jnp.bfloat16)
                    acc_ref[n] = jnp.exp2(m_old - m_safe) * acc_ref[n] + _dot(p, vc)
                    m_ref[n] = m_new
                else:
                    p = jnp.exp2(s).astype(jnp.bfloat16).reshape(KV_GROUP, tq, width) * keep_b[None]
                    acc_ref[n] += _dot(p.reshape(KV_GROUP * tq, width), vc)

        def att_tile(kt, carry):
            att_block(kt, tk)
            return carry

        n_full = (i * tq + tq) // tk
        lax.fori_loop(0, n_full, att_tile, 0)
        rest = (i * tq + tq) % tk
        for part in range(1, tk // tq):
            @pl.when(rest == part * tq)
            def _(part=part):
                att_block(n_full, part * tq)

    def normalise():
        least = None
        for c in range(n_heads // 2):
            a0, a1 = (acc_ref[h // KV_GROUP, (h % KV_GROUP) * tq:(h % KV_GROUP + 1) * tq, :]
                      for h in (2 * c, 2 * c + 1))
            num = jnp.where(is_head, a0, pltpu.roll(a1, HEAD_DIM, axis=1))
            den = jnp.where(is_head, pltpu.roll(a0, HEAD_DIM, axis=1), a1)
            o_ref[:, c * LANES:(c + 1) * LANES] = (num / den).astype(o_ref.dtype)
            least = den if least is None else jnp.minimum(least, den)
        return jnp.min(least)

    attend(False)

    @pl.when(jnp.logical_not(normalise() > MIN_DENOMINATOR))
    def _():
        attend(True)
        normalise()


def _dsa(q, qi, wi, kx, vx, kix, bsz, seq, k_sel):
    t, n_q = q.shape
    n_heads = n_q // HEAD_DIM
    n_kv = n_heads // KV_GROUP
    n_idx = qi.shape[1] // IDX_DIM
    tq = min(Q_TILE, seq)
    tk = min(K_TILE, seq)
    nq = seq // tq
    assert seq % tk == 0 and tk % tq == 0 and tk % FOLD_ROWS == 0 and tq % LANES == 0
    assert seq // FOLD_ROWS <= 256, "bf16 partial counts must stay exactly representable"
    row = lambda b, i: (b * nq + i, 0)
    per_batch = lambda b, i: (b, 0)
    kern = functools.partial(_dsa_kernel, tq=tq, tk=tk, k_sel=k_sel, n_heads=n_heads, n_idx=n_idx)
    return pl.pallas_call(
        kern,
        out_shape=jax.ShapeDtypeStruct((t, n_q), jnp.bfloat16),
        grid=(bsz, nq),
        in_specs=[
            pl.BlockSpec((tq, n_q), row),
            pl.BlockSpec((tq, qi.shape[1]), row),
            pl.BlockSpec((tq, LANES), row),
            pl.BlockSpec((seq, kx.shape[1]), per_batch),
            pl.BlockSpec((seq, vx.shape[1]), per_batch),
            pl.BlockSpec((seq, kix.shape[1]), per_batch),
        ],
        out_specs=pl.BlockSpec((tq, n_q), row),
        scratch_shapes=[
            pltpu.VMEM((seq // tk, tq, tk), jnp.float32),
            pltpu.VMEM((seq // tk, tk, tq), jnp.float32),
            pltpu.VMEM((seq // tk, tk, tq), jnp.bfloat16),
            pltpu.VMEM((tq, tk), jnp.float32),
            pltpu.VMEM((n_idx, tq, LANES), jnp.float32),
            pltpu.VMEM((n_kv, KV_GROUP * tq, LANES), jnp.bfloat16),
            pltpu.VMEM((n_kv, KV_GROUP * tq, LANES), jnp.float32),
            pltpu.VMEM((n_kv, KV_GROUP * tq, 1), jnp.float32),
            pltpu.VMEM((SUBLANES, LANES), jnp.float32),
        ],
        compiler_params=_params("parallel", "arbitrary"),
        name="dsa",
    )(q, qi, wi, kx, vx, kix)


def _rope_tables(seq):
    half = HEAD_DIM // 2
    inv_freq = ROPE_THETA ** (-jnp.arange(half, dtype=jnp.float32) * 2.0 / HEAD_DIM)
    ang = jnp.arange(seq, dtype=jnp.float32)[:, None] * inv_freq[None, :]
    cos = jnp.concatenate([jnp.cos(ang)] * 4, axis=-1)
    sin = jnp.concatenate([-jnp.sin(ang), jnp.sin(ang)] * 2, axis=-1)
    return cos, sin


def _att_weights(w_in, d_model):
    n_heads = d_model // HEAD_DIM
    n_kv = n_heads // KV_GROUP
    n_idx = max(4, d_model // 128)
    n_q, n_qi = n_heads * HEAD_DIM, n_idx * IDX_DIM
    cols = n_q + 2 * n_kv * HEAD_DIM + n_qi + IDX_DIM + n_idx
    assert w_in.shape[1] == cols and IDX_DIM + n_idx <= LANES and n_kv % 2 == 0
    padded = -(-cols // LANES) * LANES
    w_cat = jnp.pad(w_in, ((0, 0), (0, padded - cols))).astype(jnp.bfloat16)
    return w_cat, n_q, n_kv, n_qi, n_idx


def kernel(x, norm_g, mlp_w1, mlp_w2, ssm_w_in, ssm_lam_re, ssm_lam_im, ssm_log_dt, ssm_b_re, ssm_b_im,
           ssm_c_re, ssm_c_im, ssm_d, ssm_w_glu, ssm_w_out, att_w_in, att_w_out):
    bsz, seq, d_model = x.shape
    depth = norm_g.shape[0]
    bf = jnp.bfloat16
    h = x.reshape(bsz * seq, d_model)
    cos, sin = _rope_tables(seq)
    k_sel = min(TOPK_MAX, seq // 4)
    for i in range(depth):
        j = i // 2
        g = norm_g[i][:, None, :]
        if i % 2 == 0:
            ar, ai, bmat, cmat, dvec = _ssm_discretise(
                ssm_lam_re[j], ssm_lam_im[j], ssm_log_dt[j], ssm_b_re[j], ssm_b_im[j],
                ssm_c_re[j], ssm_c_im[j], ssm_d[j], bsz)
            mixed = _ssm_scan(h.reshape(bsz, seq, d_model), g[0], ssm_w_in[j].astype(bf), ar, ai, bmat, cmat, dvec)
            w_glu, w_out = ssm_w_glu[j].astype(bf), ssm_w_out[j].astype(bf)
        else:
            w_cat, n_q, n_kv, n_qi, n_idx = _att_weights(att_w_in[j], d_model)
            q, kx, vx, qi, kix, wi = _att_in(h, g[0], w_cat, cos, sin, bsz, seq, n_q, n_kv, n_qi,
                                             HEAD_DIM ** -0.5 * LOG2_E, n_idx ** -0.5 * IDX_DIM ** -0.5)
            mixed = _dsa(q, qi, wi, kx, vx, kix, bsz, seq, k_sel)
            w_glu, w_out = None, att_w_out[j].astype(bf)
        h = _mix_mlp(mixed, h, norm_g[i][1:], w_glu, w_out, mlp_w1[i].astype(bf), mlp_w2[i].astype(bf),
                     seq, time_major=(i % 2 == 0))
    return h.reshape(bsz, seq, d_model)
```
